```python
import jax, jax.numpy as jnp
from jax import lax
import numpy as np

D_MODEL = 1024
BATCH = 4
SEQ = 8192
DEPTH = 2

CHUNK = 64
RMS_EPS = 1e-5
CONV_DIM = D_MODEL
CONV_WIDTH = 3
HEAD_DIM = 64
N_Q_HEADS = D_MODEL // HEAD_DIM
N_KV_HEADS = 2
GQA_GROUP = N_Q_HEADS // N_KV_HEADS
WINDOW = 128
WINDOW_CHUNKS = WINDOW // CHUNK
ROT_DIM = HEAD_DIM // 4
ROPE_THETA = 500000.0
Q_W = N_Q_HEADS * HEAD_DIM
KV_W = N_KV_HEADS * HEAD_DIM
SPLITS = [CONV_DIM, 2 * CONV_DIM, 3 * CONV_DIM, 3 * CONV_DIM + Q_W, 3 * CONV_DIM + Q_W + KV_W, 3 * CONV_DIM + Q_W + 2 * KV_W]
IN_COLS = 3 * CONV_DIM + Q_W + 2 * KV_W + 2 * D_MODEL
N_GROUPS = 8
EXPERTS_PER_GROUP = 8
N_EXPERTS = N_GROUPS * EXPERTS_PER_GROUP
EXPERT_FF = 512
TOP_K = 2
MOE_BLOCK = 256

kernel_name = 'hybrid_conv_swa_hmoe_block'


def rms_norm(x, g):
    xf = x.astype(jnp.float32)
    y = xf * lax.rsqrt(jnp.mean(xf * xf, axis=-1, keepdims=True) + RMS_EPS)
    return (y * g.astype(jnp.float32)).astype(x.dtype)


def rotary_tables(positions):
    inv = ROPE_THETA ** (-jnp.arange(0, ROT_DIM, 2, dtype=jnp.float32) / ROT_DIM)
    ang = positions.astype(jnp.float32)[..., None] * inv
    return jnp.cos(ang), jnp.sin(ang)


def apply_partial_rotary(t, cos, sin):
    half = ROT_DIM // 2
    r1, r2, rest = t[..., :half], t[..., half:ROT_DIM], t[..., ROT_DIM:]
    c = cos[:, :, None, :].astype(t.dtype)
    s = sin[:, :, None, :].astype(t.dtype)
    return jnp.concatenate([r1 * c - r2 * s, r2 * c + r1 * s, rest], axis=-1)


def short_conv_mixer(cb, cc, cx, conv_w, w_conv_out):
    u = cb * cx
    y = lax.conv_general_dilated(u, conv_w[:, None, :], window_strides=(1,), padding=[(CONV_WIDTH - 1, 0)], dimension_numbers=('NWC', 'WIO', 'NWC'), feature_group_count=CONV_DIM)
    return (cc * y) @ w_conv_out


def sliding_window_sink_attention(q, k, v, sinks):
    b, s = q.shape[0], q.shape[1]
    nc = s // CHUNK
    qc = q.reshape(b, nc, CHUNK, N_KV_HEADS, GQA_GROUP, HEAD_DIM)

    def band(t):
        tc = t.reshape(b, nc, CHUNK, N_KV_HEADS, HEAD_DIM)
        tp = jnp.pad(tc, ((0, 0), (WINDOW_CHUNKS, 0), (0, 0), (0, 0), (0, 0)))
        return jnp.concatenate([tp[:, j:j + nc] for j in range(WINDOW_CHUNKS + 1)], axis=2)

    kb, vb = band(k), band(v)
    scores = jnp.einsum('bcqkgd,bcskd->bckgqs', qc, kb, preferred_element_type=jnp.float32) * (HEAD_DIM ** -0.5)
    key_chunk = jnp.arange(nc)[:, None] - WINDOW_CHUNKS + jnp.arange(WINDOW_CHUNKS + 1)[None, :]
    valid = jnp.repeat(key_chunk >= 0, CHUNK, axis=1)
    scores = jnp.where(valid[None, :, None, None, None, :], scores, -jnp.inf)
    sink = sinks.astype(jnp.float32).reshape(N_KV_HEADS, GQA_GROUP)[None, None, :, :, None, None]
    m = jnp.maximum(jnp.max(scores, axis=-1, keepdims=True), sink)
    p = jnp.exp(scores - m)
    p = p / (jnp.sum(p, axis=-1, keepdims=True) + jnp.exp(sink - m))
    out = jnp.einsum('bckgqs,bcskd->bcqkgd', p.astype(v.dtype), vb)
    return out.reshape(b, s, Q_W)


def mixer_block(h, cos, sin, w_in, b_gate, conv_w, sinks, w_conv_out, w_attn_out, w_o):
    b, s, _ = h.shape
    proj = h @ w_in
    cb, cc, cx, q, k, v, gates = jnp.split(proj, SPLITS, axis=-1)
    conv_out = short_conv_mixer(cb, cc, cx, conv_w, w_conv_out)
    q = apply_partial_rotary(q.reshape(b, s, N_Q_HEADS, HEAD_DIM), cos, sin)
    k = apply_partial_rotary(k.reshape(b, s, N_KV_HEADS, HEAD_DIM), cos, sin)
    v = v.reshape(b, s, N_KV_HEADS, HEAD_DIM)
    attn_out = sliding_window_sink_attention(q, k, v, sinks) @ w_attn_out
    g = jax.nn.sigmoid(gates + b_gate)
    g_conv, g_attn = g[..., :D_MODEL], g[..., D_MODEL:]
    return (g_conv * conv_out + g_attn * attn_out) @ w_o


def hierarchical_moe(h, w_group, b_group, w_route, b_route, w1, w3, w2):
    b, s, d = h.shape
    t = h.reshape(-1, d)
    n_tok = t.shape[0]
    group_logits = (t @ w_group).astype(jnp.float32) + b_group.astype(jnp.float32)
    group_probs = jax.nn.softmax(group_logits, axis=-1)
    g_top = jnp.argmax(group_logits, axis=-1)
    p_group = jnp.take_along_axis(group_probs, g_top[:, None], axis=-1)
    exp_logits = jnp.einsum('td,gde->tge', t, w_route).astype(jnp.float32) + b_route.astype(jnp.float32)
    exp_logits = jnp.take_along_axis(exp_logits, g_top[:, None, None], axis=1)[:, 0]
    top_p, top_e = lax.top_k(jax.nn.softmax(exp_logits, axis=-1), TOP_K)
    top_p = top_p / jnp.sum(top_p, axis=-1, keepdims=True)
    weights = p_group * top_p
    expert_ids = g_top[:, None] * EXPERTS_PER_GROUP + top_e
    n_assign = n_tok * TOP_K
    flat_e = expert_ids.reshape(-1).astype(jnp.int32)
    flat_w = weights.reshape(-1)
    flat_tok = jnp.repeat(jnp.arange(n_tok, dtype=jnp.int32), TOP_K)
    order = jnp.argsort(flat_e)
    se, sw, stok = flat_e[order], flat_w[order], flat_tok[order]
    counts = jnp.bincount(flat_e, length=N_EXPERTS)
    padded = (counts + MOE_BLOCK - 1) // MOE_BLOCK * MOE_BLOCK
    start = jnp.cumsum(counts) - counts
    pad_end = jnp.cumsum(padded)
    pad_start = pad_end - padded
    dest = pad_start[se] + jnp.arange(n_assign) - start[se]
    n_rows = -(-n_assign // MOE_BLOCK) * MOE_BLOCK + N_EXPERTS * MOE_BLOCK
    n_blocks = n_rows // MOE_BLOCK
    row_tok = jnp.zeros((n_rows,), jnp.int32).at[dest].set(stok)
    row_w = jnp.zeros((n_rows,), jnp.float32).at[dest].set(sw)
    block_expert = jnp.minimum(jnp.searchsorted(pad_end, jnp.arange(n_blocks) * MOE_BLOCK, side='right'), N_EXPERTS - 1)
    xs = t[row_tok].reshape(n_blocks, MOE_BLOCK, d)

    def expert_block(args):
        xb, e = args
        hid = jax.nn.silu(xb @ w1[e]) * (xb @ w3[e])
        return hid @ w2[e]

    ys = lax.map(expert_block, (xs, block_expert)).reshape(n_rows, d)
    out = jax.ops.segment_sum(ys * row_w[:, None].astype(ys.dtype), row_tok, num_segments=n_tok)
    return out.reshape(b, s, d)


def setup_inputs(seed: int = 0) -> dict:
    key = jax.random.key(seed)
    ks = jax.random.split(key, 20)
    f32 = jnp.float32

    def nrm(k, shape, scale):
        return jax.random.normal(k, shape, f32) * scale

    x = nrm(ks[0], (BATCH, SEQ, D_MODEL), 1.0)
    offsets = jax.random.randint(ks[1], (BATCH, 1), 0, 64, dtype=jnp.int32) * CHUNK
    positions = offsets + jnp.arange(SEQ, dtype=jnp.int32)[None, :]
    return {
        'x': x,
        'positions': positions,
        'norm1_g': 1.0 + nrm(ks[2], (DEPTH, D_MODEL), 0.02),
        'w_in': nrm(ks[3], (DEPTH, D_MODEL, IN_COLS), D_MODEL ** -0.5),
        'b_gate': nrm(ks[4], (DEPTH, 2 * D_MODEL), 0.02),
        'conv_w': nrm(ks[5], (DEPTH, CONV_WIDTH, CONV_DIM), CONV_WIDTH ** -0.5),
        'sinks': nrm(ks[6], (DEPTH, N_Q_HEADS), 0.5),
        'w_conv_out': nrm(ks[7], (DEPTH, CONV_DIM, D_MODEL), CONV_DIM ** -0.5),
        'w_attn_out': nrm(ks[8], (DEPTH, Q_W, D_MODEL), Q_W ** -0.5),
        'w_o': nrm(ks[9], (DEPTH, D_MODEL, D_MODEL), D_MODEL ** -0.5),
        'norm2_g': 1.0 + nrm(ks[10], (DEPTH, D_MODEL), 0.02),
        'w_group': nrm(ks[11], (DEPTH, D_MODEL, N_GROUPS), D_MODEL ** -0.5),
        'b_group': nrm(ks[12], (DEPTH, N_GROUPS), 0.01),
        'w_route': nrm(ks[13], (DEPTH, N_GROUPS, D_MODEL, EXPERTS_PER_GROUP), D_MODEL ** -0.5),
        'b_route': nrm(ks[14], (DEPTH, N_GROUPS, EXPERTS_PER_GROUP), 0.01),
        'w1': nrm(ks[15], (DEPTH, N_EXPERTS, D_MODEL, EXPERT_FF), D_MODEL ** -0.5),
        'w3': nrm(ks[16], (DEPTH, N_EXPERTS, D_MODEL, EXPERT_FF), D_MODEL ** -0.5),
        'w2': nrm(ks[17], (DEPTH, N_EXPERTS, EXPERT_FF, D_MODEL), EXPERT_FF ** -0.5),
        'final_g': 1.0 + nrm(ks[18], (D_MODEL,), 0.02),
    }


def reference(x, positions, norm1_g, w_in, b_gate, conv_w, sinks, w_conv_out, w_attn_out, w_o, norm2_g, w_group, b_group, w_route, b_route, w1, w3, w2, final_g):
    cos, sin = rotary_tables(positions)
    h = x
    for l in range(DEPTH):
        h = h + mixer_block(rms_norm(h, norm1_g[l]), cos, sin, w_in[l], b_gate[l], conv_w[l], sinks[l], w_conv_out[l], w_attn_out[l], w_o[l])
        h = h + hierarchical_moe(rms_norm(h, norm2_g[l]), w_group[l], b_group[l], w_route[l], b_route[l], w1[l], w3[l], w2[l])
    return rms_norm(h, final_g)
```

```python
import functools

import jax
import jax.numpy as jnp
from jax import lax
from jax.experimental import pallas as pl
from jax.experimental.pallas import tpu as pltpu

F32 = jnp.float32
BF16 = jnp.bfloat16

D_MODEL = 1024
RMS_EPS = 1e-5
CHUNK = 64
WINDOW_CHUNKS = 2
HEAD_DIM = 64
N_Q_HEADS = 16
N_KV_HEADS = 2
ROT_DIM = 16
ROPE_THETA = 500000.0
N_GROUPS = 8
EXPERTS_PER_GROUP = 8
N_EXPERTS = 64
EXPERT_FF = 512
MOE_BLOCK = 256
LANES = 128

W_CB, W_CC, W_CX, W_Q, W_K, W_G = 0, 1024, 2048, 3072, 4096, 4352
P_U, P_CC, P_Q, P_KV, P_G = 0, 1024, 2048, 3072, 3328
P_COLS = 5376
HALO = WINDOW_CHUNKS * CHUNK
KEYS = HALO + CHUNK

VMEM_BIG = 56 * 1024 * 1024


def _rms(x, g):
    ms = jnp.mean(x * x, axis=-1, keepdims=True)
    return x * lax.rsqrt(ms + RMS_EPS) * g


def _proj_kernel(x_ref, g_ref, w_ref, bg_ref, ra_ref, rb_ref, rc_ref, o_ref):
    xn = _rms(x_ref[...], g_ref[...]).astype(BF16)

    def mm(c0, n):
        return jnp.dot(xn, w_ref[:, c0:c0 + n], preferred_element_type=F32)

    ra, rb, rc = ra_ref[...], rb_ref[...], rc_ref[...]

    def rot(t):
        return t * ra + pltpu.roll(t, LANES - ROT_DIM // 2, 1) * rb + pltpu.roll(t, ROT_DIM // 2, 1) * rc

    nc = 512
    for j in range(2):
        cb = mm(W_CB + j * nc, nc)
        cx = mm(W_CX + j * nc, nc)
        o_ref[:, P_U + j * nc:P_U + (j + 1) * nc] = (cb * cx).astype(BF16)
    for j in range(2):
        o_ref[:, P_CC + j * nc:P_CC + (j + 1) * nc] = mm(W_CC + j * nc, nc).astype(BF16)
    for j in range(2):
        q = mm(W_Q + j * nc, nc)
        for p in range(nc // LANES):
            t = rot(q[:, p * LANES:(p + 1) * LANES]) * (HEAD_DIM ** -0.5)
            o_ref[:, P_Q + j * nc + p * LANES:P_Q + j * nc + (p + 1) * LANES] = t.astype(BF16)
    kv = mm(W_K, 2 * LANES)
    o_ref[:, P_KV:P_KV + LANES] = rot(kv[:, :LANES]).astype(BF16)
    o_ref[:, P_KV + LANES:P_KV + 2 * LANES] = kv[:, LANES:].astype(BF16)
    for j in range(4):
        gt = mm(W_G + j * nc, nc) + bg_ref[:, j * nc:(j + 1) * nc]
        o_ref[:, P_G + j * nc:P_G + (j + 1) * nc] = jax.nn.sigmoid(gt).astype(BF16)


def _proj(x2, g1, w_in, b_gate, ra, rb, rc, tm):
    t = x2.shape[0]
    row = lambda i: (i, 0)
    fixed = lambda i: (0, 0)
    return pl.pallas_call(
        _proj_kernel,
        grid=(t // tm,),
        in_specs=[
            pl.BlockSpec((tm, D_MODEL), row),
            pl.BlockSpec((1, D_MODEL), fixed),
            pl.BlockSpec(w_in.shape, fixed, pipeline_mode=pl.Buffered(1)),
            pl.BlockSpec((1, 2 * D_MODEL), fixed),
            pl.BlockSpec((tm, LANES), row),
            pl.BlockSpec((tm, LANES), row),
            pl.BlockSpec((tm, LANES), row),
        ],
        out_specs=pl.BlockSpec((tm, P_COLS), row),
        out_shape=jax.ShapeDtypeStruct((t, P_COLS), BF16),
        compiler_params=pltpu.CompilerParams(dimension_semantics=("arbitrary",), vmem_limit_bytes=VMEM_BIG),
        name="proj",
    )(x2, g1, w_in, b_gate, ra, rb, rc)


def _mix_kernel(tm, tiles_per_seq, p_ref, hkv_ref, hu_ref, x_ref, cw_ref, sink_ref, wco_ref, wao_ref, wo_ref,
                o_ref, ke_s, ko_s, ve_s, vo_s, attn_s, ci_s):
    first = (pl.program_id(0) % tiles_per_seq) == 0
    lo = lax.broadcasted_iota(jnp.int32, (1, LANES), 1) < HEAD_DIM

    def stage(rows, kv):
        k = kv[:, :LANES].astype(F32)
        v = kv[:, LANES:].astype(F32)
        for src, e_s, o_s in ((k, ke_s, ko_s), (v, ve_s, vo_s)):
            sr = pltpu.roll(src, HEAD_DIM, 1)
            e_s[0, rows, :] = jnp.where(lo, src, 0.0).astype(BF16)
            o_s[0, rows, :] = jnp.where(lo, 0.0, sr).astype(BF16)
            e_s[1, rows, :] = jnp.where(lo, sr, 0.0).astype(BF16)
            o_s[1, rows, :] = jnp.where(lo, 0.0, src).astype(BF16)

    stage(slice(0, HALO), hkv_ref[...])
    stage(slice(HALO, HALO + tm), p_ref[:, P_KV:P_KV + 2 * LANES])

    col = lax.broadcasted_iota(jnp.int32, (1, KEYS), 1)

    def chunk_body(c, carry):
        r0 = pl.multiple_of(c * CHUNK, CHUNK)
        nbad = jnp.where(first, jnp.maximum(HALO - c * CHUNK, 0), 0)
        bad = col < nbad
        for g in range(N_KV_HEADS):
            qbase = P_Q + g * 4 * LANES
            q = jnp.concatenate(
                [p_ref[pl.ds(r0, CHUNK), qbase + p * LANES:qbase + (p + 1) * LANES] for p in range(4)], axis=0)
            sk = sink_ref[g]
            acc = None
            for k_s, v_s, sc in ((ke_s, ve_s, sk[:, 0:1]), (ko_s, vo_s, sk[:, HEAD_DIM:HEAD_DIM + 1])):
                kk = k_s[g, pl.ds(r0, KEYS), :]
                vv = v_s[g, pl.ds(r0, KEYS), :]
                s = lax.dot_general(q, kk, (((1,), (1,)), ((), ())), preferred_element_type=F32)
                s = jnp.where(bad, -jnp.inf, s)
                m = jnp.maximum(jnp.max(s, axis=1, keepdims=True), sc)
                p = jnp.exp(s - m)
                l = jnp.sum(p, axis=1, keepdims=True) + jnp.exp(sc - m)
                o = jnp.dot(p.astype(BF16), vv, preferred_element_type=F32) / l
                acc = o if acc is None else acc + o
            for p in range(4):
                attn_s[pl.ds(r0, CHUNK), (g * 4 + p) * LANES:(g * 4 + p + 1) * LANES] = (
                    acc[p * CHUNK:(p + 1) * CHUNK].astype(BF16))
        return carry

    lax.fori_loop(0, tm // CHUNK, chunk_body, 0)

    hu = hu_ref[...].astype(F32)
    row = lax.broadcasted_iota(jnp.int32, (tm, 1), 0)
    cw = cw_ref[...]
    nc = 256
    for j in range(D_MODEL // nc):
        cs = slice(j * nc, (j + 1) * nc)
        u = p_ref[:, P_U + j * nc:P_U + (j + 1) * nc].astype(F32)
        h1 = jnp.where(first, 0.0, hu[15:16, cs])
        h2 = jnp.where(first, 0.0, hu[14:15, cs])
        s1 = jnp.where(row == 0, h1, pltpu.roll(u, 1, 0))
        s2 = jnp.where(row == 0, h2, jnp.where(row == 1, h1, pltpu.roll(u, 2, 0)))
        y = cw[2:3, cs] * u + cw[1:2, cs] * s1 + cw[0:1, cs] * s2
        cc = p_ref[:, P_CC + j * nc:P_CC + (j + 1) * nc].astype(F32)
        ci_s[:, cs] = (cc * y).astype(BF16)

    conv_out = jnp.dot(ci_s[...], wco_ref[...], preferred_element_type=F32)
    attn_out = jnp.dot(attn_s[...], wao_ref[...], preferred_element_type=F32)
    gc = p_ref[:, P_G:P_G + D_MODEL].astype(F32)
    ga = p_ref[:, P_G + D_MODEL:P_G + 2 * D_MODEL].astype(F32)
    merged = (gc * conv_out + ga * attn_out).astype(BF16)
    o_ref[...] = x_ref[...] + jnp.dot(merged, wo_ref[...], preferred_element_type=F32)


def _mix(p, x2, conv_w, sink_tab, w_co, w_ao, w_o, seq, tm):
    t = x2.shape[0]
    row = lambda i: (i, 0)
    fixed = lambda i: (0, 0)
    sq = (D_MODEL, D_MODEL)
    return pl.pallas_call(
        functools.partial(_mix_kernel, tm, seq // tm),
        grid=(t // tm,),
        in_specs=[
            pl.BlockSpec((tm, P_COLS), row),
            pl.BlockSpec((HALO, 2 * LANES), lambda i: (jnp.maximum(i * (tm // HALO) - 1, 0), P_KV // (2 * LANES))),
            pl.BlockSpec((16, D_MODEL), lambda i: (jnp.maximum(i * (tm // 16) - 1, 0), P_U // D_MODEL)),
            pl.BlockSpec((tm, D_MODEL), row),
            pl.BlockSpec(conv_w.shape, fixed),
            pl.BlockSpec(sink_tab.shape, lambda i: (0, 0, 0)),
            pl.BlockSpec(sq, fixed),
            pl.BlockSpec(sq, fixed),
            pl.BlockSpec(sq, fixed),
        ],
        out_specs=pl.BlockSpec((tm, D_MODEL), row),
        out_shape=jax.ShapeDtypeStruct((t, D_MODEL), F32),
        scratch_shapes=[pltpu.VMEM((N_KV_HEADS, HALO + tm, LANES), BF16) for _ in range(4)]
        + [pltpu.VMEM((tm, D_MODEL), BF16), pltpu.VMEM((tm, D_MODEL), BF16)],
        compiler_params=pltpu.CompilerParams(dimension_semantics=("arbitrary",), vmem_limit_bytes=VMEM_BIG),
        name="mix",
    )(p, p, p, x2, conv_w, sink_tab, w_co, w_ao, w_o)


R_E0, R_E1, R_RANK0, R_RANK1, R_W0, R_W1 = range(6)
R_LANE0 = N_GROUPS


def _route_kernel(tm, h_ref, g_ref, wr_ref, br_ref, hn_ref, r_ref, cnt_ref, base_s):
    @pl.when(pl.program_id(0) == 0)
    def _():
        base_s[...] = jnp.zeros_like(base_s)

    hn = _rms(h_ref[...], g_ref[...])
    hn_ref[...] = hn
    lg = jnp.dot(hn, wr_ref[...], precision=lax.Precision.HIGHEST, preferred_element_type=F32) + br_ref[...]
    lane = lax.broadcasted_iota(jnp.int32, (tm, LANES), 1)
    neg = -jnp.inf

    def first_max(v):
        m = jnp.max(v, axis=1, keepdims=True)
        return m, jnp.min(jnp.where(v == m, lane, LANES), axis=1, keepdims=True)

    gl = jnp.where(lane < N_GROUPS, lg, neg)
    gmax, gtop = first_max(gl)
    pg = 1.0 / jnp.sum(jnp.exp(gl - gmax), axis=1, keepdims=True)
    in_group = (lane >= R_LANE0) & (lane < R_LANE0 + N_EXPERTS) & (((lane - R_LANE0) >> 3) == gtop)
    el = jnp.where(in_group, lg, neg)
    m1, i1 = first_max(el)
    m2, i2 = first_max(jnp.where(lane == i1, neg, el))
    e2 = jnp.exp(m2 - m1)
    den = 1.0 + e2
    w0 = pg * (1.0 / den)
    w1 = pg * (e2 / den)

    hit0 = lane == i1
    hit1 = lane == i2
    a = jnp.where(hit0 | hit1, 1.0, 0.0)
    rr = lax.broadcasted_iota(jnp.int32, (tm, tm), 0)
    cc = lax.broadcasted_iota(jnp.int32, (tm, tm), 1)
    tri = jnp.where(cc < rr, 1.0, 0.0).astype(BF16)
    before = jnp.dot(tri, a.astype(BF16), preferred_element_type=F32) + base_s[...]
    rank0 = jnp.sum(jnp.where(hit0, before, 0.0), axis=1, keepdims=True)
    rank1 = jnp.sum(jnp.where(hit1, before, 0.0), axis=1, keepdims=True)
    base_s[...] = base_s[...] + jnp.sum(a, axis=0, keepdims=True)
    cnt_ref[...] = base_s[...]

    vals = ((i1 - R_LANE0).astype(F32), (i2 - R_LANE0).astype(F32), rank0, rank1, w0, w1)
    slab = jnp.zeros((tm, LANES), F32)
    for k, v in enumerate(vals):
        slab = jnp.where(lane == k, v, slab)
    r_ref[...] = slab


def _route(h2, g2, w_router, b_router, tm):
    t = h2.shape[0]
    row = lambda i: (i, 0)
    fixed = lambda i: (0, 0)
    return pl.pallas_call(
        functools.partial(_route_kernel, tm),
        grid=(t // tm,),
        in_specs=[
            pl.BlockSpec((tm, D_MODEL), row),
            pl.BlockSpec((1, D_MODEL), fixed),
            pl.BlockSpec((D_MODEL, LANES), fixed),
            pl.BlockSpec((1, LANES), fixed),
        ],
        out_specs=[
            pl.BlockSpec((tm, D_MODEL), row),
            pl.BlockSpec((tm, LANES), row),
            pl.BlockSpec((1, LANES), fixed),
        ],
        out_shape=[
            jax.ShapeDtypeStruct((t, D_MODEL), F32),
            jax.ShapeDtypeStruct((t, LANES), F32),
            jax.ShapeDtypeStruct((1, LANES), F32),
        ],
        scratch_shapes=[pltpu.VMEM((1, LANES), F32)],
        compiler_params=pltpu.CompilerParams(dimension_semantics=("arbitrary",)),
        name="route",
    )(h2, g2, w_router, b_router)


def _rows_copy(src_ref, dst_ref, sem, n):
    return pltpu.make_async_copy(src_ref.at[pl.ds(0, n)], dst_ref.at[pl.ds(0, n)], sem)


def _dispatch_kernel(tm, n_tiles, zl_ref, nz_ref, d_ref, hn_ref, xs_ref, zbuf, zsem, sem):
    i = pl.program_id(0)

    @pl.when(i == 0)
    def _():
        zbuf[...] = jnp.zeros_like(zbuf)

        def zero_copy(j):
            r = pl.multiple_of(zl_ref[j] * MOE_BLOCK, MOE_BLOCK)
            return pltpu.make_async_copy(zbuf, xs_ref.at[pl.ds(r, MOE_BLOCK)], zsem)

        def start(j, c):
            zero_copy(j).start()
            return c

        def wait(j, c):
            zero_copy(j).wait()
            return c

        lax.fori_loop(0, nz_ref[0], start, 0)
        lax.fori_loop(0, nz_ref[0], wait, 0)

    base = i * tm

    def body(t, c):
        for k in range(2):
            d = d_ref[2 * t + k]
            pltpu.make_async_copy(hn_ref.at[pl.ds(base + t, 1)], xs_ref.at[pl.ds(d, 1)], sem).start()
        return c

    lax.fori_loop(0, tm, body, 0, unroll=8)

    @pl.when(i > 0)
    def _():
        _rows_copy(hn_ref, xs_ref, sem, 2 * tm).wait()

    @pl.when(i == n_tiles - 1)
    def _():
        _rows_copy(hn_ref, xs_ref, sem, 2 * tm).wait()


def _dispatch(hn, dest, zero_list, n_zero, n_rows, tm):
    t = hn.shape[0]
    n_tiles = t // tm
    return pl.pallas_call(
        functools.partial(_dispatch_kernel, tm, n_tiles),
        grid_spec=pltpu.PrefetchScalarGridSpec(
            num_scalar_prefetch=2,
            grid=(n_tiles,),
            in_specs=[
                pl.BlockSpec((2 * tm,), lambda i, zl, nz: (i,), memory_space=pltpu.SMEM),
                pl.BlockSpec(memory_space=pl.ANY),
            ],
            out_specs=pl.BlockSpec(memory_space=pl.ANY),
            scratch_shapes=[
                pltpu.VMEM((MOE_BLOCK, D_MODEL), F32),
                pltpu.SemaphoreType.DMA(()),
                pltpu.SemaphoreType.DMA(()),
            ],
        ),
        out_shape=jax.ShapeDtypeStruct((n_rows, D_MODEL), F32),
        compiler_params=pltpu.CompilerParams(dimension_semantics=("arbitrary",)),
        name="dispatch",
    )(zero_list, n_zero, dest, hn)


def _expert_kernel(be_ref, nu_ref, x_ref, w1_ref, w3_ref, w2_ref, o_ref):
    b = pl.program_id(0)

    @pl.when(b < nu_ref[0])
    def _():
        x = x_ref[...].astype(BF16)
        h1 = jnp.dot(x, w1_ref[0], preferred_element_type=F32)
        h3 = jnp.dot(x, w3_ref[0], preferred_element_type=F32)
        hid = (h1 * jax.nn.sigmoid(h1) * h3).astype(BF16)
        o_ref[...] = jnp.dot(hid, w2_ref[0], preferred_element_type=F32)

    @pl.when(b >= nu_ref[0])
    def _():
        o_ref[...] = jnp.zeros_like(o_ref)


def _experts(xs, block_expert, n_used, w1, w3, w2):
    n_rows = xs.shape[0]
    wsel = lambda b, be, nu: (be[b], 0, 0)
    return pl.pallas_call(
        _expert_kernel,
        grid_spec=pltpu.PrefetchScalarGridSpec(
            num_scalar_prefetch=2,
            grid=(n_rows // MOE_BLOCK,),
            in_specs=[
                pl.BlockSpec((MOE_BLOCK, D_MODEL), lambda b, be, nu: (jnp.minimum(b, nu[0] - 1), 0)),
                pl.BlockSpec((1, D_MODEL, EXPERT_FF), wsel),
                pl.BlockSpec((1, D_MODEL, EXPERT_FF), wsel),
                pl.BlockSpec((1, EXPERT_FF, D_MODEL), wsel),
            ],
            out_specs=pl.BlockSpec((MOE_BLOCK, D_MODEL), lambda b, be, nu: (b, 0)),
        ),
        out_shape=jax.ShapeDtypeStruct((n_rows, D_MODEL), F32),
        compiler_params=pltpu.CompilerParams(dimension_semantics=("arbitrary",)),
        name="experts",
    )(block_expert, n_used, xs, w1, w3, w2)


def _combine_kernel(tm, n_tiles, final, dcur_ref, dnext_ref, r_ref, h_ref, fg_ref, ys_ref, o_ref, ybuf, sem):
    i = pl.program_id(0)

    def issue(d_ref, slot):
        def body(t, c):
            for k in range(2):
                d = d_ref[2 * t + k]
                pltpu.make_async_copy(ys_ref.at[pl.ds(d, 1)], ybuf.at[slot, pl.ds(k * tm + t, 1)], sem.at[slot]).start()
            return c

        lax.fori_loop(0, tm, body, 0, unroll=8)

    @pl.when(i == 0)
    def _():
        issue(dcur_ref, 0)

    @pl.when(i + 1 < n_tiles)
    def _():
        issue(dnext_ref, (i + 1) % 2)

    slot = i % 2
    _rows_copy(ys_ref, ybuf.at[slot], sem.at[slot], 2 * tm).wait()
    r = r_ref[...]
    w0 = r[:, R_W0:R_W0 + 1]
    w1 = r[:, R_W1:R_W1 + 1]
    out = h_ref[...] + (w0 * ybuf[slot, 0:tm, :] + w1 * ybuf[slot, tm:2 * tm, :])
    if final:
        out = _rms(out, fg_ref[...])
    o_ref[...] = out


def _combine(ys, dest, route, h2, final_g, final, tm):
    t = h2.shape[0]
    n_tiles = t // tm
    row = lambda i: (i, 0)
    return pl.pallas_call(
        functools.partial(_combine_kernel, tm, n_tiles, final),
        grid=(n_tiles,),
        in_specs=[
            pl.BlockSpec((2 * tm,), lambda i: (i,), memory_space=pltpu.SMEM),
            pl.BlockSpec((2 * tm,), lambda i: (jnp.minimum(i + 1, n_tiles - 1),), memory_space=pltpu.SMEM),
            pl.BlockSpec((tm, LANES), row),
            pl.BlockSpec((tm, D_MODEL), row),
            pl.BlockSpec((1, D_MODEL), lambda i: (0, 0)),
            pl.BlockSpec(memory_space=pl.ANY),
        ],
        out_specs=pl.BlockSpec((tm, D_MODEL), row),
        out_shape=jax.ShapeDtypeStruct((t, D_MODEL), F32),
        scratch_shapes=[pltpu.VMEM((2, 2 * tm, D_MODEL), F32), pltpu.SemaphoreType.DMA((2,))],
        compiler_params=pltpu.CompilerParams(dimension_semantics=("arbitrary",), vmem_limit_bytes=VMEM_BIG),
        name="combine",
    )(dest, dest, route, h2, final_g, ys)


def _rotary_tables(positions):
    half = ROT_DIM // 2
    inv = ROPE_THETA ** (-jnp.arange(0, ROT_DIM, 2, dtype=F32) / ROT_DIM)
    ang = positions.reshape(-1).astype(F32)[:, None] * inv
    cos, sin = jnp.cos(ang), jnp.sin(ang)
    n = ang.shape[0]
    one = jnp.ones((n, HEAD_DIM - ROT_DIM), F32)
    zero = jnp.zeros((n, HEAD_DIM - ROT_DIM), F32)
    zh = jnp.zeros((n, half), F32)
    a = jnp.concatenate([cos, cos, one], axis=1)
    b = jnp.concatenate([-sin, zh, zero], axis=1)
    c = jnp.concatenate([zh, sin, zero], axis=1)
    return tuple(jnp.concatenate([m, m], axis=1) for m in (a, b, c))


def _sink_table(sinks):
    s = sinks.astype(F32).reshape(N_KV_HEADS, 4, 1, 2, 1)
    s = jnp.broadcast_to(s, (N_KV_HEADS, 4, CHUNK, 2, HEAD_DIM))
    return s.reshape(N_KV_HEADS, 4 * CHUNK, LANES)


def _moe_plan(route, counts, n_tok):
    n_rows = -(-(2 * n_tok) // MOE_BLOCK) * MOE_BLOCK + N_EXPERTS * MOE_BLOCK
    n_blocks = n_rows // MOE_BLOCK
    cnt = counts[0, R_LANE0:R_LANE0 + N_EXPERTS].astype(jnp.int32)
    padded = (cnt + MOE_BLOCK - 1) // MOE_BLOCK * MOE_BLOCK
    pad_end = jnp.cumsum(padded)
    pad_start = pad_end - padded
    e = route[:, R_E0:R_E1 + 1].astype(jnp.int32)
    rank = route[:, R_RANK0:R_RANK1 + 1].astype(jnp.int32)
    dest = (pad_start[e] + rank).reshape(-1)
    n_used = pad_end[-1] // MOE_BLOCK
    blk = jnp.arange(n_blocks, dtype=jnp.int32)
    be = jnp.minimum(jnp.searchsorted(pad_end, blk * MOE_BLOCK, side="right"), N_EXPERTS - 1).astype(jnp.int32)
    be = jnp.where(blk < n_used, be, be[n_used - 1])
    last = pad_end // MOE_BLOCK - 1
    is_last = jnp.any((blk[:, None] == last[None, :]) & (padded[None, :] > 0), axis=1)
    zero_mask = is_last | (blk >= n_used)
    zero_list = jnp.nonzero(zero_mask, size=n_blocks, fill_value=0)[0].astype(jnp.int32)
    n_zero = jnp.sum(zero_mask).astype(jnp.int32).reshape(1)
    return n_rows, dest, be, n_used.astype(jnp.int32).reshape(1), zero_list, n_zero


def kernel(x, positions, norm1_g, w_in, b_gate, conv_w, sinks, w_conv_out, w_attn_out, w_o, norm2_g, w_group, b_group, w_route, b_route, w1, w3, w2, final_g):
    b, s, d = x.shape
    t = b * s
    depth = w_in.shape[0]
    tm = 512
    ra, rb, rc = _rotary_tables(positions)
    h = x.reshape(t, d)
    for l in range(depth):
        p = _proj(h, norm1_g[l][None], w_in[l].astype(BF16), b_gate[l][None], ra, rb, rc, tm)
        h = _mix(p, h, conv_w[l], _sink_table(sinks[l]), w_conv_out[l].astype(BF16), w_attn_out[l].astype(BF16),
                 w_o[l].astype(BF16), s, tm)
        w_router = jnp.concatenate(
            [w_group[l], w_route[l].transpose(1, 0, 2).reshape(d, N_EXPERTS),
             jnp.zeros((d, LANES - N_GROUPS - N_EXPERTS), F32)], axis=1)
        b_router = jnp.concatenate(
            [b_group[l], b_route[l].reshape(-1), jnp.zeros((LANES - N_GROUPS - N_EXPERTS,), F32)])[None]
        hn, route, counts = _route(h, norm2_g[l][None], w_router, b_router, tm)
        n_rows, dest, be, n_used, zero_list, n_zero = _moe_plan(route, counts, t)
        xs = _dispatch(hn, dest, zero_list, n_zero, n_rows, tm)
        ys = _experts(xs, be, n_used, w1[l].astype(BF16), w3[l].astype(BF16), w2[l].astype(BF16))
        h = _combine(ys, dest, route, h, final_g[None], l == depth - 1, tm)
    return h.reshape(b, s, d)
```

```python
import functools

import jax
import jax.numpy as jnp
from jax import lax
from jax.experimental import pallas as pl
from jax.experimental.pallas import tpu as pltpu

F32 = jnp.float32
BF16 = jnp.bfloat16

D_MODEL = 1024
RMS_EPS = 1e-5
CHUNK = 64
WINDOW_CHUNKS = 2
HEAD_DIM = 64
N_Q_HEADS = 16
N_KV_HEADS = 2
ROT_DIM = 16
ROPE_THETA = 500000.0
N_GROUPS = 8
EXPERTS_PER_GROUP = 8
N_EXPERTS = 64
EXPERT_FF = 512
MOE_BLOCK = 256
LANES = 128

W_CB, W_CC, W_CX, W_Q, W_K, W_G = 0, 1024, 2048, 3072, 4096, 4352
P_U, P_CC, P_Q, P_KV, P_G = 0, 1024, 2048, 3072, 3328
P_COLS = 5376
HALO = WINDOW_CHUNKS * CHUNK
KEYS = HALO + CHUNK

VMEM_BIG = 56 * 1024 * 1024


def _rms(x, g):
    ms = jnp.mean(x * x, axis=-1, keepdims=True)
    return x * lax.rsqrt(ms + RMS_EPS) * g


def _proj_kernel(x_ref, g_ref, w_ref, bg_ref, ra_ref, rb_ref, rc_ref, o_ref):
    xn = _rms(x_ref[...], g_ref[...]).astype(BF16)

    def mm(c0, n):
        return jnp.dot(xn, w_ref[:, c0:c0 + n], preferred_element_type=F32)

    ra, rb, rc = ra_ref[...], rb_ref[...], rc_ref[...]

    def rot(t):
        return t * ra + pltpu.roll(t, LANES - ROT_DIM // 2, 1) * rb + pltpu.roll(t, ROT_DIM // 2, 1) * rc

    nc = 512
    for j in range(2):
        cb = mm(W_CB + j * nc, nc)
        cx = mm(W_CX + j * nc, nc)
        o_ref[:, P_U + j * nc:P_U + (j + 1) * nc] = (cb * cx).astype(BF16)
    for j in range(2):
        o_ref[:, P_CC + j * nc:P_CC + (j + 1) * nc] = mm(W_CC + j * nc, nc).astype(BF16)
    for j in range(2):
        q = mm(W_Q + j * nc, nc)
        for p in range(nc // LANES):
            t = rot(q[:, p * LANES:(p + 1) * LANES]) * (HEAD_DIM ** -0.5)
            o_ref[:, P_Q + j * nc + p * LANES:P_Q + j * nc + (p + 1) * LANES] = t.astype(BF16)
    kv = mm(W_K, 2 * LANES)
    o_ref[:, P_KV:P_KV + LANES] = rot(kv[:, :LANES]).astype(BF16)
    o_ref[:, P_KV + LANES:P_KV + 2 * LANES] = kv[:, LANES:].astype(BF16)
    for j in range(4):
        gt = mm(W_G + j * nc, nc) + bg_ref[:, j * nc:(j + 1) * nc]
        o_ref[:, P_G + j * nc:P_G + (j + 1) * nc] = jax.nn.sigmoid(gt).astype(BF16)


def _proj(x2, g1, w_in, b_gate, ra, rb, rc, tm):
    t = x2.shape[0]
    row = lambda i: (i, 0)
    fixed = lambda i: (0, 0)
    return pl.pallas_call(
        _proj_kernel,
        grid=(t // tm,),
        in_specs=[
            pl.BlockSpec((tm, D_MODEL), row),
            pl.BlockSpec((1, D_MODEL), fixed),
            pl.BlockSpec(w_in.shape, fixed, pipeline_mode=pl.Buffered(1)),
            pl.BlockSpec((1, 2 * D_MODEL), fixed),
            pl.BlockSpec((tm, LANES), row),
            pl.BlockSpec((tm, LANES), row),
            pl.BlockSpec((tm, LANES), row),
        ],
        out_specs=pl.BlockSpec((tm, P_COLS), row),
        out_shape=jax.ShapeDtypeStruct((t, P_COLS), BF16),
        compiler_params=pltpu.CompilerParams(dimension_semantics=("arbitrary",), vmem_limit_bytes=VMEM_BIG),
        name="proj",
    )(x2, g1, w_in, b_gate, ra, rb, rc)


def _mix_kernel(tm, tiles_per_seq, p_ref, hkv_ref, hu_ref, x_ref, cw_ref, sink_ref, wco_ref, wao_ref, wo_ref,
                o_ref, ke_s, ko_s, vt_s, attn_s, ci_s):
    first = (pl.program_id(0) % tiles_per_seq) == 0
    lo = lax.broadcasted_iota(jnp.int32, (1, LANES), 1) < HEAD_DIM

    def stage(row0, kv):
        n = kv.shape[0]
        k = kv[:, :LANES].astype(F32)
        kr = pltpu.roll(k, HEAD_DIM, 1)
        ke_s[0, row0:row0 + n, :] = jnp.where(lo, k, 0.0).astype(BF16)
        ko_s[0, row0:row0 + n, :] = jnp.where(lo, 0.0, kr).astype(BF16)
        ke_s[1, row0:row0 + n, :] = jnp.where(lo, kr, 0.0).astype(BF16)
        ko_s[1, row0:row0 + n, :] = jnp.where(lo, 0.0, k).astype(BF16)
        v = kv[:, LANES:].astype(F32)
        for j in range(n // CHUNK):
            vt_s[row0 // CHUNK + j] = v[j * CHUNK:(j + 1) * CHUNK].T.astype(BF16)

    stage(0, hkv_ref[...])
    stage(HALO, p_ref[:, P_KV:P_KV + 2 * LANES])

    krow = lax.broadcasted_iota(jnp.int32, (2 * KEYS, 1), 0)
    krow = jnp.where(krow >= KEYS, krow - KEYS, krow)

    def chunk_body(c, carry):
        r0 = pl.multiple_of(c * CHUNK, CHUNK)
        nbad = jnp.where(first, jnp.maximum(HALO - c * CHUNK, 0), 0)
        bad = krow < nbad
        for g in range(N_KV_HEADS):
            qbase = P_Q + g * 4 * LANES
            q = jnp.concatenate(
                [p_ref[pl.ds(r0, CHUNK), qbase + p * LANES:qbase + (p + 1) * LANES] for p in range(4)], axis=0)
            kk = jnp.concatenate([ke_s[g, pl.ds(r0, KEYS), :], ko_s[g, pl.ds(r0, KEYS), :]], axis=0)
            st = lax.dot_general(kk, q, (((1,), (1,)), ((), ())), preferred_element_type=F32)
            st = jnp.where(bad, -jnp.inf, st)
            outs = []
            for par in range(2):
                s = st[par * KEYS:(par + 1) * KEYS]
                sc = sink_ref[2 * g + par:2 * g + par + 1, :]
                m = jnp.maximum(jnp.max(s, axis=0, keepdims=True), sc)
                p = jnp.exp(s - m)
                l = jnp.sum(p, axis=0, keepdims=True) + jnp.exp(sc - m)
                pb = p.astype(BF16)
                o = None
                for j in range(KEYS // CHUNK):
                    vt = vt_s[c + j, g * HEAD_DIM:(g + 1) * HEAD_DIM, :]
                    part = jnp.dot(vt, pb[j * CHUNK:(j + 1) * CHUNK], preferred_element_type=F32)
                    o = part if o is None else o + part
                outs.append(o / l)
            ot = jnp.concatenate(outs, axis=0)
            for p2 in range(2):
                blk = ot[:, p2 * LANES:(p2 + 1) * LANES].T
                for h in range(2):
                    c0 = (g * 4 + 2 * p2 + h) * LANES
                    attn_s[pl.ds(r0, CHUNK), c0:c0 + LANES] = blk[h * CHUNK:(h + 1) * CHUNK].astype(BF16)
        return carry

    lax.fori_loop(0, tm // CHUNK, chunk_body, 0)

    hu = hu_ref[...].astype(F32)
    row = lax.broadcasted_iota(jnp.int32, (tm, 1), 0)
    cw = cw_ref[...]
    nc = 256
    for j in range(D_MODEL // nc):
        cs = slice(j * nc, (j + 1) * nc)
        u = p_ref[:, P_U + j * nc:P_U + (j + 1) * nc].astype(F32)
        h1 = jnp.where(first, 0.0, hu[15:16, cs])
        h2 = jnp.where(first, 0.0, hu[14:15, cs])
        s1 = jnp.where(row == 0, h1, pltpu.roll(u, 1, 0))
        s2 = jnp.where(row == 0, h2, jnp.where(row == 1, h1, pltpu.roll(u, 2, 0)))
        y = cw[2:3, cs] * u + cw[1:2, cs] * s1 + cw[0:1, cs] * s2
        cc = p_ref[:, P_CC + j * nc:P_CC + (j + 1) * nc].astype(F32)
        ci_s[:, cs] = (cc * y).astype(BF16)

    conv_out = jnp.dot(ci_s[...], wco_ref[...], preferred_element_type=F32)
    attn_out = jnp.dot(attn_s[...], wao_ref[...], preferred_element_type=F32)
    gc = p_ref[:, P_G:P_G + D_MODEL].astype(F32)
    ga = p_ref[:, P_G + D_MODEL:P_G + 2 * D_MODEL].astype(F32)
    merged = (gc * conv_out + ga * attn_out).astype(BF16)
    o_ref[...] = x_ref[...] + jnp.dot(merged, wo_ref[...], preferred_element_type=F32)


def _mix(p, x2, conv_w, sink_tab, w_co, w_ao, w_o, seq, tm):
    t = x2.shape[0]
    row = lambda i: (i, 0)
    fixed = lambda i: (0, 0)
    sq = (D_MODEL, D_MODEL)
    return pl.pallas_call(
        functools.partial(_mix_kernel, tm, seq // tm),
        grid=(t // tm,),
        in_specs=[
            pl.BlockSpec((tm, P_COLS), row),
            pl.BlockSpec((HALO, 2 * LANES), lambda i: (jnp.maximum(i * (tm // HALO) - 1, 0), P_KV // (2 * LANES))),
            pl.BlockSpec((16, D_MODEL), lambda i: (jnp.maximum(i * (tm // 16) - 1, 0), P_U // D_MODEL)),
            pl.BlockSpec((tm, D_MODEL), row),
            pl.BlockSpec(conv_w.shape, fixed),
            pl.BlockSpec(sink_tab.shape, fixed),
            pl.BlockSpec(sq, fixed),
            pl.BlockSpec(sq, fixed),
            pl.BlockSpec(sq, fixed),
        ],
        out_specs=pl.BlockSpec((tm, D_MODEL), row),
        out_shape=jax.ShapeDtypeStruct((t, D_MODEL), F32),
        scratch_shapes=[
            pltpu.VMEM((N_KV_HEADS, HALO + tm, LANES), BF16),
            pltpu.VMEM((N_KV_HEADS, HALO + tm, LANES), BF16),
            pltpu.VMEM(((HALO + tm) // CHUNK, 2 * HEAD_DIM, CHUNK), BF16),
            pltpu.VMEM((tm, D_MODEL), BF16),
            pltpu.VMEM((tm, D_MODEL), BF16),
        ],
        compiler_params=pltpu.CompilerParams(dimension_semantics=("arbitrary",), vmem_limit_bytes=VMEM_BIG),
        name="mix",
    )(p, p, p, x2, conv_w, sink_tab, w_co, w_ao, w_o)


R_E0, R_E1, R_RANK0, R_RANK1, R_W0, R_W1 = range(6)
R_LANE0 = N_GROUPS


def _route_kernel(tm, h_ref, g_ref, wr_ref, br_ref, r_ref, cnt_ref, base_s):
    @pl.when(pl.program_id(0) == 0)
    def _():
        base_s[...] = jnp.zeros_like(base_s)

    hn = _rms(h_ref[...], g_ref[...])
    lg = jnp.dot(hn, wr_ref[...], precision=lax.Precision.HIGHEST, preferred_element_type=F32) + br_ref[...]
    lane = lax.broadcasted_iota(jnp.int32, (tm, LANES), 1)
    neg = -jnp.inf

    def first_max(v):
        m = jnp.max(v, axis=1, keepdims=True)
        return m, jnp.min(jnp.where(v == m, lane, LANES), axis=1, keepdims=True)

    gl = jnp.where(lane < N_GROUPS, lg, neg)
    gmax, gtop = first_max(gl)
    pg = 1.0 / jnp.sum(jnp.exp(gl - gmax), axis=1, keepdims=True)
    in_group = (lane >= R_LANE0) & (lane < R_LANE0 + N_EXPERTS) & (((lane - R_LANE0) >> 3) == gtop)
    el = jnp.where(in_group, lg, neg)
    m1, i1 = first_max(el)
    m2, i2 = first_max(jnp.where(lane == i1, neg, el))
    e2 = jnp.exp(m2 - m1)
    den = 1.0 + e2
    w0 = pg * (1.0 / den)
    w1 = pg * (e2 / den)

    hit0 = lane == i1
    hit1 = lane == i2
    a = jnp.where(hit0 | hit1, 1.0, 0.0)
    rr = lax.broadcasted_iota(jnp.int32, (tm, tm), 0)
    cc = lax.broadcasted_iota(jnp.int32, (tm, tm), 1)
    tri = jnp.where(cc < rr, 1.0, 0.0).astype(BF16)
    before = jnp.dot(tri, a.astype(BF16), preferred_element_type=F32) + base_s[...]
    rank0 = jnp.sum(jnp.where(hit0, before, 0.0), axis=1, keepdims=True)
    rank1 = jnp.sum(jnp.where(hit1, before, 0.0), axis=1, keepdims=True)
    base_s[...] = base_s[...] + jnp.sum(a, axis=0, keepdims=True)
    cnt_ref[...] = base_s[...]

    vals = ((i1 - R_LANE0).astype(F32), (i2 - R_LANE0).astype(F32), rank0, rank1, w0, w1)
    slab = jnp.zeros((tm, LANES), F32)
    for k, v in enumerate(vals):
        slab = jnp.where(lane == k, v, slab)
    r_ref[...] = slab


def _route(h2, g2, w_router, b_router, tm):
    t = h2.shape[0]
    row = lambda i: (i, 0)
    fixed = lambda i: (0, 0)
    return pl.pallas_call(
        functools.partial(_route_kernel, tm),
        grid=(t // tm,),
        in_specs=[
            pl.BlockSpec((tm, D_MODEL), row),
            pl.BlockSpec((1, D_MODEL), fixed),
            pl.BlockSpec((D_MODEL, LANES), fixed),
            pl.BlockSpec((1, LANES), fixed),
        ],
        out_specs=[pl.BlockSpec((tm, LANES), row), pl.BlockSpec((1, LANES), fixed)],
        out_shape=[jax.ShapeDtypeStruct((t, LANES), F32), jax.ShapeDtypeStruct((1, LANES), F32)],
        scratch_shapes=[pltpu.VMEM((1, LANES), F32)],
        compiler_params=pltpu.CompilerParams(dimension_semantics=("arbitrary",)),
        name="route",
    )(h2, g2, w_router, b_router)


def _rows_copy(src_ref, dst_ref, sem, n):
    return pltpu.make_async_copy(src_ref.at[pl.ds(0, n)], dst_ref.at[pl.ds(0, n)], sem)


def _dispatch_kernel(tm, n_tiles, n_blocks, zm_ref, d_ref, h_ref, g_ref, xs_ref, hn_s, zbuf, zsem, sem):
    i = pl.program_id(0)

    @pl.when(i == 0)
    def _():
        zbuf[...] = jnp.zeros_like(zbuf)

        def zero_copy(b):
            r = pl.multiple_of(b * MOE_BLOCK, MOE_BLOCK)
            return pltpu.make_async_copy(zbuf, xs_ref.at[pl.ds(r, MOE_BLOCK)], zsem)

        def start(b, c):
            @pl.when(zm_ref[b] == 1)
            def _():
                zero_copy(b).start()
            return c

        def wait(b, c):
            @pl.when(zm_ref[b] == 1)
            def _():
                zero_copy(b).wait()
            return c

        lax.fori_loop(0, n_blocks, start, 0)
        lax.fori_loop(0, n_blocks, wait, 0)

    slot = i % 2
    hn_s[slot] = _rms(h_ref[...], g_ref[...])

    def body(t, c):
        for k in range(2):
            d = d_ref[2 * t + k]
            pltpu.make_async_copy(hn_s.at[slot, pl.ds(t, 1)], xs_ref.at[pl.ds(d, 1)], sem.at[slot]).start()
        return c

    lax.fori_loop(0, tm, body, 0, unroll=8)

    def drain(s):
        for _ in range(2):
            _rows_copy(hn_s.at[s], xs_ref, sem.at[s], tm).wait()

    @pl.when(i > 0)
    def _():
        drain(1 - slot)

    @pl.when(i == n_tiles - 1)
    def _():
        drain(slot)


def _dispatch(h2, g2, dest, zero_mask, n_rows, tm):
    t = h2.shape[0]
    n_tiles = t // tm
    n_blocks = n_rows // MOE_BLOCK
    return pl.pallas_call(
        functools.partial(_dispatch_kernel, tm, n_tiles, n_blocks),
        grid_spec=pltpu.PrefetchScalarGridSpec(
            num_scalar_prefetch=1,
            grid=(n_tiles,),
            in_specs=[
                pl.BlockSpec((2 * tm,), lambda i, zm: (i,), memory_space=pltpu.SMEM),
                pl.BlockSpec((tm, D_MODEL), lambda i, zm: (i, 0)),
                pl.BlockSpec((1, D_MODEL), lambda i, zm: (0, 0)),
            ],
            out_specs=pl.BlockSpec(memory_space=pl.ANY),
            scratch_shapes=[
                pltpu.VMEM((2, tm, D_MODEL), F32),
                pltpu.VMEM((MOE_BLOCK, D_MODEL), F32),
                pltpu.SemaphoreType.DMA(()),
                pltpu.SemaphoreType.DMA((2,)),
            ],
        ),
        out_shape=jax.ShapeDtypeStruct((n_rows, D_MODEL), F32),
        compiler_params=pltpu.CompilerParams(dimension_semantics=("arbitrary",)),
        name="dispatch",
    )(zero_mask, dest, h2, g2)


def _expert_kernel(be_ref, nw_ref, nu_ref, x_ref, w1_ref, w3_ref, w2_ref, o_ref, w1_s, w3_s, w2_s):
    b = pl.program_id(0)

    @pl.when(b < nu_ref[0])
    def _():
        @pl.when(nw_ref[b] == 1)
        def _():
            w1_s[...] = w1_ref[0].astype(BF16)
            w3_s[...] = w3_ref[0].astype(BF16)
            w2_s[...] = w2_ref[0].astype(BF16)

        x = x_ref[...].astype(BF16)
        h1 = jnp.dot(x, w1_s[...], preferred_element_type=F32)
        h3 = jnp.dot(x, w3_s[...], preferred_element_type=F32)
        hid = (h1 * jax.nn.sigmoid(h1) * h3).astype(BF16)
        o_ref[...] = jnp.dot(hid, w2_s[...], preferred_element_type=F32)

    @pl.when(b >= nu_ref[0])
    def _():
        o_ref[...] = jnp.zeros_like(o_ref)


def _experts(xs, block_expert, new_expert, n_used, w1, w3, w2):
    n_rows = xs.shape[0]
    wsel = lambda b, be, nw, nu: (be[b], 0, 0)
    return pl.pallas_call(
        _expert_kernel,
        grid_spec=pltpu.PrefetchScalarGridSpec(
            num_scalar_prefetch=3,
            grid=(n_rows // MOE_BLOCK,),
            in_specs=[
                pl.BlockSpec((MOE_BLOCK, D_MODEL), lambda b, be, nw, nu: (jnp.minimum(b, nu[0] - 1), 0)),
                pl.BlockSpec((1, D_MODEL, EXPERT_FF), wsel),
                pl.BlockSpec((1, D_MODEL, EXPERT_FF), wsel),
                pl.BlockSpec((1, EXPERT_FF, D_MODEL), wsel),
            ],
            out_specs=pl.BlockSpec((MOE_BLOCK, D_MODEL), lambda b, be, nw, nu: (b, 0)),
            scratch_shapes=[
                pltpu.VMEM((D_MODEL, EXPERT_FF), BF16),
                pltpu.VMEM((D_MODEL, EXPERT_FF), BF16),
                pltpu.VMEM((EXPERT_FF, D_MODEL), BF16),
            ],
        ),
        out_shape=jax.ShapeDtypeStruct((n_rows, D_MODEL), F32),
        compiler_params=pltpu.CompilerParams(dimension_semantics=("arbitrary",), vmem_limit_bytes=VMEM_BIG),
        name="experts",
    )(block_expert, new_expert, n_used, xs, w1, w3, w2)


def _combine_kernel(tm, n_tiles, final, dcur_ref, dnext_ref, r_ref, h_ref, fg_ref, ys_ref, o_ref, ybuf, sem):
    i = pl.program_id(0)

    def issue(d_ref, slot):
        def body(t, c):
            for k in range(2):
                d = d_ref[2 * t + k]
                pltpu.make_async_copy(ys_ref.at[pl.ds(d, 1)], ybuf.at[slot, pl.ds(k * tm + t, 1)], sem.at[slot]).start()
            return c

        lax.fori_loop(0, tm, body, 0, unroll=8)

    @pl.when(i == 0)
    def _():
        issue(dcur_ref, 0)

    @pl.when(i + 1 < n_tiles)
    def _():
        issue(dnext_ref, (i + 1) % 2)

    slot = i % 2
    _rows_copy(ys_ref, ybuf.at[slot], sem.at[slot], 2 * tm).wait()
    r = r_ref[...]
    w0 = r[:, R_W0:R_W0 + 1]
    w1 = r[:, R_W1:R_W1 + 1]
    out = h_ref[...] + (w0 * ybuf[slot, 0:tm, :] + w1 * ybuf[slot, tm:2 * tm, :])
    if final:
        out = _rms(out, fg_ref[...])
    o_ref[...] = out


def _combine(ys, dest, route, h2, final_g, final, tm):
    t = h2.shape[0]
    n_tiles = t // tm
    row = lambda i: (i, 0)
    return pl.pallas_call(
        functools.partial(_combine_kernel, tm, n_tiles, final),
        grid=(n_tiles,),
        in_specs=[
            pl.BlockSpec((2 * tm,), lambda i: (i,), memory_space=pltpu.SMEM),
            pl.BlockSpec((2 * tm,), lambda i: (jnp.minimum(i + 1, n_tiles - 1),), memory_space=pltpu.SMEM),
            pl.BlockSpec((tm, LANES), row),
            pl.BlockSpec((tm, D_MODEL), row),
            pl.BlockSpec((1, D_MODEL), lambda i: (0, 0)),
            pl.BlockSpec(memory_space=pl.ANY),
        ],
        out_specs=pl.BlockSpec((tm, D_MODEL), row),
        out_shape=jax.ShapeDtypeStruct((t, D_MODEL), F32),
        scratch_shapes=[pltpu.VMEM((2, 2 * tm, D_MODEL), F32), pltpu.SemaphoreType.DMA((2,))],
        compiler_params=pltpu.CompilerParams(dimension_semantics=("arbitrary",), vmem_limit_bytes=VMEM_BIG),
        name="combine",
    )(dest, dest, route, h2, final_g, ys)


def _rotary_tables(positions):
    half = ROT_DIM // 2
    inv = ROPE_THETA ** (-jnp.arange(0, ROT_DIM, 2, dtype=F32) / ROT_DIM)
    ang = positions.reshape(-1).astype(F32)[:, None] * inv
    cos, sin = jnp.cos(ang), jnp.sin(ang)
    n = ang.shape[0]
    one = jnp.ones((n, HEAD_DIM - ROT_DIM), F32)
    zero = jnp.zeros((n, HEAD_DIM - ROT_DIM), F32)
    zh = jnp.zeros((n, half), F32)
    a = jnp.concatenate([cos, cos, one], axis=1)
    b = jnp.concatenate([-sin, zh, zero], axis=1)
    c = jnp.concatenate([zh, sin, zero], axis=1)
    return tuple(jnp.concatenate([m, m], axis=1) for m in (a, b, c))


def _sink_table(sinks):
    s = sinks.astype(F32).reshape(N_KV_HEADS, 4, 2).transpose(0, 2, 1)
    s = jnp.broadcast_to(s[..., None], (N_KV_HEADS, 2, 4, CHUNK)).reshape(2 * N_KV_HEADS, 4 * CHUNK)
    return jnp.concatenate([s, jnp.zeros((8 - 2 * N_KV_HEADS, 4 * CHUNK), F32)], axis=0)


def _moe_plan(route, counts, n_tok):
    n_rows = -(-(2 * n_tok) // MOE_BLOCK) * MOE_BLOCK + N_EXPERTS * MOE_BLOCK
    n_blocks = n_rows // MOE_BLOCK
    ids = jnp.arange(N_EXPERTS, dtype=jnp.int32)
    cnt = counts[0, R_LANE0:R_LANE0 + N_EXPERTS].astype(jnp.int32)
    padded = (cnt + MOE_BLOCK - 1) // MOE_BLOCK * MOE_BLOCK
    pad_end = jnp.sum(jnp.where(ids[None, :] <= ids[:, None], padded[None, :], 0), axis=1)
    pad_start = pad_end - padded
    e = route[:, R_E0:R_E1 + 1].astype(jnp.int32)
    rank = route[:, R_RANK0:R_RANK1 + 1].astype(jnp.int32)
    dest = (jnp.sum(jnp.where(e[..., None] == ids, pad_start, 0), axis=-1) + rank).reshape(-1)
    n_used = pad_end[-1] // MOE_BLOCK
    blk = jnp.arange(n_blocks, dtype=jnp.int32)
    be = jnp.minimum(jnp.sum((pad_end[None, :] <= blk[:, None] * MOE_BLOCK).astype(jnp.int32), axis=1), N_EXPERTS - 1)
    be_last = jnp.sum(jnp.where(blk == n_used - 1, be, 0))
    be = jnp.where(blk < n_used, be, be_last)
    new_expert = jnp.concatenate([jnp.ones((1,), jnp.int32), (be[1:] != be[:-1]).astype(jnp.int32)])
    last = pad_end // MOE_BLOCK - 1
    is_last = jnp.any((blk[:, None] == last[None, :]) & (padded[None, :] > 0), axis=1)
    zero_mask = (is_last | (blk >= n_used)).astype(jnp.int32)
    return n_rows, dest, be, new_expert, n_used.astype(jnp.int32).reshape(1), zero_mask


def kernel(x, positions, norm1_g, w_in, b_gate, conv_w, sinks, w_conv_out, w_attn_out, w_o, norm2_g, w_group, b_group, w_route, b_route, w1, w3, w2, final_g):
    b, s, d = x.shape
    t = b * s
    depth = w_in.shape[0]
    tm = 512
    ra, rb, rc = _rotary_tables(positions)
    h = x.reshape(t, d)
    for l in range(depth):
        p = _proj(h, norm1_g[l][None], w_in[l].astype(BF16), b_gate[l][None], ra, rb, rc, tm)
        h = _mix(p, h, conv_w[l], _sink_table(sinks[l]), w_conv_out[l].astype(BF16), w_attn_out[l].astype(BF16),
                 w_o[l].astype(BF16), s, tm)
        w_router = jnp.concatenate(
            [w_group[l], w_route[l].transpose(1, 0, 2).reshape(d, N_EXPERTS),
             jnp.zeros((d, LANES - N_GROUPS - N_EXPERTS), F32)], axis=1)
        b_router = jnp.concatenate(
            [b_group[l], b_route[l].reshape(-1), jnp.zeros((LANES - N_GROUPS - N_EXPERTS,), F32)])[None]
        g2 = norm2_g[l][None]
        route, counts = _route(h, g2, w_router, b_router, tm)
        n_rows, dest, be, new_expert, n_used, zero_mask = _moe_plan(route, counts, t)
        xs = _dispatch(h, g2, dest, zero_mask, n_rows, tm)
        ys = _experts(xs, be, new_expert, n_used, w1[l], w3[l], w2[l])
        h = _combine(ys, dest, route, h, final_g[None], l == depth - 1, tm)
    return h.reshape(b, s, d)
```

```python
import functools

import jax
import jax.numpy as jnp
from jax import lax
from jax.experimental import pallas as pl
from jax.experimental.pallas import tpu as pltpu

F32 = jnp.float32
BF16 = jnp.bfloat16
U32 = jnp.uint32

D_MODEL = 1024
HALF = D_MODEL // 2
RMS_EPS = 1e-5
CHUNK = 64
WINDOW_CHUNKS = 2
HEAD_DIM = 64
N_Q_HEADS = 16
N_KV_HEADS = 2
ROT_DIM = 16
ROPE_THETA = 500000.0
N_GROUPS = 8
EXPERTS_PER_GROUP = 8
N_EXPERTS = 64
EXPERT_FF = 512
MOE_BLOCK = 256
LANES = 128

W_CB, W_CC, W_CX, W_Q, W_K, W_G = 0, 1024, 2048, 3072, 4096, 4352
P_U, P_CC, P_Q, P_KV, P_G = 0, 1024, 2048, 3072, 3328
P_COLS = 5376
HALO = WINDOW_CHUNKS * CHUNK
KEYS = HALO + CHUNK

VMEM_BIG = 56 * 1024 * 1024
NT = (((1,), (1,)), ((), ()))


def _rms(x, g):
    ms = jnp.mean(x * x, axis=-1, keepdims=True)
    return x * lax.rsqrt(ms + RMS_EPS) * g


def _pack_pair(lo, hi):
    def rne(x):
        b = lax.bitcast_convert_type(x, U32)
        return b + U32(0x7FFF) + ((b >> 16) & U32(1))

    return (rne(hi) & U32(0xFFFF0000)) | (rne(lo) >> 16)


def _unpack_pair(w):
    lo = lax.bitcast_convert_type(w << 16, F32)
    hi = lax.bitcast_convert_type(w & U32(0xFFFF0000), F32)
    return lo, hi


def _proj_kernel(x_ref, g_ref, w_ref, bg_ref, rot_ref, o_ref):
    xn = _rms(x_ref[...], g_ref[...]).astype(BF16)

    def mm(c0, n):
        return jnp.dot(xn, w_ref[:, c0:c0 + n], preferred_element_type=F32)

    ra = rot_ref[:, 0:LANES]
    rb = rot_ref[:, LANES:2 * LANES]
    rc = rot_ref[:, 2 * LANES:3 * LANES]

    def rot(t):
        return t * ra + pltpu.roll(t, LANES - ROT_DIM // 2, 1) * rb + pltpu.roll(t, ROT_DIM // 2, 1) * rc

    nc = 512
    for j in range(2):
        cb = mm(W_CB + j * nc, nc)
        cx = mm(W_CX + j * nc, nc)
        o_ref[:, P_U + j * nc:P_U + (j + 1) * nc] = (cb * cx).astype(BF16)
    for j in range(2):
        o_ref[:, P_CC + j * nc:P_CC + (j + 1) * nc] = mm(W_CC + j * nc, nc).astype(BF16)
    for j in range(2):
        q = mm(W_Q + j * nc, nc)
        for p in range(nc // LANES):
            t = rot(q[:, p * LANES:(p + 1) * LANES]) * (HEAD_DIM ** -0.5)
            o_ref[:, P_Q + j * nc + p * LANES:P_Q + j * nc + (p + 1) * LANES] = t.astype(BF16)
    kv = mm(W_K, 2 * LANES)
    o_ref[:, P_KV:P_KV + LANES] = rot(kv[:, :LANES]).astype(BF16)
    o_ref[:, P_KV + LANES:P_KV + 2 * LANES] = kv[:, LANES:].astype(BF16)
    for j in range(4):
        gt = mm(W_G + j * nc, nc) + bg_ref[:, j * nc:(j + 1) * nc]
        o_ref[:, P_G + j * nc:P_G + (j + 1) * nc] = jax.nn.sigmoid(gt).astype(BF16)


def _proj(x2, g1, w_in, b_gate, rot, tm):
    t = x2.shape[0]
    row = lambda i: (i, 0)
    fixed = lambda i: (0, 0)
    return pl.pallas_call(
        _proj_kernel,
        grid=(t // tm,),
        in_specs=[
            pl.BlockSpec((tm, D_MODEL), row),
            pl.BlockSpec((1, D_MODEL), fixed),
            pl.BlockSpec(w_in.shape, fixed, pipeline_mode=pl.Buffered(1)),
            pl.BlockSpec((1, 2 * D_MODEL), fixed),
            pl.BlockSpec((tm, 3 * LANES), row),
        ],
        out_specs=pl.BlockSpec((tm, P_COLS), row),
        out_shape=jax.ShapeDtypeStruct((t, P_COLS), BF16),
        compiler_params=pltpu.CompilerParams(dimension_semantics=("arbitrary",), vmem_limit_bytes=VMEM_BIG),
        name="proj",
    )(x2, g1, w_in, b_gate, rot)


R_E0, R_E1, R_RANK0, R_RANK1, R_W0, R_W1 = range(6)
R_ROW0 = N_GROUPS


def _route_tile(tm, h, g_ref, wt_ref, bt_ref, r_ref, cnt_ref, base_s):
    hn = _rms(h, g_ref[...])
    hi = hn.astype(BF16)
    lo = (hn - hi.astype(F32)).astype(BF16)
    both = lax.dot_general(wt_ref[...], hi, NT, preferred_element_type=F32)
    lt = (both[:LANES] + both[LANES:] + lax.dot_general(wt_ref[0:LANES, :], lo, NT, preferred_element_type=F32)
          + bt_ref[:, 0:1])
    row = lax.broadcasted_iota(jnp.int32, (LANES, tm), 0)
    neg = -jnp.inf

    def first_max(v):
        m = jnp.max(v, axis=0, keepdims=True)
        return m, jnp.min(jnp.where(v == m, row, LANES), axis=0, keepdims=True)

    gl = jnp.where(row < N_GROUPS, lt, neg)
    gmax, gtop = first_max(gl)
    pg = 1.0 / jnp.sum(jnp.exp(gl - gmax), axis=0, keepdims=True)
    in_group = (row >= R_ROW0) & (row < R_ROW0 + N_EXPERTS) & (((row - R_ROW0) >> 3) == gtop)
    el = jnp.where(in_group, lt, neg)
    m1, i1 = first_max(el)
    m2, i2 = first_max(jnp.where(row == i1, neg, el))
    e2 = jnp.exp(m2 - m1)
    den = 1.0 + e2
    w0 = pg * (1.0 / den)
    w1 = pg * (e2 / den)
    hit0 = row == i1
    hit1 = row == i2
    a = jnp.where(hit0 | hit1, 1.0, 0.0)
    rr = lax.broadcasted_iota(jnp.int32, (tm, tm), 0)
    cc = lax.broadcasted_iota(jnp.int32, (tm, tm), 1)
    tri = jnp.where(rr < cc, 1.0, 0.0).astype(BF16)
    before = jnp.dot(a.astype(BF16), tri, preferred_element_type=F32) + base_s[:, 0:1]
    rank0 = jnp.sum(jnp.where(hit0, before, 0.0), axis=0, keepdims=True)
    rank1 = jnp.sum(jnp.where(hit1, before, 0.0), axis=0, keepdims=True)
    base_s[...] = base_s[...] + jnp.sum(a, axis=1, keepdims=True)
    cnt_ref[...] = base_s[...]
    vals = ((i1 - R_ROW0).astype(F32), (i2 - R_ROW0).astype(F32), rank0, rank1, w0, w1)
    slab = jnp.zeros((LANES, tm), F32)
    for k, v in enumerate(vals):
        slab = jnp.where(row == k, v, slab)
    r_ref[...] = slab.T


def _mix_kernel(tm, tiles_per_seq, p_ref, hkv_ref, hu_ref, x_ref, cw_ref, sink_ref, wco_ref, wao_ref, wo_ref,
                g2_ref, wt_ref, bt_ref, o_ref, r_ref, cnt_ref, ke_s, ko_s, vta_s, vtb_s, attn_s, ci_s, base_s):
    i = pl.program_id(0)
    first = (i % tiles_per_seq) == 0
    lo = lax.broadcasted_iota(jnp.int32, (1, LANES), 1) < HEAD_DIM

    @pl.when(i == 0)
    def _():
        base_s[...] = jnp.zeros_like(base_s)

    kv_all = jnp.concatenate([hkv_ref[...], p_ref[:, P_KV:P_KV + 2 * LANES]], axis=0)
    k = kv_all[:, :LANES].astype(F32)
    kr = pltpu.roll(k, HEAD_DIM, 1)
    ke_s[0] = jnp.where(lo, k, 0.0).astype(BF16)
    ko_s[0] = jnp.where(lo, 0.0, kr).astype(BF16)
    ke_s[1] = jnp.where(lo, kr, 0.0).astype(BF16)
    ko_s[1] = jnp.where(lo, 0.0, k).astype(BF16)
    v = kv_all[:, LANES:].astype(F32)
    n_keys = HALO + tm
    for j in range(n_keys // LANES):
        vta_s[:, j * LANES:(j + 1) * LANES] = v[j * LANES:(j + 1) * LANES].T.astype(BF16)
    for j in range(n_keys // LANES - 1):
        vtb_s[:, j * LANES:(j + 1) * LANES] = v[CHUNK + j * LANES:CHUNK + (j + 1) * LANES].T.astype(BF16)
    vtb_s[:, n_keys - LANES:n_keys - CHUNK] = v[n_keys - CHUNK:n_keys].T.astype(BF16)

    krow = lax.broadcasted_iota(jnp.int32, (2 * KEYS, 1), 0)
    krow = jnp.where(krow >= KEYS, krow - KEYS, krow)

    for c in range(tm // CHUNK):
        r0 = c * CHUNK
        vt_s, v0 = (vta_s, r0) if c % 2 == 0 else (vtb_s, r0 - CHUNK)
        for g in range(N_KV_HEADS):
            qbase = P_Q + g * 4 * LANES
            q = jnp.concatenate(
                [p_ref[r0:r0 + CHUNK, qbase + p * LANES:qbase + (p + 1) * LANES] for p in range(4)], axis=0)
            kk = jnp.concatenate([ke_s[g, r0:r0 + KEYS, :], ko_s[g, r0:r0 + KEYS, :]], axis=0)
            st = lax.dot_general(kk, q, NT, preferred_element_type=F32)
            if r0 < HALO:
                nbad = jnp.where(first, HALO - r0, 0)
                st = jnp.where(krow < nbad, -jnp.inf, st)
            vt = vt_s[g * HEAD_DIM:(g + 1) * HEAD_DIM, v0:v0 + KEYS]
            outs = []
            for par in range(2):
                s = st[par * KEYS:(par + 1) * KEYS]
                sc = sink_ref[2 * g + par:2 * g + par + 1, :]
                m = jnp.maximum(jnp.max(s, axis=0, keepdims=True), sc)
                p = jnp.exp(s - m)
                l = jnp.sum(p, axis=0, keepdims=True) + jnp.exp(sc - m)
                outs.append(jnp.dot(vt, p.astype(BF16), preferred_element_type=F32) / l)
            ot = jnp.concatenate(outs, axis=0)
            for p2 in range(2):
                blk = ot[:, p2 * LANES:(p2 + 1) * LANES].T
                for h in range(2):
                    c0 = (g * 4 + 2 * p2 + h) * LANES
                    attn_s[r0:r0 + CHUNK, c0:c0 + LANES] = blk[h * CHUNK:(h + 1) * CHUNK].astype(BF16)

    hu = hu_ref[...].astype(F32)
    row = lax.broadcasted_iota(jnp.int32, (tm, 1), 0)
    cw = cw_ref[...]
    nc = 256
    for j in range(D_MODEL // nc):
        cs = slice(j * nc, (j + 1) * nc)
        u = p_ref[:, P_U + j * nc:P_U + (j + 1) * nc].astype(F32)
        h1 = jnp.where(first, 0.0, hu[15:16, cs])
        h2 = jnp.where(first, 0.0, hu[14:15, cs])
        s1 = jnp.where(row == 0, h1, pltpu.roll(u, 1, 0))
        s2 = jnp.where(row == 0, h2, jnp.where(row == 1, h1, pltpu.roll(u, 2, 0)))
        y = cw[2:3, cs] * u + cw[1:2, cs] * s1 + cw[0:1, cs] * s2
        cc = p_ref[:, P_CC + j * nc:P_CC + (j + 1) * nc].astype(F32)
        ci_s[:, cs] = (cc * y).astype(BF16)

    conv_out = jnp.dot(ci_s[...], wco_ref[...], preferred_element_type=F32)
    attn_out = jnp.dot(attn_s[...], wao_ref[...], preferred_element_type=F32)
    gc = p_ref[:, P_G:P_G + D_MODEL].astype(F32)
    ga = p_ref[:, P_G + D_MODEL:P_G + 2 * D_MODEL].astype(F32)
    merged = (gc * conv_out + ga * attn_out).astype(BF16)
    h_new = x_ref[...] + jnp.dot(merged, wo_ref[...], preferred_element_type=F32)
    o_ref[...] = h_new
    _route_tile(tm, h_new, g2_ref, wt_ref, bt_ref, r_ref, cnt_ref, base_s)


def _mix(p, x2, conv_w, sink_tab, w_co, w_ao, w_o, g2, wt_router, bt_router, seq, tm):
    t = x2.shape[0]
    row = lambda i: (i, 0)
    fixed = lambda i: (0, 0)
    sq = (D_MODEL, D_MODEL)
    n_keys = HALO + tm
    return pl.pallas_call(
        functools.partial(_mix_kernel, tm, seq // tm),
        grid=(t // tm,),
        in_specs=[
            pl.BlockSpec((tm, P_COLS), row),
            pl.BlockSpec((HALO, 2 * LANES), lambda i: (jnp.maximum(i * (tm // HALO) - 1, 0), P_KV // (2 * LANES))),
            pl.BlockSpec((16, D_MODEL), lambda i: (jnp.maximum(i * (tm // 16) - 1, 0), P_U // D_MODEL)),
            pl.BlockSpec((tm, D_MODEL), row),
            pl.BlockSpec(conv_w.shape, fixed),
            pl.BlockSpec(sink_tab.shape, fixed),
            pl.BlockSpec(sq, fixed),
            pl.BlockSpec(sq, fixed),
            pl.BlockSpec(sq, fixed),
            pl.BlockSpec((1, D_MODEL), fixed),
            pl.BlockSpec((2 * LANES, D_MODEL), fixed),
            pl.BlockSpec((LANES, LANES), fixed),
        ],
        out_specs=[
            pl.BlockSpec((tm, D_MODEL), row),
            pl.BlockSpec((tm, LANES), row),
            pl.BlockSpec((LANES, LANES), fixed),
        ],
        out_shape=[
            jax.ShapeDtypeStruct((t, D_MODEL), F32),
            jax.ShapeDtypeStruct((t, LANES), F32),
            jax.ShapeDtypeStruct((LANES, LANES), F32),
        ],
        scratch_shapes=[
            pltpu.VMEM((N_KV_HEADS, n_keys, LANES), BF16),
            pltpu.VMEM((N_KV_HEADS, n_keys, LANES), BF16),
            pltpu.VMEM((2 * HEAD_DIM, n_keys), BF16),
            pltpu.VMEM((2 * HEAD_DIM, n_keys), BF16),
            pltpu.VMEM((tm, D_MODEL), BF16),
            pltpu.VMEM((tm, D_MODEL), BF16),
            pltpu.VMEM((LANES, LANES), F32),
        ],
        compiler_params=pltpu.CompilerParams(dimension_semantics=("arbitrary",), vmem_limit_bytes=VMEM_BIG),
        name="mix",
    )(p, p, p, x2, conv_w, sink_tab, w_co, w_ao, w_o, g2, wt_router, bt_router)


def _rows_copy(src_ref, dst_ref, sem, n):
    return pltpu.make_async_copy(src_ref.at[pl.ds(0, n)], dst_ref.at[pl.ds(0, n)], sem)


def _dispatch_kernel(tm, n_tiles, n_blocks, zm_ref, d_ref, h_ref, g_ref, xs_ref, hn_s, zbuf, zsem, sem):
    i = pl.program_id(0)

    @pl.when(i == 0)
    def _():
        zbuf[...] = jnp.zeros_like(zbuf)

        def zero_copy(b):
            r = pl.multiple_of(b * MOE_BLOCK, MOE_BLOCK)
            return pltpu.make_async_copy(zbuf, xs_ref.at[pl.ds(r, MOE_BLOCK)], zsem)

        def start(b, c):
            @pl.when(zm_ref[b] == 1)
            def _():
                zero_copy(b).start()
            return c

        def wait(b, c):
            @pl.when(zm_ref[b] == 1)
            def _():
                zero_copy(b).wait()
            return c

        lax.fori_loop(0, n_blocks, start, 0)
        lax.fori_loop(0, n_blocks, wait, 0)

    slot = i % 2
    hn = _rms(h_ref[...], g_ref[...])
    hn_s[slot] = _pack_pair(hn[:, :HALF], hn[:, HALF:])

    def body(t, c):
        for k in range(2):
            d = d_ref[2 * t + k]
            pltpu.make_async_copy(hn_s.at[slot, pl.ds(t, 1)], xs_ref.at[pl.ds(d, 1)], sem.at[slot]).start()
        return c

    lax.fori_loop(0, tm, body, 0, unroll=8)

    def drain(s):
        for _ in range(2):
            _rows_copy(hn_s.at[s], xs_ref, sem.at[s], tm).wait()

    @pl.when(i > 0)
    def _():
        drain(1 - slot)

    @pl.when(i == n_tiles - 1)
    def _():
        drain(slot)


def _dispatch(h2, g2, dest, zero_mask, n_rows, tm):
    t = h2.shape[0]
    n_tiles = t // tm
    n_blocks = n_rows // MOE_BLOCK
    return pl.pallas_call(
        functools.partial(_dispatch_kernel, tm, n_tiles, n_blocks),
        grid_spec=pltpu.PrefetchScalarGridSpec(
            num_scalar_prefetch=1,
            grid=(n_tiles,),
            in_specs=[
                pl.BlockSpec((2 * tm,), lambda i, zm: (i,), memory_space=pltpu.SMEM),
                pl.BlockSpec((tm, D_MODEL), lambda i, zm: (i, 0)),
                pl.BlockSpec((1, D_MODEL), lambda i, zm: (0, 0)),
            ],
            out_specs=pl.BlockSpec(memory_space=pl.ANY),
            scratch_shapes=[
                pltpu.VMEM((2, tm, HALF), U32),
                pltpu.VMEM((MOE_BLOCK, HALF), U32),
                pltpu.SemaphoreType.DMA(()),
                pltpu.SemaphoreType.DMA((2,)),
            ],
        ),
        out_shape=jax.ShapeDtypeStruct((n_rows, HALF), U32),
        compiler_params=pltpu.CompilerParams(dimension_semantics=("arbitrary",)),
        name="dispatch",
    )(zero_mask, dest, h2, g2)


def _expert_kernel(be_ref, nw_ref, nu_ref, x_ref, w1_ref, w3_ref, w2_ref, o_ref, w1_s, w3_s, w2_s):
    b = pl.program_id(0)

    @pl.when(b < nu_ref[0])
    def _():
        @pl.when(nw_ref[b] == 1)
        def _():
            w1_s[...] = w1_ref[0, 0].astype(BF16)
            w3_s[...] = w3_ref[0, 0].astype(BF16)
            w2_s[...] = w2_ref[0, 0].astype(BF16)

        xl, xh = _unpack_pair(x_ref[...])
        x = jnp.concatenate([xl, xh], axis=1).astype(BF16)
        h1 = jnp.dot(x, w1_s[...], preferred_element_type=F32)
        h3 = jnp.dot(x, w3_s[...], preferred_element_type=F32)
        hid = (h1 * jax.nn.sigmoid(h1) * h3).astype(BF16)
        y = jnp.dot(hid, w2_s[...], preferred_element_type=F32)
        o_ref[...] = _pack_pair(y[:, :HALF], y[:, HALF:])

    @pl.when(b >= nu_ref[0])
    def _():
        o_ref[...] = jnp.zeros_like(o_ref)


def _experts(xs, block_expert, new_expert, n_used, w1, w3, w2, layer):
    n_rows = xs.shape[0]
    wsel = lambda b, be, nw, nu: (layer, be[b], 0, 0)
    return pl.pallas_call(
        _expert_kernel,
        grid_spec=pltpu.PrefetchScalarGridSpec(
            num_scalar_prefetch=3,
            grid=(n_rows // MOE_BLOCK,),
            in_specs=[
                pl.BlockSpec((MOE_BLOCK, HALF), lambda b, be, nw, nu: (jnp.minimum(b, nu[0] - 1), 0)),
                pl.BlockSpec((1, 1, D_MODEL, EXPERT_FF), wsel),
                pl.BlockSpec((1, 1, D_MODEL, EXPERT_FF), wsel),
                pl.BlockSpec((1, 1, EXPERT_FF, D_MODEL), wsel),
            ],
            out_specs=pl.BlockSpec((MOE_BLOCK, HALF), lambda b, be, nw, nu: (b, 0)),
            scratch_shapes=[
                pltpu.VMEM((D_MODEL, EXPERT_FF), BF16),
                pltpu.VMEM((D_MODEL, EXPERT_FF), BF16),
                pltpu.VMEM((EXPERT_FF, D_MODEL), BF16),
            ],
        ),
        out_shape=jax.ShapeDtypeStruct((n_rows, HALF), U32),
        compiler_params=pltpu.CompilerParams(dimension_semantics=("arbitrary",), vmem_limit_bytes=VMEM_BIG),
        name="experts",
    )(block_expert, new_expert, n_used, xs, w1, w3, w2)


def _combine_kernel(tm, n_tiles, final, dcur_ref, dnext_ref, r_ref, h_ref, fg_ref, ys_ref, o_ref, ybuf, sem):
    i = pl.program_id(0)

    def issue(d_ref, slot):
        def body(t, c):
            for k in range(2):
                d = d_ref[2 * t + k]
                pltpu.make_async_copy(ys_ref.at[pl.ds(d, 1)], ybuf.at[slot, pl.ds(k * tm + t, 1)], sem.at[slot]).start()
            return c

        lax.fori_loop(0, tm, body, 0, unroll=8)

    @pl.when(i == 0)
    def _():
        issue(dcur_ref, 0)

    @pl.when(i + 1 < n_tiles)
    def _():
        issue(dnext_ref, (i + 1) % 2)

    slot = i % 2
    _rows_copy(ys_ref, ybuf.at[slot], sem.at[slot], 2 * tm).wait()
    r = r_ref[...]
    w0 = r[:, R_W0:R_W0 + 1]
    w1 = r[:, R_W1:R_W1 + 1]
    y0l, y0h = _unpack_pair(ybuf[slot, 0:tm, :])
    y1l, y1h = _unpack_pair(ybuf[slot, tm:2 * tm, :])
    moe = jnp.concatenate([w0 * y0l + w1 * y1l, w0 * y0h + w1 * y1h], axis=1)
    out = h_ref[...] + moe
    if final:
        out = _rms(out, fg_ref[...])
    o_ref[...] = out


def _combine(ys, dest, route, h2, final_g, final, tm):
    t = h2.shape[0]
    n_tiles = t // tm
    row = lambda i: (i, 0)
    return pl.pallas_call(
        functools.partial(_combine_kernel, tm, n_tiles, final),
        grid=(n_tiles,),
        in_specs=[
            pl.BlockSpec((2 * tm,), lambda i: (i,), memory_space=pltpu.SMEM),
            pl.BlockSpec((2 * tm,), lambda i: (jnp.minimum(i + 1, n_tiles - 1),), memory_space=pltpu.SMEM),
            pl.BlockSpec((tm, LANES), row),
            pl.BlockSpec((tm, D_MODEL), row),
            pl.BlockSpec((1, D_MODEL), lambda i: (0, 0)),
            pl.BlockSpec(memory_space=pl.ANY),
        ],
        out_specs=pl.BlockSpec((tm, D_MODEL), row),
        out_shape=jax.ShapeDtypeStruct((t, D_MODEL), F32),
        scratch_shapes=[pltpu.VMEM((2, 2 * tm, HALF), U32), pltpu.SemaphoreType.DMA((2,))],
        compiler_params=pltpu.CompilerParams(dimension_semantics=("arbitrary",)),
        name="combine",
    )(dest, dest, route, h2, final_g, ys)


def _rotary_table(positions):
    half = ROT_DIM // 2
    inv = ROPE_THETA ** (-jnp.arange(0, ROT_DIM, 2, dtype=F32) / ROT_DIM)
    ang = positions.reshape(-1).astype(F32)[:, None] * inv
    cs1 = jnp.concatenate([jnp.cos(ang), jnp.sin(ang), jnp.ones((ang.shape[0], 1), F32)], axis=1)
    lane = jnp.arange(LANES) % HEAD_DIM
    j = jnp.arange(2 * half + 1)[:, None]
    a = jnp.where(lane < half, j == lane, jnp.where(lane < ROT_DIM, j == lane - half, j == 2 * half))
    b = -((lane < half) & (j == half + lane)).astype(F32)
    c = ((lane >= half) & (lane < ROT_DIM) & (j == lane)).astype(F32)
    place = jnp.concatenate([a.astype(F32), b, c], axis=1)
    return jnp.dot(cs1, place, precision=lax.Precision.HIGHEST)


def _sink_table(sinks):
    s = sinks.astype(F32).reshape(N_KV_HEADS, 4, 2).transpose(0, 2, 1)
    s = jnp.broadcast_to(s[..., None], (N_KV_HEADS, 2, 4, CHUNK)).reshape(2 * N_KV_HEADS, 4 * CHUNK)
    return jnp.concatenate([s, jnp.zeros((8 - 2 * N_KV_HEADS, 4 * CHUNK), F32)], axis=0)


def _router_params(w_group, b_group, w_route, b_route):
    d = w_group.shape[0]
    pad = LANES - N_GROUPS - N_EXPERTS
    wt = jnp.concatenate([w_group.T, w_route.transpose(0, 2, 1).reshape(N_EXPERTS, d), jnp.zeros((pad, d), F32)], axis=0)
    hi = wt.astype(BF16)
    lo = (wt - hi.astype(F32)).astype(BF16)
    bias = jnp.concatenate([b_group, b_route.reshape(-1), jnp.zeros((pad,), F32)])
    return jnp.concatenate([hi, lo], axis=0), jnp.broadcast_to(bias[:, None], (LANES, LANES))


def _moe_plan(route, counts, n_tok):
    n_rows = -(-(2 * n_tok) // MOE_BLOCK) * MOE_BLOCK + N_EXPERTS * MOE_BLOCK
    n_blocks = n_rows // MOE_BLOCK
    ids = jnp.arange(N_EXPERTS, dtype=jnp.int32)
    cnt = counts[R_ROW0:R_ROW0 + N_EXPERTS, 0].astype(jnp.int32)
    padded = (cnt + MOE_BLOCK - 1) // MOE_BLOCK * MOE_BLOCK
    pad_end = jnp.sum(jnp.where(ids[None, :] <= ids[:, None], padded[None, :], 0), axis=1)
    pad_start = pad_end - padded
    e = route[:, R_E0:R_E1 + 1].astype(jnp.int32)
    rank = route[:, R_RANK0:R_RANK1 + 1].astype(jnp.int32)
    dest = (jnp.sum(jnp.where(e[..., None] == ids, pad_start, 0), axis=-1) + rank).reshape(-1)
    n_used = pad_end[-1] // MOE_BLOCK
    blk = jnp.arange(n_blocks, dtype=jnp.int32)
    be = jnp.minimum(jnp.sum((pad_end[None, :] <= blk[:, None] * MOE_BLOCK).astype(jnp.int32), axis=1), N_EXPERTS - 1)
    be_last = jnp.sum(jnp.where(blk == n_used - 1, be, 0))
    be = jnp.where(blk < n_used, be, be_last)
    new_expert = jnp.concatenate([jnp.ones((1,), jnp.int32), (be[1:] != be[:-1]).astype(jnp.int32)])
    last = pad_end // MOE_BLOCK - 1
    is_last = jnp.any((blk[:, None] == last[None, :]) & (padded[None, :] > 0), axis=1)
    zero_mask = (is_last | (blk >= n_used)).astype(jnp.int32)
    return n_rows, dest, be, new_expert, n_used.astype(jnp.int32).reshape(1), zero_mask


def kernel(x, positions, norm1_g, w_in, b_gate, conv_w, sinks, w_conv_out, w_attn_out, w_o, norm2_g, w_group, b_group, w_route, b_route, w1, w3, w2, final_g):
    b, s, d = x.shape
    t = b * s
    depth = w_in.shape[0]
    tm = 512
    rot = _rotary_table(positions)
    h = x.reshape(t, d)
    for l in range(depth):
        p = _proj(h, norm1_g[l][None], w_in[l].astype(BF16), b_gate[l][None], rot, tm)
        g2 = norm2_g[l][None]
        wt_router, bt_router = _router_params(w_group[l], b_group[l], w_route[l], b_route[l])
        h, route, counts = _mix(p, h, conv_w[l], _sink_table(sinks[l]), w_conv_out[l].astype(BF16),
                                w_attn_out[l].astype(BF16), w_o[l].astype(BF16), g2, wt_router, bt_router, s, tm)
        n_rows, dest, be, new_expert, n_used, zero_mask = _moe_plan(route, counts, t)
        xs = _dispatch(h, g2, dest, zero_mask, n_rows, tm)
        ys = _experts(xs, be, new_expert, n_used, w1, w3, w2, l)
        h = _combine(ys, dest, route, h, final_g[None], l == depth - 1, tm)
    return h.reshape(b, s, d)
```

```python
import functools

import jax
import jax.numpy as jnp
from jax import lax
from jax.experimental import pallas as pl
from jax.experimental.pallas import tpu as pltpu

F32 = jnp.float32
BF16 = jnp.bfloat16
U32 = jnp.uint32

D_MODEL = 1024
HALF = D_MODEL // 2
RMS_EPS = 1e-5
CHUNK = 64
WINDOW_CHUNKS = 2
HEAD_DIM = 64
N_Q_HEADS = 16
N_KV_HEADS = 2
ROT_DIM = 16
ROPE_THETA = 500000.0
N_GROUPS = 8
EXPERTS_PER_GROUP = 8
N_EXPERTS = 64
EXPERT_FF = 512
MOE_BLOCK = 256
LANES = 128

W_CB, W_CC, W_CX, W_Q, W_K, W_G = 0, 1024, 2048, 3072, 4096, 4352
P_U, P_CC, P_Q, P_KV, P_G = 0, 1024, 2048, 3072, 3328
P_COLS = 5376
HALO = WINDOW_CHUNKS * CHUNK
KEYS = HALO + CHUNK

VMEM_BIG = 56 * 1024 * 1024
NT = (((1,), (1,)), ((), ()))


def _rms(x, g):
    ms = jnp.mean(x * x, axis=-1, keepdims=True)
    return x * lax.rsqrt(ms + RMS_EPS) * g


def _pack_pair(lo, hi):
    def rne(x):
        b = lax.bitcast_convert_type(x, U32)
        return b + U32(0x7FFF) + ((b >> 16) & U32(1))

    return (rne(hi) & U32(0xFFFF0000)) | (rne(lo) >> 16)


def _unpack_pair(w):
    lo = lax.bitcast_convert_type(w << 16, F32)
    hi = lax.bitcast_convert_type(w & U32(0xFFFF0000), F32)
    return lo, hi


ROW_WORDS = HALF // LANES


def _store_rows(ref, lead, n, x):
    for j in range(ROW_WORDS):
        w = _pack_pair(x[:, j * LANES:(j + 1) * LANES], x[:, HALF + j * LANES:HALF + (j + 1) * LANES])
        ref[lead + (pl.ds(j, n, stride=ROW_WORDS), slice(None))] = w


def _load_rows(ref, lead, row0, n):
    parts = [_unpack_pair(ref[lead + (pl.ds(row0 * ROW_WORDS + j, n, stride=ROW_WORDS), slice(None))])
             for j in range(ROW_WORDS)]
    return jnp.concatenate([p[0] for p in parts] + [p[1] for p in parts], axis=1)


def _proj_kernel(x_ref, g_ref, w_ref, bg_ref, rot_ref, o_ref):
    _proj_body(_rms(x_ref[...], g_ref[...]).astype(BF16), w_ref, bg_ref, rot_ref, o_ref)


def _proj_body(xn, w_ref, bg_ref, rot_ref, o_ref):
    def mm(c0, n):
        return jnp.dot(xn, w_ref[:, c0:c0 + n], preferred_element_type=F32)

    ra = rot_ref[:, 0:LANES]
    rb = rot_ref[:, LANES:2 * LANES]
    rc = rot_ref[:, 2 * LANES:3 * LANES]

    def rot(t):
        return t * ra + pltpu.roll(t, LANES - ROT_DIM // 2, 1) * rb + pltpu.roll(t, ROT_DIM // 2, 1) * rc

    nc = 512
    for j in range(2):
        cb = mm(W_CB + j * nc, nc)
        cx = mm(W_CX + j * nc, nc)
        o_ref[:, P_U + j * nc:P_U + (j + 1) * nc] = (cb * cx).astype(BF16)
    for j in range(2):
        o_ref[:, P_CC + j * nc:P_CC + (j + 1) * nc] = mm(W_CC + j * nc, nc).astype(BF16)
    for j in range(2):
        q = mm(W_Q + j * nc, nc)
        for p in range(nc // LANES):
            t = rot(q[:, p * LANES:(p + 1) * LANES]) * (HEAD_DIM ** -0.5)
            o_ref[:, P_Q + j * nc + p * LANES:P_Q + j * nc + (p + 1) * LANES] = t.astype(BF16)
    kv = mm(W_K, 2 * LANES)
    o_ref[:, P_KV:P_KV + LANES] = rot(kv[:, :LANES]).astype(BF16)
    o_ref[:, P_KV + LANES:P_KV + 2 * LANES] = kv[:, LANES:].astype(BF16)
    for j in range(4):
        gt = mm(W_G + j * nc, nc) + bg_ref[:, j * nc:(j + 1) * nc]
        o_ref[:, P_G + j * nc:P_G + (j + 1) * nc] = jax.nn.sigmoid(gt).astype(BF16)


def _proj(x2, g1, w_in, b_gate, rot, tm):
    t = x2.shape[0]
    row = lambda i: (i, 0)
    fixed = lambda i: (0, 0)
    return pl.pallas_call(
        _proj_kernel,
        grid=(t // tm,),
        in_specs=[
            pl.BlockSpec((tm, D_MODEL), row),
            pl.BlockSpec((1, D_MODEL), fixed),
            pl.BlockSpec(w_in.shape, fixed, pipeline_mode=pl.Buffered(1)),
            pl.BlockSpec((1, 2 * D_MODEL), fixed),
            pl.BlockSpec((tm, 3 * LANES), row),
        ],
        out_specs=pl.BlockSpec((tm, P_COLS), row),
        out_shape=jax.ShapeDtypeStruct((t, P_COLS), BF16),
        compiler_params=pltpu.CompilerParams(dimension_semantics=("arbitrary",), vmem_limit_bytes=VMEM_BIG),
        name="proj",
    )(x2, g1, w_in, b_gate, rot)


R_E0, R_E1, R_RANK0, R_RANK1, R_W0, R_W1 = range(6)
R_ROW0 = N_GROUPS


def _route_tile(tm, h, g_ref, wt_ref, bt_ref, r_ref, cnt_ref, base_s):
    hn = _rms(h, g_ref[...])
    hi = hn.astype(BF16)
    lo = (hn - hi.astype(F32)).astype(BF16)
    both = lax.dot_general(wt_ref[...], hi, NT, preferred_element_type=F32)
    lt = (both[:LANES] + both[LANES:] + lax.dot_general(wt_ref[0:LANES, :], lo, NT, preferred_element_type=F32)
          + bt_ref[:, 0:1])
    row = lax.broadcasted_iota(jnp.int32, (LANES, tm), 0)
    neg = -jnp.inf

    def first_max(v):
        m = jnp.max(v, axis=0, keepdims=True)
        return m, jnp.min(jnp.where(v == m, row, LANES), axis=0, keepdims=True)

    gl = jnp.where(row < N_GROUPS, lt, neg)
    gmax, gtop = first_max(gl)
    pg = 1.0 / jnp.sum(jnp.exp(gl - gmax), axis=0, keepdims=True)
    in_group = (row >= R_ROW0) & (row < R_ROW0 + N_EXPERTS) & (((row - R_ROW0) >> 3) == gtop)
    el = jnp.where(in_group, lt, neg)
    m1, i1 = first_max(el)
    m2, i2 = first_max(jnp.where(row == i1, neg, el))
    e2 = jnp.exp(m2 - m1)
    den = 1.0 + e2
    w0 = pg * (1.0 / den)
    w1 = pg * (e2 / den)
    hit0 = row == i1
    hit1 = row == i2
    a = jnp.where(hit0 | hit1, 1.0, 0.0)
    rr = lax.broadcasted_iota(jnp.int32, (tm, tm), 0)
    cc = lax.broadcasted_iota(jnp.int32, (tm, tm), 1)
    tri = jnp.where(rr < cc, 1.0, 0.0).astype(BF16)
    before = jnp.dot(a.astype(BF16), tri, preferred_element_type=F32) + base_s[:, 0:1]
    rank0 = jnp.sum(jnp.where(hit0, before, 0.0), axis=0, keepdims=True)
    rank1 = jnp.sum(jnp.where(hit1, before, 0.0), axis=0, keepdims=True)
    base_s[...] = base_s[...] + jnp.sum(a, axis=1, keepdims=True)
    cnt_ref[...] = base_s[...]
    vals = ((i1 - R_ROW0).astype(F32), (i2 - R_ROW0).astype(F32), rank0, rank1, w0, w1)
    slab = jnp.zeros((LANES, tm), F32)
    for k, v in enumerate(vals):
        slab = jnp.where(row == k, v, slab)
    r_ref[...] = slab.T
    return hn


def _mix_kernel(tm, tiles_per_seq, p_ref, hkv_ref, hu_ref, x_ref, cw_ref, sink_ref, wco_ref, wao_ref, wo_ref,
                g2_ref, wt_ref, bt_ref, o_ref, r_ref, cnt_ref, hp_ref, ke_s, ko_s, vta_s, vtb_s, attn_s, ci_s, base_s):
    i = pl.program_id(0)
    first = (i % tiles_per_seq) == 0
    lo = lax.broadcasted_iota(jnp.int32, (1, LANES), 1) < HEAD_DIM

    @pl.when(i == 0)
    def _():
        base_s[...] = jnp.zeros_like(base_s)

    kv_all = jnp.concatenate([hkv_ref[...], p_ref[:, P_KV:P_KV + 2 * LANES]], axis=0)
    k = kv_all[:, :LANES].astype(F32)
    kr = pltpu.roll(k, HEAD_DIM, 1)
    ke_s[0] = jnp.where(lo, k, 0.0).astype(BF16)
    ko_s[0] = jnp.where(lo, 0.0, kr).astype(BF16)
    ke_s[1] = jnp.where(lo, kr, 0.0).astype(BF16)
    ko_s[1] = jnp.where(lo, 0.0, k).astype(BF16)
    v = kv_all[:, LANES:].astype(F32)
    n_keys = HALO + tm
    for j in range(n_keys // LANES):
        vta_s[:, j * LANES:(j + 1) * LANES] = v[j * LANES:(j + 1) * LANES].T.astype(BF16)
    for j in range(n_keys // LANES - 1):
        vtb_s[:, j * LANES:(j + 1) * LANES] = v[CHUNK + j * LANES:CHUNK + (j + 1) * LANES].T.astype(BF16)
    vtb_s[:, n_keys - LANES:n_keys - CHUNK] = v[n_keys - CHUNK:n_keys].T.astype(BF16)

    krow = lax.broadcasted_iota(jnp.int32, (2 * KEYS, 1), 0)
    krow = jnp.where(krow >= KEYS, krow - KEYS, krow)

    for c in range(tm // CHUNK):
        r0 = c * CHUNK
        vt_s, v0 = (vta_s, r0) if c % 2 == 0 else (vtb_s, r0 - CHUNK)
        for g in range(N_KV_HEADS):
            qbase = P_Q + g * 4 * LANES
            q = jnp.concatenate(
                [p_ref[r0:r0 + CHUNK, qbase + p * LANES:qbase + (p + 1) * LANES] for p in range(4)], axis=0)
            kk = jnp.concatenate([ke_s[g, r0:r0 + KEYS, :], ko_s[g, r0:r0 + KEYS, :]], axis=0)
            st = lax.dot_general(kk, q, NT, preferred_element_type=F32)
            if r0 < HALO:
                nbad = jnp.where(first, HALO - r0, 0)
                st = jnp.where(krow < nbad, -jnp.inf, st)
            vt = vt_s[g * HEAD_DIM:(g + 1) * HEAD_DIM, v0:v0 + KEYS]
            outs = []
            for par in range(2):
                s = st[par * KEYS:(par + 1) * KEYS]
                sc = sink_ref[2 * g + par:2 * g + par + 1, :]
                m = jnp.maximum(jnp.max(s, axis=0, keepdims=True), sc)
                p = jnp.exp(s - m)
                l = jnp.sum(p, axis=0, keepdims=True) + jnp.exp(sc - m)
                outs.append(jnp.dot(vt, p.astype(BF16), preferred_element_type=F32) / l)
            ot = jnp.concatenate(outs, axis=0)
            for p2 in range(2):
                blk = ot[:, p2 * LANES:(p2 + 1) * LANES].T
                for h in range(2):
                    c0 = (g * 4 + 2 * p2 + h) * LANES
                    attn_s[r0:r0 + CHUNK, c0:c0 + LANES] = blk[h * CHUNK:(h + 1) * CHUNK].astype(BF16)

    hu = hu_ref[...].astype(F32)
    row = lax.broadcasted_iota(jnp.int32, (tm, 1), 0)
    cw = cw_ref[...]
    nc = 256
    for j in range(D_MODEL // nc):
        cs = slice(j * nc, (j + 1) * nc)
        u = p_ref[:, P_U + j * nc:P_U + (j + 1) * nc].astype(F32)
        h1 = jnp.where(first, 0.0, hu[15:16, cs])
        h2 = jnp.where(first, 0.0, hu[14:15, cs])
        s1 = jnp.where(row == 0, h1, pltpu.roll(u, 1, 0))
        s2 = jnp.where(row == 0, h2, jnp.where(row == 1, h1, pltpu.roll(u, 2, 0)))
        y = cw[2:3, cs] * u + cw[1:2, cs] * s1 + cw[0:1, cs] * s2
        cc = p_ref[:, P_CC + j * nc:P_CC + (j + 1) * nc].astype(F32)
        ci_s[:, cs] = (cc * y).astype(BF16)

    conv_out = jnp.dot(ci_s[...], wco_ref[...], preferred_element_type=F32)
    attn_out = jnp.dot(attn_s[...], wao_ref[...], preferred_element_type=F32)
    gc = p_ref[:, P_G:P_G + D_MODEL].astype(F32)
    ga = p_ref[:, P_G + D_MODEL:P_G + 2 * D_MODEL].astype(F32)
    merged = (gc * conv_out + ga * attn_out).astype(BF16)
    h_new = x_ref[...] + jnp.dot(merged, wo_ref[...], preferred_element_type=F32)
    o_ref[...] = h_new
    hn = _route_tile(tm, h_new, g2_ref, wt_ref, bt_ref, r_ref, cnt_ref, base_s)
    _store_rows(hp_ref, (), tm, hn)


def _mix(p, x2, conv_w, sink_tab, w_co, w_ao, w_o, g2, wt_router, bt_router, seq, tm):
    t = x2.shape[0]
    row = lambda i: (i, 0)
    fixed = lambda i: (0, 0)
    sq = (D_MODEL, D_MODEL)
    n_keys = HALO + tm
    return pl.pallas_call(
        functools.partial(_mix_kernel, tm, seq // tm),
        grid=(t // tm,),
        in_specs=[
            pl.BlockSpec((tm, P_COLS), row),
            pl.BlockSpec((HALO, 2 * LANES), lambda i: (jnp.maximum(i * (tm // HALO) - 1, 0), P_KV // (2 * LANES))),
            pl.BlockSpec((16, D_MODEL), lambda i: (jnp.maximum(i * (tm // 16) - 1, 0), P_U // D_MODEL)),
            pl.BlockSpec((tm, D_MODEL), row),
            pl.BlockSpec(conv_w.shape, fixed),
            pl.BlockSpec(sink_tab.shape, fixed),
            pl.BlockSpec(sq, fixed),
            pl.BlockSpec(sq, fixed),
            pl.BlockSpec(sq, fixed),
            pl.BlockSpec((1, D_MODEL), fixed),
            pl.BlockSpec((2 * LANES, D_MODEL), fixed),
            pl.BlockSpec((LANES, LANES), fixed),
        ],
        out_specs=[
            pl.BlockSpec((tm, D_MODEL), row),
            pl.BlockSpec((tm, LANES), row),
            pl.BlockSpec((LANES, LANES), fixed),
            pl.BlockSpec((tm * ROW_WORDS, LANES), row),
        ],
        out_shape=[
            jax.ShapeDtypeStruct((t, D_MODEL), F32),
            jax.ShapeDtypeStruct((t, LANES), F32),
            jax.ShapeDtypeStruct((LANES, LANES), F32),
            jax.ShapeDtypeStruct((t * ROW_WORDS, LANES), U32),
        ],
        scratch_shapes=[
            pltpu.VMEM((N_KV_HEADS, n_keys, LANES), BF16),
            pltpu.VMEM((N_KV_HEADS, n_keys, LANES), BF16),
            pltpu.VMEM((2 * HEAD_DIM, n_keys), BF16),
            pltpu.VMEM((2 * HEAD_DIM, n_keys), BF16),
            pltpu.VMEM((tm, D_MODEL), BF16),
            pltpu.VMEM((tm, D_MODEL), BF16),
            pltpu.VMEM((LANES, LANES), F32),
        ],
        compiler_params=pltpu.CompilerParams(dimension_semantics=("arbitrary",), vmem_limit_bytes=VMEM_BIG),
        name="mix",
    )(p, p, p, x2, conv_w, sink_tab, w_co, w_ao, w_o, g2, wt_router, bt_router)


def _rows_copy(src_ref, dst_ref, sem, n):
    return pltpu.make_async_copy(src_ref.at[pl.ds(0, n)], dst_ref.at[pl.ds(0, n)], sem)


def _dispatch_kernel(tm, n_tiles, n_blocks, zm_ref, d_ref, hp_ref, xs_ref, hn_s, zbuf, zsem, sem):
    i = pl.program_id(0)
    blk_rows = MOE_BLOCK * ROW_WORDS

    @pl.when(i == 0)
    def _():
        zbuf[...] = jnp.zeros_like(zbuf)

        def zero_copy(b):
            r = pl.multiple_of(b * blk_rows, blk_rows)
            return pltpu.make_async_copy(zbuf, xs_ref.at[pl.ds(r, blk_rows)], zsem)

        def start(b, c):
            @pl.when(zm_ref[b] == 1)
            def _():
                zero_copy(b).start()
            return c

        def wait(b, c):
            @pl.when(zm_ref[b] == 1)
            def _():
                zero_copy(b).wait()
            return c

        lax.fori_loop(0, n_blocks, start, 0)
        lax.fori_loop(0, n_blocks, wait, 0)

    slot = i % 2
    hn_s[slot] = hp_ref[...]

    def body(t, c):
        src = hn_s.at[slot, pl.ds(pl.multiple_of(t * ROW_WORDS, ROW_WORDS), ROW_WORDS)]
        for k in range(2):
            d = pl.multiple_of(d_ref[2 * t + k] * ROW_WORDS, ROW_WORDS)
            pltpu.make_async_copy(src, xs_ref.at[pl.ds(d, ROW_WORDS)], sem.at[slot]).start(priority=k)
        return c

    lax.fori_loop(0, tm, body, 0, unroll=8)

    def drain(s):
        for _ in range(2):
            _rows_copy(hn_s.at[s], xs_ref, sem.at[s], tm * ROW_WORDS).wait()

    @pl.when(i > 0)
    def _():
        drain(1 - slot)

    @pl.when(i == n_tiles - 1)
    def _():
        drain(slot)


def _dispatch(hp, dest, zero_mask, n_rows, tm):
    n_tiles = hp.shape[0] // (tm * ROW_WORDS)
    n_blocks = n_rows // MOE_BLOCK
    return pl.pallas_call(
        functools.partial(_dispatch_kernel, tm, n_tiles, n_blocks),
        grid_spec=pltpu.PrefetchScalarGridSpec(
            num_scalar_prefetch=1,
            grid=(n_tiles,),
            in_specs=[
                pl.BlockSpec((2 * tm,), lambda i, zm: (i,), memory_space=pltpu.SMEM),
                pl.BlockSpec((tm * ROW_WORDS, LANES), lambda i, zm: (i, 0)),
            ],
            out_specs=pl.BlockSpec(memory_space=pl.ANY),
            scratch_shapes=[
                pltpu.VMEM((2, tm * ROW_WORDS, LANES), U32),
                pltpu.VMEM((MOE_BLOCK * ROW_WORDS, LANES), U32),
                pltpu.SemaphoreType.DMA(()),
                pltpu.SemaphoreType.DMA((2,)),
            ],
        ),
        out_shape=jax.ShapeDtypeStruct((n_rows * ROW_WORDS, LANES), U32),
        compiler_params=pltpu.CompilerParams(dimension_semantics=("arbitrary",)),
        name="dispatch",
    )(zero_mask, dest, hp)


def _expert_kernel(be_ref, nw_ref, nu_ref, x_ref, w1_ref, w3_ref, w2_ref, o_ref, w1_s, w3_s, w2_s):
    b = pl.program_id(0)

    @pl.when(b < nu_ref[0])
    def _():
        @pl.when(nw_ref[b] == 1)
        def _():
            w1_s[...] = w1_ref[0, 0].astype(BF16)
            w3_s[...] = w3_ref[0, 0].astype(BF16)
            w2_s[...] = w2_ref[0, 0].astype(BF16)

        x = _load_rows(x_ref, (), 0, MOE_BLOCK).astype(BF16)
        h1 = jnp.dot(x, w1_s[...], preferred_element_type=F32)
        h3 = jnp.dot(x, w3_s[...], preferred_element_type=F32)
        hid = (h1 * jax.nn.sigmoid(h1) * h3).astype(BF16)
        y = jnp.dot(hid, w2_s[...], preferred_element_type=F32)
        _store_rows(o_ref, (), MOE_BLOCK, y)

    @pl.when(b >= nu_ref[0])
    def _():
        o_ref[...] = jnp.zeros_like(o_ref)


def _experts(xs, block_expert, new_expert, n_used, w1, w3, w2, layer):
    n_rows = xs.shape[0] // ROW_WORDS
    blk = (MOE_BLOCK * ROW_WORDS, LANES)
    wsel = lambda b, be, nw, nu: (layer, be[b], 0, 0)
    return pl.pallas_call(
        _expert_kernel,
        grid_spec=pltpu.PrefetchScalarGridSpec(
            num_scalar_prefetch=3,
            grid=(n_rows // MOE_BLOCK,),
            in_specs=[
                pl.BlockSpec(blk, lambda b, be, nw, nu: (jnp.minimum(b, nu[0] - 1), 0)),
                pl.BlockSpec((1, 1, D_MODEL, EXPERT_FF), wsel),
                pl.BlockSpec((1, 1, D_MODEL, EXPERT_FF), wsel),
                pl.BlockSpec((1, 1, EXPERT_FF, D_MODEL), wsel),
            ],
            out_specs=pl.BlockSpec(blk, lambda b, be, nw, nu: (b, 0)),
            scratch_shapes=[
                pltpu.VMEM((D_MODEL, EXPERT_FF), BF16),
                pltpu.VMEM((D_MODEL, EXPERT_FF), BF16),
                pltpu.VMEM((EXPERT_FF, D_MODEL), BF16),
            ],
        ),
        out_shape=jax.ShapeDtypeStruct(xs.shape, U32),
        compiler_params=pltpu.CompilerParams(dimension_semantics=("arbitrary",), vmem_limit_bytes=VMEM_BIG),
        name="experts",
    )(block_expert, new_expert, n_used, xs, w1, w3, w2)


def _issue_row_gathers(tm, d_ref, ys_ref, ybuf, sem, slot, unrolled):
    def one(t):
        for k in range(2):
            d = pl.multiple_of(d_ref[2 * t + k] * ROW_WORDS, ROW_WORDS)
            r = (k * tm + t) * ROW_WORDS
            if not isinstance(r, int):
                r = pl.multiple_of(r, ROW_WORDS)
            pltpu.make_async_copy(
                ys_ref.at[pl.ds(d, ROW_WORDS)], ybuf.at[slot, pl.ds(r, ROW_WORDS)], sem.at[slot]).start(priority=k)

    if unrolled:
        for t in range(tm):
            one(t)
    else:
        def body(t, c):
            one(t)
            return c

        lax.fori_loop(0, tm, body, 0, unroll=8)


def _combined_tile(tm, r_ref, h_ref, ys_ref, ybuf, sem, slot):
    _rows_copy(ys_ref, ybuf.at[slot], sem.at[slot], 2 * tm * ROW_WORDS).wait()
    r = r_ref[...]
    w0 = r[:, R_W0:R_W0 + 1]
    w1 = r[:, R_W1:R_W1 + 1]
    return h_ref[...] + (w0 * _load_rows(ybuf, (slot,), 0, tm) + w1 * _load_rows(ybuf, (slot,), tm, tm))


def _combine_kernel(tm, n_tiles, dcur_ref, dnext_ref, r_ref, h_ref, fg_ref, ys_ref, o_ref, ybuf, sem):
    i = pl.program_id(0)

    @pl.when(i == 0)
    def _():
        _issue_row_gathers(tm, dcur_ref, ys_ref, ybuf, sem, 0, False)

    @pl.when(i + 1 < n_tiles)
    def _():
        _issue_row_gathers(tm, dnext_ref, ys_ref, ybuf, sem, (i + 1) % 2, False)

    o_ref[...] = _rms(_combined_tile(tm, r_ref, h_ref, ys_ref, ybuf, sem, i % 2), fg_ref[...])


def _combine_specs(tm, n_tiles):
    row = lambda i: (i, 0)
    return [
        pl.BlockSpec((2 * tm,), lambda i: (i,), memory_space=pltpu.SMEM),
        pl.BlockSpec((2 * tm,), lambda i: (jnp.minimum(i + 1, n_tiles - 1),), memory_space=pltpu.SMEM),
        pl.BlockSpec((tm, LANES), row),
        pl.BlockSpec((tm, D_MODEL), row),
    ]


def _combine_scratch(tm):
    return [pltpu.VMEM((2, 2 * tm * ROW_WORDS, LANES), U32), pltpu.SemaphoreType.DMA((2,))]


def _combine_final(ys, dest, route, h2, final_g, tm):
    t = h2.shape[0]
    n_tiles = t // tm
    return pl.pallas_call(
        functools.partial(_combine_kernel, tm, n_tiles),
        grid=(n_tiles,),
        in_specs=_combine_specs(tm, n_tiles)
        + [pl.BlockSpec((1, D_MODEL), lambda i: (0, 0)), pl.BlockSpec(memory_space=pl.ANY)],
        out_specs=pl.BlockSpec((tm, D_MODEL), lambda i: (i, 0)),
        out_shape=jax.ShapeDtypeStruct((t, D_MODEL), F32),
        scratch_shapes=_combine_scratch(tm),
        compiler_params=pltpu.CompilerParams(dimension_semantics=("arbitrary",)),
        name="combine",
    )(dest, dest, route, h2, final_g, ys)


def _combine_proj_kernel(tm, n_tiles, dcur_ref, dnext_ref, r_ref, hm_ref, g_ref, w_ref, bg_ref, rot_ref, ys_ref,
                         o_ref, h_ref, ybuf, sem):
    i = pl.program_id(0)
    slot = i % 2

    @pl.when(i == 0)
    def _():
        _issue_row_gathers(tm, dcur_ref, ys_ref, ybuf, sem, 0, False)

    h = _combined_tile(tm, r_ref, hm_ref, ys_ref, ybuf, sem, slot)
    h_ref[...] = h
    _issue_row_gathers(tm, dnext_ref, ys_ref, ybuf, sem, 1 - slot, True)
    _proj_body(_rms(h, g_ref[...]).astype(BF16), w_ref, bg_ref, rot_ref, o_ref)

    @pl.when(i == n_tiles - 1)
    def _():
        _rows_copy(ys_ref, ybuf.at[1 - slot], sem.at[1 - slot], 2 * tm * ROW_WORDS).wait()


def _combine_proj(ys, dest, route, h2, g1, w_in, b_gate, rot, tm):
    t = h2.shape[0]
    n_tiles = t // tm
    row = lambda i: (i, 0)
    fixed = lambda i: (0, 0)
    return pl.pallas_call(
        functools.partial(_combine_proj_kernel, tm, n_tiles),
        grid=(n_tiles,),
        in_specs=_combine_specs(tm, n_tiles) + [
            pl.BlockSpec((1, D_MODEL), fixed),
            pl.BlockSpec(w_in.shape, fixed, pipeline_mode=pl.Buffered(1)),
            pl.BlockSpec((1, 2 * D_MODEL), fixed),
            pl.BlockSpec((tm, 3 * LANES), row),
            pl.BlockSpec(memory_space=pl.ANY),
        ],
        out_specs=[pl.BlockSpec((tm, P_COLS), row), pl.BlockSpec((tm, D_MODEL), row)],
        out_shape=[jax.ShapeDtypeStruct((t, P_COLS), BF16), jax.ShapeDtypeStruct((t, D_MODEL), F32)],
        scratch_shapes=_combine_scratch(tm),
        compiler_params=pltpu.CompilerParams(dimension_semantics=("arbitrary",), vmem_limit_bytes=VMEM_BIG),
        name="combine_proj",
    )(dest, dest, route, h2, g1, w_in, b_gate, rot, ys)


def _rotary_table(positions):
    half = ROT_DIM // 2
    inv = ROPE_THETA ** (-jnp.arange(0, ROT_DIM, 2, dtype=F32) / ROT_DIM)
    ang = positions.reshape(-1).astype(F32)[:, None] * inv
    cs1 = jnp.concatenate([jnp.cos(ang), jnp.sin(ang), jnp.ones((ang.shape[0], 1), F32)], axis=1)
    lane = jnp.arange(LANES) % HEAD_DIM
    j = jnp.arange(2 * half + 1)[:, None]
    a = jnp.where(lane < half, j == lane, jnp.where(lane < ROT_DIM, j == lane - half, j == 2 * half))
    b = -((lane < half) & (j == half + lane)).astype(F32)
    c = ((lane >= half) & (lane < ROT_DIM) & (j == lane)).astype(F32)
    place = jnp.concatenate([a.astype(F32), b, c], axis=1)
    return jnp.dot(cs1, place, precision=lax.Precision.HIGHEST)


def _sink_table(sinks):
    s = sinks.astype(F32).reshape(N_KV_HEADS, 4, 2).transpose(0, 2, 1)
    s = jnp.broadcast_to(s[..., None], (N_KV_HEADS, 2, 4, CHUNK)).reshape(2 * N_KV_HEADS, 4 * CHUNK)
    return jnp.concatenate([s, jnp.zeros((8 - 2 * N_KV_HEADS, 4 * CHUNK), F32)], axis=0)


def _router_params(w_group, b_group, w_route, b_route):
    d = w_group.shape[0]
    pad = LANES - N_GROUPS - N_EXPERTS
    wt = jnp.concatenate([w_group.T, w_route.transpose(0, 2, 1).reshape(N_EXPERTS, d), jnp.zeros((pad, d), F32)], axis=0)
    hi = wt.astype(BF16)
    lo = (wt - hi.astype(F32)).astype(BF16)
    bias = jnp.concatenate([b_group, b_route.reshape(-1), jnp.zeros((pad,), F32)])
    return jnp.concatenate([hi, lo], axis=0), jnp.broadcast_to(bias[:, None], (LANES, LANES))


def _moe_plan(route, counts, n_tok):
    n_rows = -(-(2 * n_tok) // MOE_BLOCK) * MOE_BLOCK + N_EXPERTS * MOE_BLOCK
    n_blocks = n_rows // MOE_BLOCK
    ids = jnp.arange(N_EXPERTS, dtype=jnp.int32)
    cnt = counts[R_ROW0:R_ROW0 + N_EXPERTS, 0].astype(jnp.int32)
    padded = (cnt + MOE_BLOCK - 1) // MOE_BLOCK * MOE_BLOCK
    pad_end = jnp.sum(jnp.where(ids[None, :] <= ids[:, None], padded[None, :], 0), axis=1)
    pad_start = pad_end - padded
    e = route[:, R_E0:R_E1 + 1].astype(jnp.int32)
    rank = route[:, R_RANK0:R_RANK1 + 1].astype(jnp.int32)
    dest = (jnp.sum(jnp.where(e[..., None] == ids, pad_start, 0), axis=-1) + rank).reshape(-1)
    n_used = pad_end[-1] // MOE_BLOCK
    blk = jnp.arange(n_blocks, dtype=jnp.int32)
    be = jnp.minimum(jnp.sum((pad_end[None, :] <= blk[:, None] * MOE_BLOCK).astype(jnp.int32), axis=1), N_EXPERTS - 1)
    be_last = jnp.sum(jnp.where(blk == n_used - 1, be, 0))
    be = jnp.where(blk < n_used, be, be_last)
    new_expert = jnp.concatenate([jnp.ones((1,), jnp.int32), (be[1:] != be[:-1]).astype(jnp.int32)])
    last = pad_end // MOE_BLOCK - 1
    is_last = jnp.any((blk[:, None] == last[None, :]) & (padded[None, :] > 0), axis=1)
    zero_mask = (is_last | (blk >= n_used)).astype(jnp.int32)
    return n_rows, dest, be, new_expert, n_used.astype(jnp.int32).reshape(1), zero_mask


def kernel(x, positions, norm1_g, w_in, b_gate, conv_w, sinks, w_conv_out, w_attn_out, w_o, norm2_g, w_group, b_group, w_route, b_route, w1, w3, w2, final_g):
    b, s, d = x.shape
    t = b * s
    depth = w_in.shape[0]
    tm = 512
    rot = _rotary_table(positions)
    h = x.reshape(t, d)
    ys = dest = route = None
    for l in range(depth):
        w_in_l = w_in[l].astype(BF16)
        if l == 0:
            p = _proj(h, norm1_g[l][None], w_in_l, b_gate[l][None], rot, tm)
        else:
            p, h = _combine_proj(ys, dest, route, h, norm1_g[l][None], w_in_l, b_gate[l][None], rot, tm)
        g2 = norm2_g[l][None]
        wt_router, bt_router = _router_params(w_group[l], b_group[l], w_route[l], b_route[l])
        h, route, counts, hp = _mix(p, h, conv_w[l], _sink_table(sinks[l]), w_conv_out[l].astype(BF16),
                                    w_attn_out[l].astype(BF16), w_o[l].astype(BF16), g2, wt_router, bt_router, s, tm)
        n_rows, dest, be, new_expert, n_used, zero_mask = _moe_plan(route, counts, t)
        xs = _dispatch(hp, dest, zero_mask, n_rows, tm)
        ys = _experts(xs, be, new_expert, n_used, w1, w3, w2, l)
    return _combine_final(ys, dest, route, h, final_g[None], tm).reshape(b, s, d)
```

```python
import functools

import jax
import jax.numpy as jnp
from jax import lax
from jax.experimental import pallas as pl
from jax.experimental.pallas import tpu as pltpu

F32 = jnp.float32
BF16 = jnp.bfloat16
U32 = jnp.uint32

D_MODEL = 1024
HALF = D_MODEL // 2
RMS_EPS = 1e-5
CHUNK = 64
WINDOW_CHUNKS = 2
HEAD_DIM = 64
N_Q_HEADS = 16
N_KV_HEADS = 2
ROT_DIM = 16
ROPE_THETA = 500000.0
N_GROUPS = 8
EXPERTS_PER_GROUP = 8
N_EXPERTS = 64
EXPERT_FF = 512
MOE_BLOCK = 256
LANES = 128

W_CB, W_CC, W_CX, W_Q, W_K, W_G = 0, 1024, 2048, 3072, 4096, 4352
P_CI, P_Q, P_KV, P_G = 0, 1024, 2048, 2304
P_COLS = 4352
HALO = WINDOW_CHUNKS * CHUNK
KEYS = HALO + CHUNK

VMEM_BIG = 56 * 1024 * 1024
NT = (((1,), (1,)), ((), ()))


def _rms(x, g):
    ms = jnp.mean(x * x, axis=-1, keepdims=True)
    return x * lax.rsqrt(ms + RMS_EPS) * g


def _pack_pair(lo, hi):
    def rnd(x):
        return lax.bitcast_convert_type(x, U32) + U32(0x8000)

    return (rnd(hi) & U32(0xFFFF0000)) | (rnd(lo) >> 16)


def _unpack_pair(w):
    lo = lax.bitcast_convert_type(w << 16, F32)
    hi = lax.bitcast_convert_type(w & U32(0xFFFF0000), F32)
    return lo, hi


ROW_WORDS = HALF // LANES


def _store_rows(ref, lead, n, x):
    for j in range(ROW_WORDS):
        w = _pack_pair(x[:, j * LANES:(j + 1) * LANES], x[:, HALF + j * LANES:HALF + (j + 1) * LANES])
        ref[lead + (pl.ds(j, n, stride=ROW_WORDS), slice(None))] = w


def _load_rows(ref, lead, row0, n):
    parts = [_unpack_pair(ref[lead + (pl.ds(row0 * ROW_WORDS + j, n, stride=ROW_WORDS), slice(None))])
             for j in range(ROW_WORDS)]
    return jnp.concatenate([p[0] for p in parts] + [p[1] for p in parts], axis=1)


def _proj_kernel(tiles_per_seq, x_ref, g_ref, w_ref, bg_ref, rot_ref, cw_ref, o_ref, carry_s):
    first = (pl.program_id(0) % tiles_per_seq) == 0

    @pl.when(pl.program_id(0) == 0)
    def _():
        carry_s[...] = jnp.zeros_like(carry_s)

    _proj_body(_rms(x_ref[...], g_ref[...]).astype(BF16), first, w_ref, bg_ref, rot_ref, cw_ref, o_ref, carry_s)


def _proj_body(xn, first, w_ref, bg_ref, rot_ref, cw_ref, o_ref, carry_s):
    tm = xn.shape[0]

    def mm(c0, n):
        return jnp.dot(xn, w_ref[:, c0:c0 + n], preferred_element_type=F32)

    ra = rot_ref[:, 0:LANES]
    rb = rot_ref[:, LANES:2 * LANES]
    rc = rot_ref[:, 2 * LANES:3 * LANES]

    def rot(t):
        return t * ra + pltpu.roll(t, LANES - ROT_DIM // 2, 1) * rb + pltpu.roll(t, ROT_DIM // 2, 1) * rc

    nc = 512
    row = lax.broadcasted_iota(jnp.int32, (tm, 1), 0)
    cw = cw_ref[...]
    for j in range(2):
        cs = slice(j * nc, (j + 1) * nc)
        u = mm(W_CB + j * nc, nc) * mm(W_CX + j * nc, nc)
        prev = carry_s[:, cs]
        h1 = jnp.where(first, 0.0, prev[7:8])
        h2 = jnp.where(first, 0.0, prev[6:7])
        s1 = jnp.where(row == 0, h1, pltpu.roll(u, 1, 0))
        s2 = jnp.where(row == 0, h2, jnp.where(row == 1, h1, pltpu.roll(u, 2, 0)))
        y = cw[2:3, cs] * u + cw[1:2, cs] * s1 + cw[0:1, cs] * s2
        carry_s[:, cs] = u[tm - 8:tm]
        o_ref[:, P_CI + j * nc:P_CI + (j + 1) * nc] = (mm(W_CC + j * nc, nc) * y).astype(BF16)
    for j in range(2):
        q = mm(W_Q + j * nc, nc)
        for p in range(nc // LANES):
            t = rot(q[:, p * LANES:(p + 1) * LANES]) * (HEAD_DIM ** -0.5)
            o_ref[:, P_Q + j * nc + p * LANES:P_Q + j * nc + (p + 1) * LANES] = t.astype(BF16)
    kv = mm(W_K, 2 * LANES)
    o_ref[:, P_KV:P_KV + LANES] = rot(kv[:, :LANES]).astype(BF16)
    o_ref[:, P_KV + LANES:P_KV + 2 * LANES] = kv[:, LANES:].astype(BF16)
    for j in range(4):
        gt = mm(W_G + j * nc, nc) + bg_ref[:, j * nc:(j + 1) * nc]
        o_ref[:, P_G + j * nc:P_G + (j + 1) * nc] = jax.nn.sigmoid(gt).astype(BF16)


def _proj_specs(w_in, conv_w, tm):
    fixed = lambda i: (0, 0)
    return [
        pl.BlockSpec((1, D_MODEL), fixed),
        pl.BlockSpec(w_in.shape, fixed, pipeline_mode=pl.Buffered(1)),
        pl.BlockSpec((1, 2 * D_MODEL), fixed),
        pl.BlockSpec((tm, 3 * LANES), lambda i: (i, 0)),
        pl.BlockSpec(conv_w.shape, fixed),
    ]


def _proj(x2, g1, w_in, b_gate, rot, conv_w, seq, tm):
    t = x2.shape[0]
    row = lambda i: (i, 0)
    return pl.pallas_call(
        functools.partial(_proj_kernel, seq // tm),
        grid=(t // tm,),
        in_specs=[pl.BlockSpec((tm, D_MODEL), row)] + _proj_specs(w_in, conv_w, tm),
        out_specs=pl.BlockSpec((tm, P_COLS), row),
        out_shape=jax.ShapeDtypeStruct((t, P_COLS), BF16),
        scratch_shapes=[pltpu.VMEM((8, D_MODEL), F32)],
        compiler_params=pltpu.CompilerParams(dimension_semantics=("arbitrary",), vmem_limit_bytes=VMEM_BIG),
        name="proj",
    )(x2, g1, w_in, b_gate, rot, conv_w)


R_E0, R_E1, R_RANK0, R_RANK1, R_W0, R_W1 = range(6)
R_ROW0 = N_GROUPS


def _route_tile(tm, h, g_ref, wt_ref, bt_ref, r_ref, cnt_ref, base_s):
    hn = _rms(h, g_ref[...])
    hi = hn.astype(BF16)
    lo = (hn - hi.astype(F32)).astype(BF16)
    both = lax.dot_general(wt_ref[...], hi, NT, preferred_element_type=F32)
    lt = (both[:LANES] + both[LANES:] + lax.dot_general(wt_ref[0:LANES, :], lo, NT, preferred_element_type=F32)
          + bt_ref[:, 0:1])
    row = lax.broadcasted_iota(jnp.int32, (LANES, tm), 0)
    neg = -jnp.inf

    def first_max(v):
        m = jnp.max(v, axis=0, keepdims=True)
        return m, jnp.min(jnp.where(v == m, row, LANES), axis=0, keepdims=True)

    gl = jnp.where(row < N_GROUPS, lt, neg)
    gmax, gtop = first_max(gl)
    pg = 1.0 / jnp.sum(jnp.exp(gl - gmax), axis=0, keepdims=True)
    in_group = (row >= R_ROW0) & (row < R_ROW0 + N_EXPERTS) & (((row - R_ROW0) >> 3) == gtop)
    el = jnp.where(in_group, lt, neg)
    m1, i1 = first_max(el)
    m2, i2 = first_max(jnp.where(row == i1, neg, el))
    e2 = jnp.exp(m2 - m1)
    den = 1.0 + e2
    w0 = pg * (1.0 / den)
    w1 = pg * (e2 / den)
    hit0 = row == i1
    hit1 = row == i2
    a = jnp.where(hit0 | hit1, 1.0, 0.0)
    rr = lax.broadcasted_iota(jnp.int32, (tm, tm), 0)
    cc = lax.broadcasted_iota(jnp.int32, (tm, tm), 1)
    tri = jnp.where(rr < cc, 1.0, 0.0).astype(BF16)
    before = jnp.dot(a.astype(BF16), tri, preferred_element_type=F32) + base_s[:, 0:1]
    rank0 = jnp.sum(jnp.where(hit0, before, 0.0), axis=0, keepdims=True)
    rank1 = jnp.sum(jnp.where(hit1, before, 0.0), axis=0, keepdims=True)
    base_s[...] = base_s[...] + jnp.sum(a, axis=1, keepdims=True)
    cnt_ref[...] = base_s[...]
    vals = ((i1 - R_ROW0).astype(F32), (i2 - R_ROW0).astype(F32), rank0, rank1, w0, w1)
    slab = jnp.zeros((LANES, tm), F32)
    for k, v in enumerate(vals):
        slab = jnp.where(row == k, v, slab)
    r_ref[...] = slab.T
    return hn


def _mix_kernel(tm, tiles_per_seq, p_ref, hkv_ref, x_ref, sink_ref, wco_ref, wao_ref, wo_ref,
                g2_ref, wt_ref, bt_ref, o_ref, r_ref, cnt_ref, hp_ref, ke_s, ko_s, vta_s, vtb_s, attn_s, base_s):
    i = pl.program_id(0)
    first = (i % tiles_per_seq) == 0
    lo = lax.broadcasted_iota(jnp.int32, (1, LANES), 1) < HEAD_DIM

    @pl.when(i == 0)
    def _():
        base_s[...] = jnp.zeros_like(base_s)

    kv_all = jnp.concatenate([hkv_ref[...], p_ref[:, P_KV:P_KV + 2 * LANES]], axis=0)
    k = kv_all[:, :LANES].astype(F32)
    kr = pltpu.roll(k, HEAD_DIM, 1)
    ke_s[0] = jnp.where(lo, k, 0.0).astype(BF16)
    ko_s[0] = jnp.where(lo, 0.0, kr).astype(BF16)
    ke_s[1] = jnp.where(lo, kr, 0.0).astype(BF16)
    ko_s[1] = jnp.where(lo, 0.0, k).astype(BF16)
    v = kv_all[:, LANES:].astype(F32)
    n_keys = HALO + tm
    for j in range(n_keys // LANES):
        vta_s[:, j * LANES:(j + 1) * LANES] = v[j * LANES:(j + 1) * LANES].T.astype(BF16)
    for j in range(n_keys // LANES - 1):
        vtb_s[:, j * LANES:(j + 1) * LANES] = v[CHUNK + j * LANES:CHUNK + (j + 1) * LANES].T.astype(BF16)
    vtb_s[:, n_keys - LANES:n_keys - CHUNK] = v[n_keys - CHUNK:n_keys].T.astype(BF16)

    krow = lax.broadcasted_iota(jnp.int32, (2 * KEYS, 1), 0)
    krow = jnp.where(krow >= KEYS, krow - KEYS, krow)

    for c in range(tm // CHUNK):
        r0 = c * CHUNK
        vt_s, v0 = (vta_s, r0) if c % 2 == 0 else (vtb_s, r0 - CHUNK)
        for g in range(N_KV_HEADS):
            qbase = P_Q + g * 4 * LANES
            q = jnp.concatenate(
                [p_ref[r0:r0 + CHUNK, qbase + p * LANES:qbase + (p + 1) * LANES] for p in range(4)], axis=0)
            kk = jnp.concatenate([ke_s[g, r0:r0 + KEYS, :], ko_s[g, r0:r0 + KEYS, :]], axis=0)
            st = lax.dot_general(kk, q, NT, preferred_element_type=F32)
            if r0 < HALO:
                nbad = jnp.where(first, HALO - r0, 0)
                st = jnp.where(krow < nbad, -jnp.inf, st)
            vt = vt_s[g * HEAD_DIM:(g + 1) * HEAD_DIM, v0:v0 + KEYS]
            outs = []
            for par in range(2):
                s = st[par * KEYS:(par + 1) * KEYS]
                sc = sink_ref[2 * g + par:2 * g + par + 1, :]
                m = jnp.maximum(jnp.max(s, axis=0, keepdims=True), sc)
                p = jnp.exp(s - m)
                l = jnp.sum(p, axis=0, keepdims=True) + jnp.exp(sc - m)
                outs.append(jnp.dot(vt, p.astype(BF16), preferred_element_type=F32) / l)
            ot = jnp.concatenate(outs, axis=0)
            for p2 in range(2):
                blk = ot[:, p2 * LANES:(p2 + 1) * LANES].T
                for h in range(2):
                    c0 = (g * 4 + 2 * p2 + h) * LANES
                    attn_s[r0:r0 + CHUNK, c0:c0 + LANES] = blk[h * CHUNK:(h + 1) * CHUNK].astype(BF16)

    conv_out = jnp.dot(p_ref[:, P_CI:P_CI + D_MODEL], wco_ref[...], preferred_element_type=F32)
    attn_out = jnp.dot(attn_s[...], wao_ref[...], preferred_element_type=F32)
    gc = p_ref[:, P_G:P_G + D_MODEL].astype(F32)
    ga = p_ref[:, P_G + D_MODEL:P_G + 2 * D_MODEL].astype(F32)
    merged = (gc * conv_out + ga * attn_out).astype(BF16)
    h_new = x_ref[...] + jnp.dot(merged, wo_ref[...], preferred_element_type=F32)
    o_ref[...] = h_new
    hn = _route_tile(tm, h_new, g2_ref, wt_ref, bt_ref, r_ref, cnt_ref, base_s)
    _store_rows(hp_ref, (), tm, hn)


def _mix(p, x2, sink_tab, w_co, w_ao, w_o, g2, wt_router, bt_router, seq, tm):
    t = x2.shape[0]
    row = lambda i: (i, 0)
    fixed = lambda i: (0, 0)
    sq = (D_MODEL, D_MODEL)
    n_keys = HALO + tm
    return pl.pallas_call(
        functools.partial(_mix_kernel, tm, seq // tm),
        grid=(t // tm,),
        in_specs=[
            pl.BlockSpec((tm, P_COLS), row),
            pl.BlockSpec((HALO, 2 * LANES), lambda i: (jnp.maximum(i * (tm // HALO) - 1, 0), P_KV // (2 * LANES))),
            pl.BlockSpec((tm, D_MODEL), row),
            pl.BlockSpec(sink_tab.shape, fixed),
            pl.BlockSpec(sq, fixed),
            pl.BlockSpec(sq, fixed),
            pl.BlockSpec(sq, fixed),
            pl.BlockSpec((1, D_MODEL), fixed),
            pl.BlockSpec((2 * LANES, D_MODEL), fixed),
            pl.BlockSpec((LANES, LANES), fixed),
        ],
        out_specs=[
            pl.BlockSpec((tm, D_MODEL), row),
            pl.BlockSpec((tm, LANES), row),
            pl.BlockSpec((LANES, LANES), fixed),
            pl.BlockSpec((tm * ROW_WORDS, LANES), row),
        ],
        out_shape=[
            jax.ShapeDtypeStruct((t, D_MODEL), F32),
            jax.ShapeDtypeStruct((t, LANES), F32),
            jax.ShapeDtypeStruct((LANES, LANES), F32),
            jax.ShapeDtypeStruct((t * ROW_WORDS, LANES), U32),
        ],
        scratch_shapes=[
            pltpu.VMEM((N_KV_HEADS, n_keys, LANES), BF16),
            pltpu.VMEM((N_KV_HEADS, n_keys, LANES), BF16),
            pltpu.VMEM((2 * HEAD_DIM, n_keys), BF16),
            pltpu.VMEM((2 * HEAD_DIM, n_keys), BF16),
            pltpu.VMEM((tm, D_MODEL), BF16),
            pltpu.VMEM((LANES, LANES), F32),
        ],
        compiler_params=pltpu.CompilerParams(dimension_semantics=("arbitrary",), vmem_limit_bytes=VMEM_BIG),
        name="mix",
    )(p, p, x2, sink_tab, w_co, w_ao, w_o, g2, wt_router, bt_router)


def _rows_copy(src_ref, dst_ref, sem, n):
    return pltpu.make_async_copy(src_ref.at[pl.ds(0, n)], dst_ref.at[pl.ds(0, n)], sem)


def _dispatch_kernel(tm, n_tiles, n_blocks, zm_ref, d_ref, hp_ref, xs_ref, hn_s, zbuf, zsem, sem):
    i = pl.program_id(0)
    blk_rows = MOE_BLOCK * ROW_WORDS

    @pl.when(i == 0)
    def _():
        zbuf[...] = jnp.zeros_like(zbuf)

        def zero_copy(b):
            r = pl.multiple_of(b * blk_rows, blk_rows)
            return pltpu.make_async_copy(zbuf, xs_ref.at[pl.ds(r, blk_rows)], zsem)

        def start(b, c):
            @pl.when(zm_ref[b] == 1)
            def _():
                zero_copy(b).start()
            return c

        def wait(b, c):
            @pl.when(zm_ref[b] == 1)
            def _():
                zero_copy(b).wait()
            return c

        lax.fori_loop(0, n_blocks, start, 0)
        lax.fori_loop(0, n_blocks, wait, 0)

    slot = i % 2
    hn_s[slot] = hp_ref[...]

    def body(t, c):
        src = hn_s.at[slot, pl.ds(pl.multiple_of(t * ROW_WORDS, ROW_WORDS), ROW_WORDS)]
        for k in range(2):
            d = pl.multiple_of(d_ref[2 * t + k] * ROW_WORDS, ROW_WORDS)
            pltpu.make_async_copy(src, xs_ref.at[pl.ds(d, ROW_WORDS)], sem.at[slot]).start(priority=k)
        return c

    lax.fori_loop(0, tm, body, 0, unroll=8)

    def drain(s):
        for _ in range(2):
            _rows_copy(hn_s.at[s], xs_ref, sem.at[s], tm * ROW_WORDS).wait()

    @pl.when(i > 0)
    def _():
        drain(1 - slot)

    @pl.when(i == n_tiles - 1)
    def _():
        drain(slot)


def _dispatch(hp, dest, zero_mask, n_rows, tm):
    n_tiles = hp.shape[0] // (tm * ROW_WORDS)
    n_blocks = n_rows // MOE_BLOCK
    return pl.pallas_call(
        functools.partial(_dispatch_kernel, tm, n_tiles, n_blocks),
        grid_spec=pltpu.PrefetchScalarGridSpec(
            num_scalar_prefetch=1,
            grid=(n_tiles,),
            in_specs=[
                pl.BlockSpec((2 * tm,), lambda i, zm: (i,), memory_space=pltpu.SMEM),
                pl.BlockSpec((tm * ROW_WORDS, LANES), lambda i, zm: (i, 0)),
            ],
            out_specs=pl.BlockSpec(memory_space=pl.ANY),
            scratch_shapes=[
                pltpu.VMEM((2, tm * ROW_WORDS, LANES), U32),
                pltpu.VMEM((MOE_BLOCK * ROW_WORDS, LANES), U32),
                pltpu.SemaphoreType.DMA(()),
                pltpu.SemaphoreType.DMA((2,)),
            ],
        ),
        out_shape=jax.ShapeDtypeStruct((n_rows * ROW_WORDS, LANES), U32),
        compiler_params=pltpu.CompilerParams(dimension_semantics=("arbitrary",)),
        name="dispatch",
    )(zero_mask, dest, hp)


def _expert_kernel(layer, be_ref, nw_ref, nx_ref, sl_ref, nu_ref, x_ref, w1_hbm, w3_hbm, w2_hbm, o_ref,
                   w1_f, w3_f, w2_f, w1_s, w3_s, w2_s, sem):
    b = pl.program_id(0)

    def fetch(e, slot):
        return (pltpu.make_async_copy(w1_hbm.at[layer, e], w1_f.at[slot], sem.at[slot]),
                pltpu.make_async_copy(w3_hbm.at[layer, e], w3_f.at[slot], sem.at[slot]),
                pltpu.make_async_copy(w2_hbm.at[layer, e], w2_f.at[slot], sem.at[slot]))

    @pl.when(b < nu_ref[0])
    def _():
        @pl.when(nw_ref[b] == 1)
        def _():
            slot = sl_ref[b]

            @pl.when(b == 0)
            def _():
                for c in fetch(be_ref[0], 0):
                    c.start()

            for c in fetch(be_ref[b], slot):
                c.wait()
            w1_s[...] = w1_f[slot].astype(BF16)
            w3_s[...] = w3_f[slot].astype(BF16)
            w2_s[...] = w2_f[slot].astype(BF16)

            @pl.when(nx_ref[b] >= 0)
            def _():
                for c in fetch(nx_ref[b], 1 - slot):
                    c.start()

        x = _load_rows(x_ref, (), 0, MOE_BLOCK).astype(BF16)
        h1 = jnp.dot(x, w1_s[...], preferred_element_type=F32)
        h3 = jnp.dot(x, w3_s[...], preferred_element_type=F32)
        hid = (h1 * jax.nn.sigmoid(h1) * h3).astype(BF16)
        y = jnp.dot(hid, w2_s[...], preferred_element_type=F32)
        _store_rows(o_ref, (), MOE_BLOCK, y)

    @pl.when(b >= nu_ref[0])
    def _():
        o_ref[...] = jnp.zeros_like(o_ref)


def _experts(xs, plan, w1, w3, w2, layer):
    n_rows = xs.shape[0] // ROW_WORDS
    blk = (MOE_BLOCK * ROW_WORDS, LANES)
    any_spec = pl.BlockSpec(memory_space=pl.ANY)
    return pl.pallas_call(
        functools.partial(_expert_kernel, layer),
        grid_spec=pltpu.PrefetchScalarGridSpec(
            num_scalar_prefetch=5,
            grid=(n_rows // MOE_BLOCK,),
            in_specs=[
                pl.BlockSpec(blk, lambda b, be, nw, nx, sl, nu: (jnp.minimum(b, nu[0] - 1), 0)),
                any_spec, any_spec, any_spec,
            ],
            out_specs=pl.BlockSpec(blk, lambda b, be, nw, nx, sl, nu: (b, 0)),
            scratch_shapes=[
                pltpu.VMEM((2, D_MODEL, EXPERT_FF), F32),
                pltpu.VMEM((2, D_MODEL, EXPERT_FF), F32),
                pltpu.VMEM((2, EXPERT_FF, D_MODEL), F32),
                pltpu.VMEM((D_MODEL, EXPERT_FF), BF16),
                pltpu.VMEM((D_MODEL, EXPERT_FF), BF16),
                pltpu.VMEM((EXPERT_FF, D_MODEL), BF16),
                pltpu.SemaphoreType.DMA((2,)),
            ],
        ),
        out_shape=jax.ShapeDtypeStruct(xs.shape, U32),
        compiler_params=pltpu.CompilerParams(dimension_semantics=("arbitrary",), vmem_limit_bytes=VMEM_BIG),
        name="experts",
    )(*plan, xs, w1, w3, w2)


def _issue_row_gathers(tm, d_ref, ys_ref, ybuf, sem, slot, unrolled):
    def one(t):
        for k in range(2):
            d = pl.multiple_of(d_ref[2 * t + k] * ROW_WORDS, ROW_WORDS)
            r = (k * tm + t) * ROW_WORDS
            if not isinstance(r, int):
                r = pl.multiple_of(r, ROW_WORDS)
            pltpu.make_async_copy(
                ys_ref.at[pl.ds(d, ROW_WORDS)], ybuf.at[slot, pl.ds(r, ROW_WORDS)], sem.at[slot]).start(priority=k)

    if unrolled:
        for t in range(tm):
            one(t)
    else:
        def body(t, c):
            one(t)
            return c

        lax.fori_loop(0, tm, body, 0, unroll=8)


def _combined_tile(tm, r_ref, h_ref, ys_ref, ybuf, sem, slot):
    _rows_copy(ys_ref, ybuf.at[slot], sem.at[slot], 2 * tm * ROW_WORDS).wait()
    r = r_ref[...]
    w0 = r[:, R_W0:R_W0 + 1]
    w1 = r[:, R_W1:R_W1 + 1]
    return h_ref[...] + (w0 * _load_rows(ybuf, (slot,), 0, tm) + w1 * _load_rows(ybuf, (slot,), tm, tm))


def _combine_kernel(tm, n_tiles, dcur_ref, dnext_ref, r_ref, h_ref, fg_ref, ys_ref, o_ref, ybuf, sem):
    i = pl.program_id(0)

    @pl.when(i == 0)
    def _():
        _issue_row_gathers(tm, dcur_ref, ys_ref, ybuf, sem, 0, False)

    @pl.when(i + 1 < n_tiles)
    def _():
        _issue_row_gathers(tm, dnext_ref, ys_ref, ybuf, sem, (i + 1) % 2, False)

    o_ref[...] = _rms(_combined_tile(tm, r_ref, h_ref, ys_ref, ybuf, sem, i % 2), fg_ref[...])


def _combine_specs(tm, n_tiles):
    row = lambda i: (i, 0)
    return [
        pl.BlockSpec((2 * tm,), lambda i: (i,), memory_space=pltpu.SMEM),
        pl.BlockSpec((2 * tm,), lambda i: (jnp.minimum(i + 1, n_tiles - 1),), memory_space=pltpu.SMEM),
        pl.BlockSpec((tm, LANES), row),
        pl.BlockSpec((tm, D_MODEL), row),
    ]


def _combine_scratch(tm):
    return [pltpu.VMEM((2, 2 * tm * ROW_WORDS, LANES), U32), pltpu.SemaphoreType.DMA((2,))]


def _combine_final(ys, dest, route, h2, final_g, tm):
    t = h2.shape[0]
    n_tiles = t // tm
    return pl.pallas_call(
        functools.partial(_combine_kernel, tm, n_tiles),
        grid=(n_tiles,),
        in_specs=_combine_specs(tm, n_tiles)
        + [pl.BlockSpec((1, D_MODEL), lambda i: (0, 0)), pl.BlockSpec(memory_space=pl.ANY)],
        out_specs=pl.BlockSpec((tm, D_MODEL), lambda i: (i, 0)),
        out_shape=jax.ShapeDtypeStruct((t, D_MODEL), F32),
        scratch_shapes=_combine_scratch(tm),
        compiler_params=pltpu.CompilerParams(dimension_semantics=("arbitrary",)),
        name="combine",
    )(dest, dest, route, h2, final_g, ys)


def _combine_proj_kernel(tm, n_tiles, tiles_per_seq, dcur_ref, dnext_ref, r_ref, hm_ref, g_ref, w_ref, bg_ref,
                         rot_ref, cw_ref, ys_ref, o_ref, h_ref, ybuf, sem, carry_s):
    i = pl.program_id(0)
    slot = i % 2
    first = (i % tiles_per_seq) == 0

    @pl.when(i == 0)
    def _():
        carry_s[...] = jnp.zeros_like(carry_s)
        _issue_row_gathers(tm, dcur_ref, ys_ref, ybuf, sem, 0, False)

    h = _combined_tile(tm, r_ref, hm_ref, ys_ref, ybuf, sem, slot)
    h_ref[...] = h
    _issue_row_gathers(tm, dnext_ref, ys_ref, ybuf, sem, 1 - slot, True)
    _proj_body(_rms(h, g_ref[...]).astype(BF16), first, w_ref, bg_ref, rot_ref, cw_ref, o_ref, carry_s)

    @pl.when(i == n_tiles - 1)
    def _():
        _rows_copy(ys_ref, ybuf.at[1 - slot], sem.at[1 - slot], 2 * tm * ROW_WORDS).wait()


def _combine_proj(ys, dest, route, h2, g1, w_in, b_gate, rot, conv_w, seq, tm):
    t = h2.shape[0]
    n_tiles = t // tm
    row = lambda i: (i, 0)
    return pl.pallas_call(
        functools.partial(_combine_proj_kernel, tm, n_tiles, seq // tm),
        grid=(n_tiles,),
        in_specs=_combine_specs(tm, n_tiles) + _proj_specs(w_in, conv_w, tm) + [pl.BlockSpec(memory_space=pl.ANY)],
        out_specs=[pl.BlockSpec((tm, P_COLS), row), pl.BlockSpec((tm, D_MODEL), row)],
        out_shape=[jax.ShapeDtypeStruct((t, P_COLS), BF16), jax.ShapeDtypeStruct((t, D_MODEL), F32)],
        scratch_shapes=_combine_scratch(tm) + [pltpu.VMEM((8, D_MODEL), F32)],
        compiler_params=pltpu.CompilerParams(dimension_semantics=("arbitrary",), vmem_limit_bytes=VMEM_BIG),
        name="combine_proj",
    )(dest, dest, route, h2, g1, w_in, b_gate, rot, conv_w, ys)


def _rotary_table(positions):
    half = ROT_DIM // 2
    inv = ROPE_THETA ** (-jnp.arange(0, ROT_DIM, 2, dtype=F32) / ROT_DIM)
    ang = positions.reshape(-1).astype(F32)[:, None] * inv
    cs1 = jnp.concatenate([jnp.cos(ang), jnp.sin(ang), jnp.ones((ang.shape[0], 1), F32)], axis=1)
    lane = jnp.arange(LANES) % HEAD_DIM
    j = jnp.arange(2 * half + 1)[:, None]
    a = jnp.where(lane < half, j == lane, jnp.where(lane < ROT_DIM, j == lane - half, j == 2 * half))
    b = -((lane < half) & (j == half + lane)).astype(F32)
    c = ((lane >= half) & (lane < ROT_DIM) & (j == lane)).astype(F32)
    place = jnp.concatenate([a.astype(F32), b, c], axis=1)
    return jnp.dot(cs1, place, precision=lax.Precision.HIGHEST)


def _sink_table(sinks):
    s = sinks.astype(F32).reshape(N_KV_HEADS, 4, 2).transpose(0, 2, 1)
    s = jnp.broadcast_to(s[..., None], (N_KV_HEADS, 2, 4, CHUNK)).reshape(2 * N_KV_HEADS, 4 * CHUNK)
    return jnp.concatenate([s, jnp.zeros((8 - 2 * N_KV_HEADS, 4 * CHUNK), F32)], axis=0)


def _router_params(w_group, b_group, w_route, b_route):
    d = w_group.shape[0]
    pad = LANES - N_GROUPS - N_EXPERTS
    wt = jnp.concatenate([w_group.T, w_route.transpose(0, 2, 1).reshape(N_EXPERTS, d), jnp.zeros((pad, d), F32)], axis=0)
    hi = wt.astype(BF16)
    lo = (wt - hi.astype(F32)).astype(BF16)
    bias = jnp.concatenate([b_group, b_route.reshape(-1), jnp.zeros((pad,), F32)])
    return jnp.concatenate([hi, lo], axis=0), jnp.broadcast_to(bias[:, None], (LANES, LANES))


def _moe_plan(route, counts, n_tok):
    n_rows = -(-(2 * n_tok) // MOE_BLOCK) * MOE_BLOCK + N_EXPERTS * MOE_BLOCK
    n_blocks = n_rows // MOE_BLOCK
    ids = jnp.arange(N_EXPERTS, dtype=jnp.int32)
    cnt = counts[R_ROW0:R_ROW0 + N_EXPERTS, 0].astype(jnp.int32)
    padded = (cnt + MOE_BLOCK - 1) // MOE_BLOCK * MOE_BLOCK
    pad_end = jnp.sum(jnp.where(ids[None, :] <= ids[:, None], padded[None, :], 0), axis=1)
    pad_start = pad_end - padded
    e = route[:, R_E0:R_E1 + 1].astype(jnp.int32)
    rank = route[:, R_RANK0:R_RANK1 + 1].astype(jnp.int32)
    dest = (jnp.sum(jnp.where(e[..., None] == ids, pad_start, 0), axis=-1) + rank).reshape(-1)
    n_used = pad_end[-1] // MOE_BLOCK
    blk = jnp.arange(n_blocks, dtype=jnp.int32)
    be = jnp.minimum(jnp.sum((pad_end[None, :] <= blk[:, None] * MOE_BLOCK).astype(jnp.int32), axis=1), N_EXPERTS - 1)
    be_last = jnp.sum(jnp.where(blk == n_used - 1, be, 0))
    be = jnp.where(blk < n_used, be, be_last)
    new_expert = jnp.concatenate([jnp.ones((1,), jnp.int32), (be[1:] != be[:-1]).astype(jnp.int32)])
    later = blk[None, :] > blk[:, None]
    seq_slot = (jnp.sum(jnp.where(later.T | (blk[None, :] == blk[:, None]), new_expert[None, :], 0), axis=1) - 1) % 2
    next_start = jnp.min(jnp.where(later & (new_expert[None, :] == 1), blk[None, :], n_blocks), axis=1)
    next_expert = jnp.sum(jnp.where(blk[None, :] == next_start[:, None], be[None, :], 0), axis=1)
    next_expert = jnp.where(next_start < n_blocks, next_expert, -1)
    last = pad_end // MOE_BLOCK - 1
    is_last = jnp.any((blk[:, None] == last[None, :]) & (padded[None, :] > 0), axis=1)
    zero_mask = (is_last | (blk >= n_used)).astype(jnp.int32)
    plan = (be, new_expert, next_expert.astype(jnp.int32), seq_slot.astype(jnp.int32), n_used.astype(jnp.int32).reshape(1))
    return n_rows, dest, plan, zero_mask


def kernel(x, positions, norm1_g, w_in, b_gate, conv_w, sinks, w_conv_out, w_attn_out, w_o, norm2_g, w_group, b_group, w_route, b_route, w1, w3, w2, final_g):
    b, s, d = x.shape
    t = b * s
    depth = w_in.shape[0]
    tm = 512
    rot = _rotary_table(positions)
    h = x.reshape(t, d)
    ys = dest = route = None
    for l in range(depth):
        w_in_l = w_in[l].astype(BF16)
        if l == 0:
            p = _proj(h, norm1_g[l][None], w_in_l, b_gate[l][None], rot, conv_w[l], s, tm)
        else:
            p, h = _combine_proj(ys, dest, route, h, norm1_g[l][None], w_in_l, b_gate[l][None], rot, conv_w[l], s, tm)
        g2 = norm2_g[l][None]
        wt_router, bt_router = _router_params(w_group[l], b_group[l], w_route[l], b_route[l])
        h, route, counts, hp = _mix(p, h, _sink_table(sinks[l]), w_conv_out[l].astype(BF16),
                                    w_attn_out[l].astype(BF16), w_o[l].astype(BF16), g2, wt_router, bt_router, s, tm)
        n_rows, dest, plan, zero_mask = _moe_plan(route, counts, t)
        xs = _dispatch(hp, dest, zero_mask, n_rows, tm)
        ys = _experts(xs, plan, w1, w3, w2, l)
    return _combine_final(ys, dest, route, h, final_g[None], tm).reshape(b, s, d)
```

```python
import functools

import jax
import jax.numpy as jnp
from jax import lax
from jax.experimental import pallas as pl
from jax.experimental.pallas import tpu as pltpu

F32 = jnp.float32
BF16 = jnp.bfloat16
U32 = jnp.uint32

D_MODEL = 1024
HALF = D_MODEL // 2
RMS_EPS = 1e-5
CHUNK = 64
WINDOW_CHUNKS = 2
HEAD_DIM = 64
N_Q_HEADS = 16
N_KV_HEADS = 2
ROT_DIM = 16
ROPE_THETA = 500000.0
N_GROUPS = 8
EXPERTS_PER_GROUP = 8
N_EXPERTS = 64
EXPERT_FF = 512
MOE_BLOCK = 256
LANES = 128

W_CB, W_CC, W_CX, W_Q, W_K, W_G = 0, 1024, 2048, 3072, 4096, 4352
P_CI, P_Q, P_KV, P_G = 0, 1024, 2048, 2304
P_COLS = 4352
HALO = WINDOW_CHUNKS * CHUNK
KEYS = HALO + CHUNK
SCORES_AHEAD = 4

VMEM_BIG = 56 * 1024 * 1024
NT = (((1,), (1,)), ((), ()))


def _rms(x, g):
    ms = jnp.mean(x * x, axis=-1, keepdims=True)
    return x * lax.rsqrt(ms + RMS_EPS) * g


def _pack_pair(lo, hi):
    def rnd(x):
        return lax.bitcast_convert_type(x, U32) + U32(0x8000)

    return (rnd(hi) & U32(0xFFFF0000)) | (rnd(lo) >> 16)


def _unpack_pair(w):
    lo = lax.bitcast_convert_type(w << 16, F32)
    hi = lax.bitcast_convert_type(w & U32(0xFFFF0000), F32)
    return lo, hi


ROW_WORDS = HALF // LANES


def _store_rows(ref, lead, n, x):
    for j in range(ROW_WORDS):
        w = _pack_pair(x[:, j * LANES:(j + 1) * LANES], x[:, HALF + j * LANES:HALF + (j + 1) * LANES])
        ref[lead + (pl.ds(j, n, stride=ROW_WORDS), slice(None))] = w


def _load_rows(ref, lead, row0, n):
    parts = [_unpack_pair(ref[lead + (pl.ds(row0 * ROW_WORDS + j, n, stride=ROW_WORDS), slice(None))])
             for j in range(ROW_WORDS)]
    return jnp.concatenate([p[0] for p in parts] + [p[1] for p in parts], axis=1)


def _proj_kernel(tiles_per_seq, x_ref, g_ref, w_ref, bg_ref, rot_ref, cw_ref, o_ref, carry_s):
    first = (pl.program_id(0) % tiles_per_seq) == 0

    @pl.when(pl.program_id(0) == 0)
    def _():
        carry_s[...] = jnp.zeros_like(carry_s)

    _proj_body(_rms(x_ref[...], g_ref[...]).astype(BF16), first, w_ref, bg_ref, rot_ref, cw_ref, o_ref, carry_s)


def _proj_body(xn, first, w_ref, bg_ref, rot_ref, cw_ref, o_ref, carry_s):
    tm = xn.shape[0]

    def mm(c0, n):
        return jnp.dot(xn, w_ref[:, c0:c0 + n], preferred_element_type=F32)

    ra = rot_ref[:, 0:LANES]
    rb = rot_ref[:, LANES:2 * LANES]
    rc = rot_ref[:, 2 * LANES:3 * LANES]

    def rot(t):
        return t * ra + pltpu.roll(t, LANES - ROT_DIM // 2, 1) * rb + pltpu.roll(t, ROT_DIM // 2, 1) * rc

    nc = 512
    row = lax.broadcasted_iota(jnp.int32, (tm, 1), 0)
    cw = cw_ref[...]
    for j in range(2):
        cs = slice(j * nc, (j + 1) * nc)
        u = mm(W_CB + j * nc, nc) * mm(W_CX + j * nc, nc)
        prev = carry_s[:, cs]
        h1 = jnp.where(first, 0.0, prev[7:8])
        h2 = jnp.where(first, 0.0, prev[6:7])
        s1 = jnp.where(row == 0, h1, pltpu.roll(u, 1, 0))
        s2 = jnp.where(row == 0, h2, jnp.where(row == 1, h1, pltpu.roll(u, 2, 0)))
        y = cw[2:3, cs] * u + cw[1:2, cs] * s1 + cw[0:1, cs] * s2
        carry_s[:, cs] = u[tm - 8:tm]
        o_ref[:, P_CI + j * nc:P_CI + (j + 1) * nc] = (mm(W_CC + j * nc, nc) * y).astype(BF16)
    for j in range(2):
        q = mm(W_Q + j * nc, nc)
        for p in range(nc // LANES):
            t = rot(q[:, p * LANES:(p + 1) * LANES]) * (HEAD_DIM ** -0.5)
            o_ref[:, P_Q + j * nc + p * LANES:P_Q + j * nc + (p + 1) * LANES] = t.astype(BF16)
    kv = mm(W_K, 2 * LANES)
    o_ref[:, P_KV:P_KV + LANES] = rot(kv[:, :LANES]).astype(BF16)
    o_ref[:, P_KV + LANES:P_KV + 2 * LANES] = kv[:, LANES:].astype(BF16)
    for j in range(4):
        gt = mm(W_G + j * nc, nc) + bg_ref[:, j * nc:(j + 1) * nc]
        o_ref[:, P_G + j * nc:P_G + (j + 1) * nc] = jax.nn.sigmoid(gt).astype(BF16)


def _proj_specs(w_in, conv_w, tm):
    fixed = lambda i: (0, 0)
    return [
        pl.BlockSpec((1, D_MODEL), fixed),
        pl.BlockSpec(w_in.shape, fixed, pipeline_mode=pl.Buffered(1)),
        pl.BlockSpec((1, 2 * D_MODEL), fixed),
        pl.BlockSpec((tm, 3 * LANES), lambda i: (i, 0)),
        pl.BlockSpec(conv_w.shape, fixed),
    ]


def _proj(x2, g1, w_in, b_gate, rot, conv_w, seq, tm):
    t = x2.shape[0]
    row = lambda i: (i, 0)
    return pl.pallas_call(
        functools.partial(_proj_kernel, seq // tm),
        grid=(t // tm,),
        in_specs=[pl.BlockSpec((tm, D_MODEL), row)] + _proj_specs(w_in, conv_w, tm),
        out_specs=pl.BlockSpec((tm, P_COLS), row),
        out_shape=jax.ShapeDtypeStruct((t, P_COLS), BF16),
        scratch_shapes=[pltpu.VMEM((8, D_MODEL), F32)],
        compiler_params=pltpu.CompilerParams(dimension_semantics=("arbitrary",), vmem_limit_bytes=VMEM_BIG),
        name="proj",
    )(x2, g1, w_in, b_gate, rot, conv_w)


R_E0, R_E1, R_RANK0, R_RANK1, R_W0, R_W1 = range(6)
R_ROW0 = N_GROUPS


def _route_tile(tm, h, g_ref, wt_ref, bt_ref, r_ref, cnt_ref, base_s):
    hn = _rms(h, g_ref[...])
    hi = hn.astype(BF16)
    lo = (hn - hi.astype(F32)).astype(BF16)
    both = lax.dot_general(wt_ref[...], hi, NT, preferred_element_type=F32)
    lt = (both[:LANES] + both[LANES:] + lax.dot_general(wt_ref[0:LANES, :], lo, NT, preferred_element_type=F32)
          + bt_ref[:, 0:1])
    row = lax.broadcasted_iota(jnp.int32, (LANES, tm), 0)
    neg = -jnp.inf

    def first_max(v):
        m = jnp.max(v, axis=0, keepdims=True)
        return m, jnp.min(jnp.where(v == m, row, LANES), axis=0, keepdims=True)

    gl = jnp.where(row < N_GROUPS, lt, neg)
    gmax, gtop = first_max(gl)
    pg = 1.0 / jnp.sum(jnp.exp(gl - gmax), axis=0, keepdims=True)
    in_group = (row >= R_ROW0) & (row < R_ROW0 + N_EXPERTS) & (((row - R_ROW0) >> 3) == gtop)
    el = jnp.where(in_group, lt, neg)
    m1, i1 = first_max(el)
    m2, i2 = first_max(jnp.where(row == i1, neg, el))
    e2 = jnp.exp(m2 - m1)
    den = 1.0 + e2
    w0 = pg * (1.0 / den)
    w1 = pg * (e2 / den)
    hit0 = row == i1
    hit1 = row == i2
    a = jnp.where(hit0 | hit1, 1.0, 0.0)
    rr = lax.broadcasted_iota(jnp.int32, (tm, tm), 0)
    cc = lax.broadcasted_iota(jnp.int32, (tm, tm), 1)
    tri = jnp.where(rr < cc, 1.0, 0.0).astype(BF16)
    before = jnp.dot(a.astype(BF16), tri, preferred_element_type=F32) + base_s[:, 0:1]
    rank0 = jnp.sum(jnp.where(hit0, before, 0.0), axis=0, keepdims=True)
    rank1 = jnp.sum(jnp.where(hit1, before, 0.0), axis=0, keepdims=True)
    base_s[...] = base_s[...] + jnp.sum(a, axis=1, keepdims=True)
    cnt_ref[...] = base_s[...]
    vals = ((i1 - R_ROW0).astype(F32), (i2 - R_ROW0).astype(F32), rank0, rank1, w0, w1)
    slab = jnp.zeros((LANES, tm), F32)
    for k, v in enumerate(vals):
        slab = jnp.where(row == k, v, slab)
    r_ref[...] = slab.T
    return hn


def _mix_kernel(tm, tiles_per_seq, p_ref, hkv_ref, x_ref, sink_ref, wco_ref, wao_ref, wo_ref,
                g2_ref, wt_ref, bt_ref, o_ref, r_ref, cnt_ref, hp_ref, ke_s, ko_s, vta_s, vtb_s, attn_s, base_s):
    i = pl.program_id(0)
    first = (i % tiles_per_seq) == 0
    lo = lax.broadcasted_iota(jnp.int32, (1, LANES), 1) < HEAD_DIM

    @pl.when(i == 0)
    def _():
        base_s[...] = jnp.zeros_like(base_s)

    kv_all = jnp.concatenate([hkv_ref[...], p_ref[:, P_KV:P_KV + 2 * LANES]], axis=0)
    k = kv_all[:, :LANES].astype(F32)
    kr = pltpu.roll(k, HEAD_DIM, 1)
    ke_s[0] = jnp.where(lo, k, 0.0).astype(BF16)
    ko_s[0] = jnp.where(lo, 0.0, kr).astype(BF16)
    ke_s[1] = jnp.where(lo, kr, 0.0).astype(BF16)
    ko_s[1] = jnp.where(lo, 0.0, k).astype(BF16)
    v = kv_all[:, LANES:].astype(F32)
    n_keys = HALO + tm
    for j in range(n_keys // LANES):
        vta_s[:, j * LANES:(j + 1) * LANES] = v[j * LANES:(j + 1) * LANES].T.astype(BF16)
    for j in range(n_keys // LANES - 1):
        vtb_s[:, j * LANES:(j + 1) * LANES] = v[CHUNK + j * LANES:CHUNK + (j + 1) * LANES].T.astype(BF16)
    vtb_s[:, n_keys - LANES:n_keys - CHUNK] = v[n_keys - CHUNK:n_keys].T.astype(BF16)

    krow = lax.broadcasted_iota(jnp.int32, (2 * KEYS, 1), 0)
    krow = jnp.where(krow >= KEYS, krow - KEYS, krow)

    def scores(c, g):
        r0 = c * CHUNK
        qbase = P_Q + g * 4 * LANES
        q = jnp.concatenate(
            [p_ref[r0:r0 + CHUNK, qbase + p * LANES:qbase + (p + 1) * LANES] for p in range(4)], axis=0)
        kk = jnp.concatenate([ke_s[g, r0:r0 + KEYS, :], ko_s[g, r0:r0 + KEYS, :]], axis=0)
        st = lax.dot_general(kk, q, NT, preferred_element_type=F32)
        if r0 < HALO:
            nbad = jnp.where(first, HALO - r0, 0)
            st = jnp.where(krow < nbad, -jnp.inf, st)
        return st

    def finish(c, g, st):
        r0 = c * CHUNK
        vt_s, v0 = (vta_s, r0) if c % 2 == 0 else (vtb_s, r0 - CHUNK)
        vt = vt_s[g * HEAD_DIM:(g + 1) * HEAD_DIM, v0:v0 + KEYS]
        outs = []
        for par in range(2):
            s = st[par * KEYS:(par + 1) * KEYS]
            sc = sink_ref[2 * g + par:2 * g + par + 1, :]
            m = jnp.maximum(jnp.max(s, axis=0, keepdims=True), sc)
            p = jnp.exp(s - m)
            l = jnp.sum(p, axis=0, keepdims=True) + jnp.exp(sc - m)
            outs.append(jnp.dot(vt, p.astype(BF16), preferred_element_type=F32) / l)
        ot = jnp.concatenate(outs, axis=0)
        for p2 in range(2):
            blk = ot[:, p2 * LANES:(p2 + 1) * LANES].T
            for h in range(2):
                c0 = (g * 4 + 2 * p2 + h) * LANES
                attn_s[r0:r0 + CHUNK, c0:c0 + LANES] = blk[h * CHUNK:(h + 1) * CHUNK].astype(BF16)

    units = [(c, g) for c in range(tm // CHUNK) for g in range(N_KV_HEADS)]
    pending = [scores(*u) for u in units[:SCORES_AHEAD]]
    for n, (c, g) in enumerate(units):
        st = pending.pop(0)
        if n + SCORES_AHEAD < len(units):
            pending.append(scores(*units[n + SCORES_AHEAD]))
        finish(c, g, st)

    conv_out = jnp.dot(p_ref[:, P_CI:P_CI + D_MODEL], wco_ref[...], preferred_element_type=F32)
    attn_out = jnp.dot(attn_s[...], wao_ref[...], preferred_element_type=F32)
    gc = p_ref[:, P_G:P_G + D_MODEL].astype(F32)
    ga = p_ref[:, P_G + D_MODEL:P_G + 2 * D_MODEL].astype(F32)
    merged = (gc * conv_out + ga * attn_out).astype(BF16)
    h_new = x_ref[...] + jnp.dot(merged, wo_ref[...], preferred_element_type=F32)
    o_ref[...] = h_new
    hn = _route_tile(tm, h_new, g2_ref, wt_ref, bt_ref, r_ref, cnt_ref, base_s)
    _store_rows(hp_ref, (), tm, hn)


def _mix(p, x2, sink_tab, w_co, w_ao, w_o, g2, wt_router, bt_router, seq, tm):
    t = x2.shape[0]
    row = lambda i: (i, 0)
    fixed = lambda i: (0, 0)
    sq = (D_MODEL, D_MODEL)
    n_keys = HALO + tm
    return pl.pallas_call(
        functools.partial(_mix_kernel, tm, seq // tm),
        grid=(t // tm,),
        in_specs=[
            pl.BlockSpec((tm, P_COLS), row),
            pl.BlockSpec((HALO, 2 * LANES), lambda i: (jnp.maximum(i * (tm // HALO) - 1, 0), P_KV // (2 * LANES))),
            pl.BlockSpec((tm, D_MODEL), row),
            pl.BlockSpec(sink_tab.shape, fixed),
            pl.BlockSpec(sq, fixed),
            pl.BlockSpec(sq, fixed),
            pl.BlockSpec(sq, fixed),
            pl.BlockSpec((1, D_MODEL), fixed),
            pl.BlockSpec((2 * LANES, D_MODEL), fixed),
            pl.BlockSpec((LANES, LANES), fixed),
        ],
        out_specs=[
            pl.BlockSpec((tm, D_MODEL), row),
            pl.BlockSpec((tm, LANES), row),
            pl.BlockSpec((LANES, LANES), fixed),
            pl.BlockSpec((tm * ROW_WORDS, LANES), row),
        ],
        out_shape=[
            jax.ShapeDtypeStruct((t, D_MODEL), F32),
            jax.ShapeDtypeStruct((t, LANES), F32),
            jax.ShapeDtypeStruct((LANES, LANES), F32),
            jax.ShapeDtypeStruct((t * ROW_WORDS, LANES), U32),
        ],
        scratch_shapes=[
            pltpu.VMEM((N_KV_HEADS, n_keys, LANES), BF16),
            pltpu.VMEM((N_KV_HEADS, n_keys, LANES), BF16),
            pltpu.VMEM((2 * HEAD_DIM, n_keys), BF16),
            pltpu.VMEM((2 * HEAD_DIM, n_keys), BF16),
            pltpu.VMEM((tm, D_MODEL), BF16),
            pltpu.VMEM((LANES, LANES), F32),
        ],
        compiler_params=pltpu.CompilerParams(dimension_semantics=("arbitrary",), vmem_limit_bytes=VMEM_BIG),
        name="mix",
    )(p, p, x2, sink_tab, w_co, w_ao, w_o, g2, wt_router, bt_router)


def _rows_copy(src_ref, dst_ref, sem, n):
    return pltpu.make_async_copy(src_ref.at[pl.ds(0, n)], dst_ref.at[pl.ds(0, n)], sem)


def _dispatch_kernel(tm, n_tiles, n_blocks, zm_ref, d_ref, hp_ref, xs_ref, hn_s, zbuf, zsem, sem):
    i = pl.program_id(0)
    blk_rows = MOE_BLOCK * ROW_WORDS

    @pl.when(i == 0)
    def _():
        zbuf[...] = jnp.zeros_like(zbuf)

        def zero_copy(b):
            r = pl.multiple_of(b * blk_rows, blk_rows)
            return pltpu.make_async_copy(zbuf, xs_ref.at[pl.ds(r, blk_rows)], zsem)

        def start(b, c):
            @pl.when(zm_ref[b] == 1)
            def _():
                zero_copy(b).start()
            return c

        def wait(b, c):
            @pl.when(zm_ref[b] == 1)
            def _():
                zero_copy(b).wait()
            return c

        lax.fori_loop(0, n_blocks, start, 0)
        lax.fori_loop(0, n_blocks, wait, 0)

    slot = i % 2
    hn_s[slot] = hp_ref[...]

    def body(t, c):
        src = hn_s.at[slot, pl.ds(pl.multiple_of(t * ROW_WORDS, ROW_WORDS), ROW_WORDS)]
        for k in range(2):
            d = pl.multiple_of(d_ref[2 * t + k] * ROW_WORDS, ROW_WORDS)
            pltpu.make_async_copy(src, xs_ref.at[pl.ds(d, ROW_WORDS)], sem.at[slot]).start(priority=k)
        return c

    lax.fori_loop(0, tm, body, 0, unroll=8)

    def drain(s):
        for _ in range(2):
            _rows_copy(hn_s.at[s], xs_ref, sem.at[s], tm * ROW_WORDS).wait()

    @pl.when(i > 0)
    def _():
        drain(1 - slot)

    @pl.when(i == n_tiles - 1)
    def _():
        drain(slot)


def _dispatch(hp, dest, zero_mask, n_rows, tm):
    n_tiles = hp.shape[0] // (tm * ROW_WORDS)
    n_blocks = n_rows // MOE_BLOCK
    return pl.pallas_call(
        functools.partial(_dispatch_kernel, tm, n_tiles, n_blocks),
        grid_spec=pltpu.PrefetchScalarGridSpec(
            num_scalar_prefetch=1,
            grid=(n_tiles,),
            in_specs=[
                pl.BlockSpec((2 * tm,), lambda i, zm: (i,), memory_space=pltpu.SMEM),
                pl.BlockSpec((tm * ROW_WORDS, LANES), lambda i, zm: (i, 0)),
            ],
            out_specs=pl.BlockSpec(memory_space=pl.ANY),
            scratch_shapes=[
                pltpu.VMEM((2, tm * ROW_WORDS, LANES), U32),
                pltpu.VMEM((MOE_BLOCK * ROW_WORDS, LANES), U32),
                pltpu.SemaphoreType.DMA(()),
                pltpu.SemaphoreType.DMA((2,)),
            ],
        ),
        out_shape=jax.ShapeDtypeStruct((n_rows * ROW_WORDS, LANES), U32),
        compiler_params=pltpu.CompilerParams(dimension_semantics=("arbitrary",)),
        name="dispatch",
    )(zero_mask, dest, hp)


def _expert_kernel(layer, be_ref, nw_ref, nx_ref, sl_ref, nu_ref, x_ref, w1_hbm, w3_hbm, w2_hbm, o_ref,
                   w1_f, w3_f, w2_f, w1_s, w3_s, w2_s, sem):
    b0 = 2 * pl.program_id(0)
    n_used = nu_ref[0]

    def fetch(e, slot):
        return (pltpu.make_async_copy(w1_hbm.at[layer, e], w1_f.at[slot], sem.at[slot]),
                pltpu.make_async_copy(w3_hbm.at[layer, e], w3_f.at[slot], sem.at[slot]),
                pltpu.make_async_copy(w2_hbm.at[layer, e], w2_f.at[slot], sem.at[slot]))

    def switch(b):
        @pl.when((b < n_used) & (nw_ref[b] == 1))
        def _():
            slot = sl_ref[b]

            @pl.when(b == 0)
            def _():
                for c in fetch(be_ref[0], 0):
                    c.start()

            for c in fetch(be_ref[b], slot):
                c.wait()
            w1_s[slot] = w1_f[slot].astype(BF16)
            w3_s[slot] = w3_f[slot].astype(BF16)
            w2_s[slot] = w2_f[slot].astype(BF16)

            @pl.when(nx_ref[b] >= 0)
            def _():
                for c in fetch(nx_ref[b], 1 - slot):
                    c.start()

    @pl.when(b0 < n_used)
    def _():
        switch(b0)
        switch(b0 + 1)
        slots = [sl_ref[b0], sl_ref[b0 + 1]]
        xs = [_load_rows(x_ref, (), k * MOE_BLOCK, MOE_BLOCK).astype(BF16) for k in range(2)]
        h1 = [jnp.dot(xs[k], w1_s[slots[k]], preferred_element_type=F32) for k in range(2)]
        h3 = [jnp.dot(xs[k], w3_s[slots[k]], preferred_element_type=F32) for k in range(2)]
        for k in range(2):
            hid = (h1[k] * jax.nn.sigmoid(h1[k]) * h3[k]).astype(BF16)
            y = jnp.dot(hid, w2_s[slots[k]], preferred_element_type=F32)
            for j in range(ROW_WORDS):
                w = _pack_pair(y[:, j * LANES:(j + 1) * LANES], y[:, HALF + j * LANES:HALF + (j + 1) * LANES])
                o_ref[pl.ds(k * MOE_BLOCK * ROW_WORDS + j, MOE_BLOCK, stride=ROW_WORDS), :] = w

    @pl.when(b0 >= n_used)
    def _():
        o_ref[...] = jnp.zeros_like(o_ref)


def _experts(xs, plan, w1, w3, w2, layer):
    n_rows = xs.shape[0] // ROW_WORDS
    blk = (2 * MOE_BLOCK * ROW_WORDS, LANES)
    any_spec = pl.BlockSpec(memory_space=pl.ANY)
    return pl.pallas_call(
        functools.partial(_expert_kernel, layer),
        grid_spec=pltpu.PrefetchScalarGridSpec(
            num_scalar_prefetch=5,
            grid=(n_rows // (2 * MOE_BLOCK),),
            in_specs=[
                pl.BlockSpec(blk, lambda p, be, nw, nx, sl, nu: (jnp.minimum(p, (nu[0] - 1) // 2), 0)),
                any_spec, any_spec, any_spec,
            ],
            out_specs=pl.BlockSpec(blk, lambda p, be, nw, nx, sl, nu: (p, 0)),
            scratch_shapes=[
                pltpu.VMEM((2, D_MODEL, EXPERT_FF), F32),
                pltpu.VMEM((2, D_MODEL, EXPERT_FF), F32),
                pltpu.VMEM((2, EXPERT_FF, D_MODEL), F32),
                pltpu.VMEM((2, D_MODEL, EXPERT_FF), BF16),
                pltpu.VMEM((2, D_MODEL, EXPERT_FF), BF16),
                pltpu.VMEM((2, EXPERT_FF, D_MODEL), BF16),
                pltpu.SemaphoreType.DMA((2,)),
            ],
        ),
        out_shape=jax.ShapeDtypeStruct(xs.shape, U32),
        compiler_params=pltpu.CompilerParams(dimension_semantics=("arbitrary",), vmem_limit_bytes=VMEM_BIG),
        name="experts",
    )(*plan, xs, w1, w3, w2)


def _issue_row_gathers(tm, d_ref, ys_ref, ybuf, sem, slot, unrolled):
    def one(t):
        for k in range(2):
            d = pl.multiple_of(d_ref[2 * t + k] * ROW_WORDS, ROW_WORDS)
            r = (k * tm + t) * ROW_WORDS
            if not isinstance(r, int):
                r = pl.multiple_of(r, ROW_WORDS)
            pltpu.make_async_copy(
                ys_ref.at[pl.ds(d, ROW_WORDS)], ybuf.at[slot, pl.ds(r, ROW_WORDS)], sem.at[slot]).start(priority=k)

    if unrolled:
        for t in range(tm):
            one(t)
    else:
        def body(t, c):
            one(t)
            return c

        lax.fori_loop(0, tm, body, 0, unroll=8)


def _combined_tile(tm, r_ref, h_ref, ys_ref, ybuf, sem, slot):
    _rows_copy(ys_ref, ybuf.at[slot], sem.at[slot], 2 * tm * ROW_WORDS).wait()
    r = r_ref[...]
    w0 = r[:, R_W0:R_W0 + 1]
    w1 = r[:, R_W1:R_W1 + 1]
    return h_ref[...] + (w0 * _load_rows(ybuf, (slot,), 0, tm) + w1 * _load_rows(ybuf, (slot,), tm, tm))


def _combine_kernel(tm, n_tiles, dcur_ref, dnext_ref, r_ref, h_ref, fg_ref, ys_ref, o_ref, ybuf, sem):
    i = pl.program_id(0)

    @pl.when(i == 0)
    def _():
        _issue_row_gathers(tm, dcur_ref, ys_ref, ybuf, sem, 0, False)

    @pl.when(i + 1 < n_tiles)
    def _():
        _issue_row_gathers(tm, dnext_ref, ys_ref, ybuf, sem, (i + 1) % 2, False)

    o_ref[...] = _rms(_combined_tile(tm, r_ref, h_ref, ys_ref, ybuf, sem, i % 2), fg_ref[...])


def _combine_specs(tm, n_tiles):
    row = lambda i: (i, 0)
    return [
        pl.BlockSpec((2 * tm,), lambda i: (i,), memory_space=pltpu.SMEM),
        pl.BlockSpec((2 * tm,), lambda i: (jnp.minimum(i + 1, n_tiles - 1),), memory_space=pltpu.SMEM),
        pl.BlockSpec((tm, LANES), row),
        pl.BlockSpec((tm, D_MODEL), row),
    ]


def _combine_scratch(tm):
    return [pltpu.VMEM((2, 2 * tm * ROW_WORDS, LANES), U32), pltpu.SemaphoreType.DMA((2,))]


def _combine_final(ys, dest, route, h2, final_g, tm):
    t = h2.shape[0]
    n_tiles = t // tm
    return pl.pallas_call(
        functools.partial(_combine_kernel, tm, n_tiles),
        grid=(n_tiles,),
        in_specs=_combine_specs(tm, n_tiles)
        + [pl.BlockSpec((1, D_MODEL), lambda i: (0, 0)), pl.BlockSpec(memory_space=pl.ANY)],
        out_specs=pl.BlockSpec((tm, D_MODEL), lambda i: (i, 0)),
        out_shape=jax.ShapeDtypeStruct((t, D_MODEL), F32),
        scratch_shapes=_combine_scratch(tm),
        compiler_params=pltpu.CompilerParams(dimension_semantics=("arbitrary",)),
        name="combine",
    )(dest, dest, route, h2, final_g, ys)


def _combine_proj_kernel(tm, n_tiles, tiles_per_seq, dcur_ref, dnext_ref, r_ref, hm_ref, g_ref, w_ref, bg_ref,
                         rot_ref, cw_ref, ys_ref, o_ref, h_ref, ybuf, sem, carry_s):
    i = pl.program_id(0)
    slot = i % 2
    first = (i % tiles_per_seq) == 0

    @pl.when(i == 0)
    def _():
        carry_s[...] = jnp.zeros_like(carry_s)
        _issue_row_gathers(tm, dcur_ref, ys_ref, ybuf, sem, 0, False)

    h = _combined_tile(tm, r_ref, hm_ref, ys_ref, ybuf, sem, slot)
    h_ref[...] = h
    _issue_row_gathers(tm, dnext_ref, ys_ref, ybuf, sem, 1 - slot, True)
    _proj_body(_rms(h, g_ref[...]).astype(BF16), first, w_ref, bg_ref, rot_ref, cw_ref, o_ref, carry_s)

    @pl.when(i == n_tiles - 1)
    def _():
        _rows_copy(ys_ref, ybuf.at[1 - slot], sem.at[1 - slot], 2 * tm * ROW_WORDS).wait()


def _combine_proj(ys, dest, route, h2, g1, w_in, b_gate, rot, conv_w, seq, tm):
    t = h2.shape[0]
    n_tiles = t // tm
    row = lambda i: (i, 0)
    return pl.pallas_call(
        functools.partial(_combine_proj_kernel, tm, n_tiles, seq // tm),
        grid=(n_tiles,),
        in_specs=_combine_specs(tm, n_tiles) + _proj_specs(w_in, conv_w, tm) + [pl.BlockSpec(memory_space=pl.ANY)],
        out_specs=[pl.BlockSpec((tm, P_COLS), row), pl.BlockSpec((tm, D_MODEL), row)],
        out_shape=[jax.ShapeDtypeStruct((t, P_COLS), BF16), jax.ShapeDtypeStruct((t, D_MODEL), F32)],
        scratch_shapes=_combine_scratch(tm) + [pltpu.VMEM((8, D_MODEL), F32)],
        compiler_params=pltpu.CompilerParams(dimension_semantics=("arbitrary",), vmem_limit_bytes=VMEM_BIG),
        name="combine_proj",
    )(dest, dest, route, h2, g1, w_in, b_gate, rot, conv_w, ys)


def _rotary_table(positions):
    half = ROT_DIM // 2
    inv = ROPE_THETA ** (-jnp.arange(0, ROT_DIM, 2, dtype=F32) / ROT_DIM)
    ang = positions.reshape(-1).astype(F32)[:, None] * inv
    cs1 = jnp.concatenate([jnp.cos(ang), jnp.sin(ang), jnp.ones((ang.shape[0], 1), F32)], axis=1)
    lane = jnp.arange(LANES) % HEAD_DIM
    j = jnp.arange(2 * half + 1)[:, None]
    a = jnp.where(lane < half, j == lane, jnp.where(lane < ROT_DIM, j == lane - half, j == 2 * half))
    b = -((lane < half) & (j == half + lane)).astype(F32)
    c = ((lane >= half) & (lane < ROT_DIM) & (j == lane)).astype(F32)
    place = jnp.concatenate([a.astype(F32), b, c], axis=1)
    return jnp.dot(cs1, place, precision=lax.Precision.HIGHEST)


def _sink_table(sinks):
    s = sinks.astype(F32).reshape(N_KV_HEADS, 4, 2).transpose(0, 2, 1)
    s = jnp.broadcast_to(s[..., None], (N_KV_HEADS, 2, 4, CHUNK)).reshape(2 * N_KV_HEADS, 4 * CHUNK)
    return jnp.concatenate([s, jnp.zeros((8 - 2 * N_KV_HEADS, 4 * CHUNK), F32)], axis=0)


def _router_params(w_group, b_group, w_route, b_route):
    d = w_group.shape[0]
    pad = LANES - N_GROUPS - N_EXPERTS
    wt = jnp.concatenate([w_group.T, w_route.transpose(0, 2, 1).reshape(N_EXPERTS, d), jnp.zeros((pad, d), F32)], axis=0)
    hi = wt.astype(BF16)
    lo = (wt - hi.astype(F32)).astype(BF16)
    bias = jnp.concatenate([b_group, b_route.reshape(-1), jnp.zeros((pad,), F32)])
    return jnp.concatenate([hi, lo], axis=0), jnp.broadcast_to(bias[:, None], (LANES, LANES))


def _moe_plan(route, counts, n_tok):
    n_rows = -(-(2 * n_tok) // MOE_BLOCK) * MOE_BLOCK + N_EXPERTS * MOE_BLOCK
    n_blocks = n_rows // MOE_BLOCK
    ids = jnp.arange(N_EXPERTS, dtype=jnp.int32)
    cnt = counts[R_ROW0:R_ROW0 + N_EXPERTS, 0].astype(jnp.int32)
    padded = (cnt + MOE_BLOCK - 1) // MOE_BLOCK * MOE_BLOCK
    pad_end = jnp.sum(jnp.where(ids[None, :] <= ids[:, None], padded[None, :], 0), axis=1)
    pad_start = pad_end - padded
    e = route[:, R_E0:R_E1 + 1].astype(jnp.int32)
    rank = route[:, R_RANK0:R_RANK1 + 1].astype(jnp.int32)
    dest = (jnp.sum(jnp.where(e[..., None] == ids, pad_start, 0), axis=-1) + rank).reshape(-1)
    n_used = pad_end[-1] // MOE_BLOCK
    blk = jnp.arange(n_blocks, dtype=jnp.int32)
    be = jnp.minimum(jnp.sum((pad_end[None, :] <= blk[:, None] * MOE_BLOCK).astype(jnp.int32), axis=1), N_EXPERTS - 1)
    be_last = jnp.sum(jnp.where(blk == n_used - 1, be, 0))
    be = jnp.where(blk < n_used, be, be_last)
    new_expert = jnp.concatenate([jnp.ones((1,), jnp.int32), (be[1:] != be[:-1]).astype(jnp.int32)])
    later = blk[None, :] > blk[:, None]
    seq_slot = (jnp.sum(jnp.where(later.T | (blk[None, :] == blk[:, None]), new_expert[None, :], 0), axis=1) - 1) % 2
    next_start = jnp.min(jnp.where(later & (new_expert[None, :] == 1), blk[None, :], n_blocks), axis=1)
    next_expert = jnp.sum(jnp.where(blk[None, :] == next_start[:, None], be[None, :], 0), axis=1)
    next_expert = jnp.where(next_start < n_blocks, next_expert, -1)
    last = pad_end // MOE_BLOCK - 1
    is_last = jnp.any((blk[:, None] == last[None, :]) & (padded[None, :] > 0), axis=1)
    zero_mask = (is_last | (blk >= n_used)).astype(jnp.int32)
    plan = (be, new_expert, next_expert.astype(jnp.int32), seq_slot.astype(jnp.int32), n_used.astype(jnp.int32).reshape(1))
    return n_rows, dest, plan, zero_mask


def kernel(x, positions, norm1_g, w_in, b_gate, conv_w, sinks, w_conv_out, w_attn_out, w_o, norm2_g, w_group, b_group, w_route, b_route, w1, w3, w2, final_g):
    b, s, d = x.shape
    t = b * s
    depth = w_in.shape[0]
    tm = 512
    rot = _rotary_table(positions)
    h = x.reshape(t, d)
    ys = dest = route = None
    for l in range(depth):
        w_in_l = w_in[l].astype(BF16)
        if l == 0:
            p = _proj(h, norm1_g[l][None], w_in_l, b_gate[l][None], rot, conv_w[l], s, tm)
        else:
            p, h = _combine_proj(ys, dest, route, h, norm1_g[l][None], w_in_l, b_gate[l][None], rot, conv_w[l], s, tm)
        g2 = norm2_g[l][None]
        wt_router, bt_router = _router_params(w_group[l], b_group[l], w_route[l], b_route[l])
        h, route, counts, hp = _mix(p, h, _sink_table(sinks[l]), w_conv_out[l].astype(BF16),
                                    w_attn_out[l].astype(BF16), w_o[l].astype(BF16), g2, wt_router, bt_router, s, tm)
        n_rows, dest, plan, zero_mask = _moe_plan(route, counts, t)
        xs = _dispatch(hp, dest, zero_mask, n_rows, tm)
        ys = _experts(xs, plan, w1, w3, w2, l)
    return _combine_final(ys, dest, route, h, final_g[None], tm).reshape(b, s, d)
```

```python
import functools

import jax
import jax.numpy as jnp
from jax import lax
from jax.experimental import pallas as pl
from jax.experimental.pallas import tpu as pltpu

F32 = jnp.float32
BF16 = jnp.bfloat16
U32 = jnp.uint32

D_MODEL = 1024
HALF = D_MODEL // 2
RMS_EPS = 1e-5
CHUNK = 64
WINDOW_CHUNKS = 2
HEAD_DIM = 64
N_Q_HEADS = 16
N_KV_HEADS = 2
ROT_DIM = 16
ROPE_THETA = 500000.0
N_GROUPS = 8
EXPERTS_PER_GROUP = 8
N_EXPERTS = 64
EXPERT_FF = 512
MOE_BLOCK = 256
STEP_BLOCKS = 2
LANES = 128

W_CB, W_CC, W_CX, W_Q, W_K, W_G = 0, 1024, 2048, 3072, 4096, 4352
P_CI, P_Q, P_KV, P_G = 0, 1024, 2048, 2304
P_COLS = 4352
HALO = WINDOW_CHUNKS * CHUNK
KEYS = HALO + CHUNK
SCORES_AHEAD = 4

VMEM_BIG = 56 * 1024 * 1024
NT = (((1,), (1,)), ((), ()))


def _rms(x, g):
    ms = jnp.mean(x * x, axis=-1, keepdims=True)
    return x * lax.rsqrt(ms + RMS_EPS) * g


def _pack_pair(lo, hi):
    def rnd(x):
        return lax.bitcast_convert_type(x, U32) + U32(0x8000)

    return (rnd(hi) & U32(0xFFFF0000)) | (rnd(lo) >> 16)


def _unpack_pair(w):
    lo = lax.bitcast_convert_type(w << 16, F32)
    hi = lax.bitcast_convert_type(w & U32(0xFFFF0000), F32)
    return lo, hi


ROW_WORDS = HALF // LANES


def _store_rows(ref, lead, n, x):
    for j in range(ROW_WORDS):
        w = _pack_pair(x[:, j * LANES:(j + 1) * LANES], x[:, HALF + j * LANES:HALF + (j + 1) * LANES])
        ref[lead + (pl.ds(j, n, stride=ROW_WORDS), slice(None))] = w


def _load_rows(ref, lead, row0, n):
    parts = [_unpack_pair(ref[lead + (pl.ds(row0 * ROW_WORDS + j, n, stride=ROW_WORDS), slice(None))])
             for j in range(ROW_WORDS)]
    return jnp.concatenate([p[0] for p in parts] + [p[1] for p in parts], axis=1)


def _proj_kernel(tiles_per_seq, x_ref, g_ref, w_ref, bg_ref, rot_ref, cw_ref, o_ref, carry_s):
    first = (pl.program_id(0) % tiles_per_seq) == 0

    @pl.when(pl.program_id(0) == 0)
    def _():
        carry_s[...] = jnp.zeros_like(carry_s)

    _proj_body(_rms(x_ref[...], g_ref[...]).astype(BF16), first, w_ref, bg_ref, rot_ref, cw_ref, o_ref, carry_s)


def _proj_body(xn, first, w_ref, bg_ref, rot_ref, cw_ref, o_ref, carry_s):
    tm = xn.shape[0]

    def mm(c0, n):
        return jnp.dot(xn, w_ref[:, c0:c0 + n], preferred_element_type=F32)

    ra = rot_ref[:, 0:LANES]
    rb = rot_ref[:, LANES:2 * LANES]
    rc = rot_ref[:, 2 * LANES:3 * LANES]

    def rot(t):
        return t * ra + pltpu.roll(t, LANES - ROT_DIM // 2, 1) * rb + pltpu.roll(t, ROT_DIM // 2, 1) * rc

    nc = 512
    row = lax.broadcasted_iota(jnp.int32, (tm, 1), 0)
    cw = cw_ref[...]
    for j in range(4):
        gt = mm(W_G + j * nc, nc) + bg_ref[:, j * nc:(j + 1) * nc]
        o_ref[:, P_G + j * nc:P_G + (j + 1) * nc] = jax.nn.sigmoid(gt).astype(BF16)
    for j in range(2):
        q = mm(W_Q + j * nc, nc)
        for p in range(nc // LANES):
            t = rot(q[:, p * LANES:(p + 1) * LANES]) * (HEAD_DIM ** -0.5)
            o_ref[:, P_Q + j * nc + p * LANES:P_Q + j * nc + (p + 1) * LANES] = t.astype(BF16)
    for j in range(2):
        cs = slice(j * nc, (j + 1) * nc)
        u = mm(W_CB + j * nc, nc) * mm(W_CX + j * nc, nc)
        prev = carry_s[:, cs]
        h1 = jnp.where(first, 0.0, prev[7:8])
        h2 = jnp.where(first, 0.0, prev[6:7])
        s1 = jnp.where(row == 0, h1, pltpu.roll(u, 1, 0))
        s2 = jnp.where(row == 0, h2, jnp.where(row == 1, h1, pltpu.roll(u, 2, 0)))
        y = cw[2:3, cs] * u + cw[1:2, cs] * s1 + cw[0:1, cs] * s2
        carry_s[:, cs] = u[tm - 8:tm]
        o_ref[:, P_CI + j * nc:P_CI + (j + 1) * nc] = (mm(W_CC + j * nc, nc) * y).astype(BF16)
    kv = mm(W_K, 2 * LANES)
    o_ref[:, P_KV:P_KV + LANES] = rot(kv[:, :LANES]).astype(BF16)
    o_ref[:, P_KV + LANES:P_KV + 2 * LANES] = kv[:, LANES:].astype(BF16)


def _proj_specs(w_in, conv_w, tm):
    fixed = lambda i: (0, 0)
    return [
        pl.BlockSpec((1, D_MODEL), fixed),
        pl.BlockSpec(w_in.shape, fixed, pipeline_mode=pl.Buffered(1)),
        pl.BlockSpec((1, 2 * D_MODEL), fixed),
        pl.BlockSpec((tm, 3 * LANES), lambda i: (i, 0)),
        pl.BlockSpec(conv_w.shape, fixed),
    ]


def _proj(x2, g1, w_in, b_gate, rot, conv_w, seq, tm):
    t = x2.shape[0]
    row = lambda i: (i, 0)
    return pl.pallas_call(
        functools.partial(_proj_kernel, seq // tm),
        grid=(t // tm,),
        in_specs=[pl.BlockSpec((tm, D_MODEL), row)] + _proj_specs(w_in, conv_w, tm),
        out_specs=pl.BlockSpec((tm, P_COLS), row),
        out_shape=jax.ShapeDtypeStruct((t, P_COLS), BF16),
        scratch_shapes=[pltpu.VMEM((8, D_MODEL), F32)],
        compiler_params=pltpu.CompilerParams(dimension_semantics=("arbitrary",), vmem_limit_bytes=VMEM_BIG),
        name="proj",
    )(x2, g1, w_in, b_gate, rot, conv_w)


R_E0, R_E1, R_RANK0, R_RANK1, R_W0, R_W1 = range(6)
R_ROW0 = N_GROUPS


def _route_tile(tm, h, g_ref, wt_ref, bt_ref, r_ref, rt_ref, cnt_ref, base_s):
    hn = _rms(h, g_ref[...])
    hi = hn.astype(BF16)
    lo = (hn - hi.astype(F32)).astype(BF16)
    both = lax.dot_general(wt_ref[...], hi, NT, preferred_element_type=F32)
    lt = (both[:LANES] + both[LANES:] + lax.dot_general(wt_ref[0:LANES, :], lo, NT, preferred_element_type=F32)
          + bt_ref[:, 0:1])
    row = lax.broadcasted_iota(jnp.int32, (LANES, tm), 0)
    neg = -jnp.inf

    def first_max(v):
        m = jnp.max(v, axis=0, keepdims=True)
        return m, jnp.min(jnp.where(v == m, row, LANES), axis=0, keepdims=True)

    gl = jnp.where(row < N_GROUPS, lt, neg)
    gmax, gtop = first_max(gl)
    pg = 1.0 / jnp.sum(jnp.exp(gl - gmax), axis=0, keepdims=True)
    in_group = (row >= R_ROW0) & (row < R_ROW0 + N_EXPERTS) & (((row - R_ROW0) >> 3) == gtop)
    el = jnp.where(in_group, lt, neg)
    m1, i1 = first_max(el)
    m2, i2 = first_max(jnp.where(row == i1, neg, el))
    e2 = jnp.exp(m2 - m1)
    den = 1.0 + e2
    w0 = pg * (1.0 / den)
    w1 = pg * (e2 / den)
    hit0 = row == i1
    hit1 = row == i2
    a = jnp.where(hit0 | hit1, 1.0, 0.0)
    rr = lax.broadcasted_iota(jnp.int32, (tm, tm), 0)
    cc = lax.broadcasted_iota(jnp.int32, (tm, tm), 1)
    tri = jnp.where(rr < cc, 1.0, 0.0).astype(BF16)
    before = jnp.dot(a.astype(BF16), tri, preferred_element_type=F32) + base_s[:, 0:1]
    rank0 = jnp.sum(jnp.where(hit0, before, 0.0), axis=0, keepdims=True)
    rank1 = jnp.sum(jnp.where(hit1, before, 0.0), axis=0, keepdims=True)
    base_s[...] = base_s[...] + jnp.sum(a, axis=1, keepdims=True)
    cnt_ref[...] = base_s[...]
    vals = ((i1 - R_ROW0).astype(F32), (i2 - R_ROW0).astype(F32), rank0, rank1, w0, w1)
    slab = jnp.zeros((LANES, tm), F32)
    for k, v in enumerate(vals):
        slab = jnp.where(row == k, v, slab)
    r_ref[...] = slab.T
    rt_ref[...] = slab[0:8]
    return hn


def _mix_kernel(tm, tiles_per_seq, p_ref, hkv_ref, x_ref, sink_ref, wco_ref, wao_ref, wo_ref,
                g2_ref, wt_ref, bt_ref, o_ref, r_ref, rt_ref, cnt_ref, hp_ref, ke_s, ko_s, vta_s, vtb_s, attn_s, base_s):
    i = pl.program_id(0)
    first = (i % tiles_per_seq) == 0
    lo = lax.broadcasted_iota(jnp.int32, (1, LANES), 1) < HEAD_DIM

    @pl.when(i == 0)
    def _():
        base_s[...] = jnp.zeros_like(base_s)

    kv_all = jnp.concatenate([hkv_ref[...], p_ref[:, P_KV:P_KV + 2 * LANES]], axis=0)
    k = kv_all[:, :LANES].astype(F32)
    kr = pltpu.roll(k, HEAD_DIM, 1)
    ke_s[0] = jnp.where(lo, k, 0.0).astype(BF16)
    ko_s[0] = jnp.where(lo, 0.0, kr).astype(BF16)
    ke_s[1] = jnp.where(lo, kr, 0.0).astype(BF16)
    ko_s[1] = jnp.where(lo, 0.0, k).astype(BF16)
    v = kv_all[:, LANES:].astype(F32)
    n_keys = HALO + tm
    for j in range(n_keys // LANES):
        vta_s[:, j * LANES:(j + 1) * LANES] = v[j * LANES:(j + 1) * LANES].T.astype(BF16)
    for j in range(n_keys // LANES - 1):
        vtb_s[:, j * LANES:(j + 1) * LANES] = v[CHUNK + j * LANES:CHUNK + (j + 1) * LANES].T.astype(BF16)
    vtb_s[:, n_keys - LANES:n_keys - CHUNK] = v[n_keys - CHUNK:n_keys].T.astype(BF16)

    krow = lax.broadcasted_iota(jnp.int32, (2 * KEYS, 1), 0)
    krow = jnp.where(krow >= KEYS, krow - KEYS, krow)

    def scores(c, g):
        r0 = c * CHUNK
        qbase = P_Q + g * 4 * LANES
        q = jnp.concatenate(
            [p_ref[r0:r0 + CHUNK, qbase + p * LANES:qbase + (p + 1) * LANES] for p in range(4)], axis=0)
        kk = jnp.concatenate([ke_s[g, r0:r0 + KEYS, :], ko_s[g, r0:r0 + KEYS, :]], axis=0)
        st = lax.dot_general(kk, q, NT, preferred_element_type=F32)
        if r0 < HALO:
            nbad = jnp.where(first, HALO - r0, 0)
            st = jnp.where(krow < nbad, -jnp.inf, st)
        return st

    def finish(c, g, st):
        r0 = c * CHUNK
        vt_s, v0 = (vta_s, r0) if c % 2 == 0 else (vtb_s, r0 - CHUNK)
        vt = vt_s[g * HEAD_DIM:(g + 1) * HEAD_DIM, v0:v0 + KEYS]
        outs = []
        for par in range(2):
            s = st[par * KEYS:(par + 1) * KEYS]
            sc = sink_ref[2 * g + par:2 * g + par + 1, :]
            m = jnp.maximum(jnp.max(s, axis=0, keepdims=True), sc)
            p = jnp.exp(s - m)
            l = jnp.sum(p, axis=0, keepdims=True) + jnp.exp(sc - m)
            outs.append(jnp.dot(vt, p.astype(BF16), preferred_element_type=F32) / l)
        ot = jnp.concatenate(outs, axis=0)
        for p2 in range(2):
            blk = ot[:, p2 * LANES:(p2 + 1) * LANES].T
            for h in range(2):
                c0 = (g * 4 + 2 * p2 + h) * LANES
                attn_s[r0:r0 + CHUNK, c0:c0 + LANES] = blk[h * CHUNK:(h + 1) * CHUNK].astype(BF16)

    units = [(c, g) for c in range(tm // CHUNK) for g in range(N_KV_HEADS)]
    pending = [scores(*u) for u in units[:SCORES_AHEAD]]
    for n, (c, g) in enumerate(units):
        st = pending.pop(0)
        if n + SCORES_AHEAD < len(units):
            pending.append(scores(*units[n + SCORES_AHEAD]))
        finish(c, g, st)

    conv_out = jnp.dot(p_ref[:, P_CI:P_CI + D_MODEL], wco_ref[...], preferred_element_type=F32)
    attn_out = jnp.dot(attn_s[...], wao_ref[...], preferred_element_type=F32)
    gc = p_ref[:, P_G:P_G + D_MODEL].astype(F32)
    ga = p_ref[:, P_G + D_MODEL:P_G + 2 * D_MODEL].astype(F32)
    merged = (gc * conv_out + ga * attn_out).astype(BF16)
    h_new = x_ref[...] + jnp.dot(merged, wo_ref[...], preferred_element_type=F32)
    o_ref[...] = h_new
    hn = _route_tile(tm, h_new, g2_ref, wt_ref, bt_ref, r_ref, rt_ref, cnt_ref, base_s)
    _store_rows(hp_ref, (), tm, hn)


def _mix(p, x2, sink_tab, w_co, w_ao, w_o, g2, wt_router, bt_router, seq, tm):
    t = x2.shape[0]
    row = lambda i: (i, 0)
    fixed = lambda i: (0, 0)
    sq = (D_MODEL, D_MODEL)
    n_keys = HALO + tm
    return pl.pallas_call(
        functools.partial(_mix_kernel, tm, seq // tm),
        grid=(t // tm,),
        in_specs=[
            pl.BlockSpec((tm, P_COLS), row),
            pl.BlockSpec((HALO, 2 * LANES), lambda i: (jnp.maximum(i * (tm // HALO) - 1, 0), P_KV // (2 * LANES))),
            pl.BlockSpec((tm, D_MODEL), row),
            pl.BlockSpec(sink_tab.shape, fixed),
            pl.BlockSpec(sq, fixed),
            pl.BlockSpec(sq, fixed),
            pl.BlockSpec(sq, fixed),
            pl.BlockSpec((1, D_MODEL), fixed),
            pl.BlockSpec((2 * LANES, D_MODEL), fixed),
            pl.BlockSpec((LANES, LANES), fixed),
        ],
        out_specs=[
            pl.BlockSpec((tm, D_MODEL), row),
            pl.BlockSpec((tm, LANES), row),
            pl.BlockSpec((8, tm), lambda i: (0, i)),
            pl.BlockSpec((LANES, LANES), fixed),
            pl.BlockSpec((tm * ROW_WORDS, LANES), row),
        ],
        out_shape=[
            jax.ShapeDtypeStruct((t, D_MODEL), F32),
            jax.ShapeDtypeStruct((t, LANES), F32),
            jax.ShapeDtypeStruct((8, t), F32),
            jax.ShapeDtypeStruct((LANES, LANES), F32),
            jax.ShapeDtypeStruct((t * ROW_WORDS, LANES), U32),
        ],
        scratch_shapes=[
            pltpu.VMEM((N_KV_HEADS, n_keys, LANES), BF16),
            pltpu.VMEM((N_KV_HEADS, n_keys, LANES), BF16),
            pltpu.VMEM((2 * HEAD_DIM, n_keys), BF16),
            pltpu.VMEM((2 * HEAD_DIM, n_keys), BF16),
            pltpu.VMEM((tm, D_MODEL), BF16),
            pltpu.VMEM((LANES, LANES), F32),
        ],
        compiler_params=pltpu.CompilerParams(dimension_semantics=("arbitrary",), vmem_limit_bytes=VMEM_BIG),
        name="mix",
    )(p, p, x2, sink_tab, w_co, w_ao, w_o, g2, wt_router, bt_router)


def _rows_copy(src_ref, dst_ref, sem, n):
    return pltpu.make_async_copy(src_ref.at[pl.ds(0, n)], dst_ref.at[pl.ds(0, n)], sem)


def _dispatch_kernel(tm, n_tiles, n_blocks, zm_ref, d_ref, hp_ref, xs_ref, hn_s, zbuf, zsem, sem):
    i = pl.program_id(0)
    blk_rows = MOE_BLOCK * ROW_WORDS

    @pl.when(i == 0)
    def _():
        zbuf[...] = jnp.zeros_like(zbuf)

        def zero_copy(b):
            r = pl.multiple_of(b * blk_rows, blk_rows)
            return pltpu.make_async_copy(zbuf, xs_ref.at[pl.ds(r, blk_rows)], zsem)

        def start(b, c):
            @pl.when(zm_ref[b] == 1)
            def _():
                zero_copy(b).start()
            return c

        def wait(b, c):
            @pl.when(zm_ref[b] == 1)
            def _():
                zero_copy(b).wait()
            return c

        lax.fori_loop(0, n_blocks, start, 0)
        lax.fori_loop(0, n_blocks, wait, 0)

    slot = i % 2
    hn_s[slot] = hp_ref[...]

    def body(t, c):
        src = hn_s.at[slot, pl.ds(pl.multiple_of(t * ROW_WORDS, ROW_WORDS), ROW_WORDS)]
        for k in range(2):
            d = pl.multiple_of(d_ref[k * tm + t] * ROW_WORDS, ROW_WORDS)
            pltpu.make_async_copy(src, xs_ref.at[pl.ds(d, ROW_WORDS)], sem.at[slot]).start(priority=k)
        return c

    lax.fori_loop(0, tm, body, 0, unroll=8)

    def drain(s):
        for _ in range(2):
            _rows_copy(hn_s.at[s], xs_ref, sem.at[s], tm * ROW_WORDS).wait()

    @pl.when(i > 0)
    def _():
        drain(1 - slot)

    @pl.when(i == n_tiles - 1)
    def _():
        drain(slot)


def _dispatch(hp, dest, zero_mask, n_rows, tm):
    n_tiles = hp.shape[0] // (tm * ROW_WORDS)
    n_blocks = n_rows // MOE_BLOCK
    return pl.pallas_call(
        functools.partial(_dispatch_kernel, tm, n_tiles, n_blocks),
        grid_spec=pltpu.PrefetchScalarGridSpec(
            num_scalar_prefetch=1,
            grid=(n_tiles,),
            in_specs=[
                pl.BlockSpec((2 * tm,), lambda i, zm: (i,), memory_space=pltpu.SMEM),
                pl.BlockSpec((tm * ROW_WORDS, LANES), lambda i, zm: (i, 0)),
            ],
            out_specs=pl.BlockSpec(memory_space=pl.ANY),
            scratch_shapes=[
                pltpu.VMEM((2, tm * ROW_WORDS, LANES), U32),
                pltpu.VMEM((MOE_BLOCK * ROW_WORDS, LANES), U32),
                pltpu.SemaphoreType.DMA(()),
                pltpu.SemaphoreType.DMA((2,)),
            ],
        ),
        out_shape=jax.ShapeDtypeStruct((n_rows * ROW_WORDS, LANES), U32),
        compiler_params=pltpu.CompilerParams(dimension_semantics=("arbitrary",)),
        name="dispatch",
    )(zero_mask, dest, hp)


def _expert_kernel(layer, be_ref, nw_ref, nx_ref, sl_ref, nu_ref, x_ref, w1_hbm, w3_hbm, w2_hbm, o_ref,
                   w1_f, w3_f, w2_f, w1_s, w3_s, w2_s, sem):
    b0 = STEP_BLOCKS * pl.program_id(0)
    n_used = nu_ref[0]

    def fetch(e, slot):
        return (pltpu.make_async_copy(w1_hbm.at[layer, e], w1_f.at[slot], sem.at[slot]),
                pltpu.make_async_copy(w3_hbm.at[layer, e], w3_f.at[slot], sem.at[slot]),
                pltpu.make_async_copy(w2_hbm.at[layer, e], w2_f.at[slot], sem.at[slot]))

    def switch(b):
        @pl.when((b < n_used) & (nw_ref[b] == 1))
        def _():
            slot = sl_ref[b]

            @pl.when(b == 0)
            def _():
                for c in fetch(be_ref[0], 0):
                    c.start()

            for c in fetch(be_ref[b], slot):
                c.wait()
            w1_s[slot] = w1_f[slot].astype(BF16)
            w3_s[slot] = w3_f[slot].astype(BF16)
            w2_s[slot] = w2_f[slot].astype(BF16)

            @pl.when(nx_ref[b] >= 0)
            def _():
                for c in fetch(nx_ref[b], 1 - slot):
                    c.start()

    @pl.when(b0 < n_used)
    def _():
        for k in range(STEP_BLOCKS):
            switch(b0 + k)
        nb = STEP_BLOCKS
        slots = [sl_ref[b0 + k] for k in range(nb)]
        xs = [_load_rows(x_ref, (), k * MOE_BLOCK, MOE_BLOCK).astype(BF16) for k in range(nb)]
        h1 = [jnp.dot(xs[k], w1_s[slots[k]], preferred_element_type=F32) for k in range(nb)]
        h3 = [jnp.dot(xs[k], w3_s[slots[k]], preferred_element_type=F32) for k in range(nb)]
        for k in range(nb):
            hid = (h1[k] * jax.nn.sigmoid(h1[k]) * h3[k]).astype(BF16)
            y = jnp.dot(hid, w2_s[slots[k]], preferred_element_type=F32)
            for j in range(ROW_WORDS):
                w = _pack_pair(y[:, j * LANES:(j + 1) * LANES], y[:, HALF + j * LANES:HALF + (j + 1) * LANES])
                o_ref[pl.ds(k * MOE_BLOCK * ROW_WORDS + j, MOE_BLOCK, stride=ROW_WORDS), :] = w

    @pl.when(b0 >= n_used)
    def _():
        o_ref[...] = jnp.zeros_like(o_ref)


def _experts(xs, plan, w1, w3, w2, layer):
    n_rows = xs.shape[0] // ROW_WORDS
    blk = (STEP_BLOCKS * MOE_BLOCK * ROW_WORDS, LANES)
    any_spec = pl.BlockSpec(memory_space=pl.ANY)
    return pl.pallas_call(
        functools.partial(_expert_kernel, layer),
        grid_spec=pltpu.PrefetchScalarGridSpec(
            num_scalar_prefetch=5,
            grid=(n_rows // (STEP_BLOCKS * MOE_BLOCK),),
            in_specs=[
                pl.BlockSpec(blk, lambda p, be, nw, nx, sl, nu: (jnp.minimum(p, (nu[0] - 1) // STEP_BLOCKS), 0)),
                any_spec, any_spec, any_spec,
            ],
            out_specs=pl.BlockSpec(blk, lambda p, be, nw, nx, sl, nu: (p, 0)),
            scratch_shapes=[
                pltpu.VMEM((2, D_MODEL, EXPERT_FF), F32),
                pltpu.VMEM((2, D_MODEL, EXPERT_FF), F32),
                pltpu.VMEM((2, EXPERT_FF, D_MODEL), F32),
                pltpu.VMEM((2, D_MODEL, EXPERT_FF), BF16),
                pltpu.VMEM((2, D_MODEL, EXPERT_FF), BF16),
                pltpu.VMEM((2, EXPERT_FF, D_MODEL), BF16),
                pltpu.SemaphoreType.DMA((2,)),
            ],
        ),
        out_shape=jax.ShapeDtypeStruct(xs.shape, U32),
        compiler_params=pltpu.CompilerParams(dimension_semantics=("arbitrary",), vmem_limit_bytes=VMEM_BIG),
        name="experts",
    )(*plan, xs, w1, w3, w2)


def _issue_row_gathers(tm, d_ref, ys_ref, ybuf, sem, slot, unrolled):
    def one(t):
        for k in range(2):
            d = pl.multiple_of(d_ref[k * tm + t] * ROW_WORDS, ROW_WORDS)
            r = (k * tm + t) * ROW_WORDS
            if not isinstance(r, int):
                r = pl.multiple_of(r, ROW_WORDS)
            pltpu.make_async_copy(
                ys_ref.at[pl.ds(d, ROW_WORDS)], ybuf.at[slot, pl.ds(r, ROW_WORDS)], sem.at[slot]).start(priority=k)

    if unrolled:
        for t in range(tm):
            one(t)
    else:
        def body(t, c):
            one(t)
            return c

        lax.fori_loop(0, tm, body, 0, unroll=8)


def _combined_tile(tm, r_ref, h_ref, ys_ref, ybuf, sem, slot):
    _rows_copy(ys_ref, ybuf.at[slot], sem.at[slot], 2 * tm * ROW_WORDS).wait()
    r = r_ref[...]
    w0 = r[:, R_W0:R_W0 + 1]
    w1 = r[:, R_W1:R_W1 + 1]
    return h_ref[...] + (w0 * _load_rows(ybuf, (slot,), 0, tm) + w1 * _load_rows(ybuf, (slot,), tm, tm))


def _combine_kernel(tm, n_tiles, dcur_ref, dnext_ref, r_ref, h_ref, fg_ref, ys_ref, o_ref, ybuf, sem):
    i = pl.program_id(0)

    @pl.when(i == 0)
    def _():
        _issue_row_gathers(tm, dcur_ref, ys_ref, ybuf, sem, 0, False)

    @pl.when(i + 1 < n_tiles)
    def _():
        _issue_row_gathers(tm, dnext_ref, ys_ref, ybuf, sem, (i + 1) % 2, False)

    o_ref[...] = _rms(_combined_tile(tm, r_ref, h_ref, ys_ref, ybuf, sem, i % 2), fg_ref[...])


def _combine_specs(tm, n_tiles):
    row = lambda i: (i, 0)
    return [
        pl.BlockSpec((2 * tm,), lambda i: (i,), memory_space=pltpu.SMEM),
        pl.BlockSpec((2 * tm,), lambda i: (jnp.minimum(i + 1, n_tiles - 1),), memory_space=pltpu.SMEM),
        pl.BlockSpec((tm, LANES), row),
        pl.BlockSpec((tm, D_MODEL), row),
    ]


def _combine_scratch(tm):
    return [pltpu.VMEM((2, 2 * tm * ROW_WORDS, LANES), U32), pltpu.SemaphoreType.DMA((2,))]


def _combine_final(ys, dest, route, h2, final_g, tm):
    t = h2.shape[0]
    n_tiles = t // tm
    return pl.pallas_call(
        functools.partial(_combine_kernel, tm, n_tiles),
        grid=(n_tiles,),
        in_specs=_combine_specs(tm, n_tiles)
        + [pl.BlockSpec((1, D_MODEL), lambda i: (0, 0)), pl.BlockSpec(memory_space=pl.ANY)],
        out_specs=pl.BlockSpec((tm, D_MODEL), lambda i: (i, 0)),
        out_shape=jax.ShapeDtypeStruct((t, D_MODEL), F32),
        scratch_shapes=_combine_scratch(tm),
        compiler_params=pltpu.CompilerParams(dimension_semantics=("arbitrary",)),
        name="combine",
    )(dest, dest, route, h2, final_g, ys)


def _combine_proj_kernel(tm, n_tiles, tiles_per_seq, dcur_ref, dnext_ref, r_ref, hm_ref, g_ref, w_ref, bg_ref,
                         rot_ref, cw_ref, ys_ref, o_ref, h_ref, ybuf, sem, carry_s):
    i = pl.program_id(0)
    slot = i % 2
    first = (i % tiles_per_seq) == 0

    @pl.when(i == 0)
    def _():
        carry_s[...] = jnp.zeros_like(carry_s)
        _issue_row_gathers(tm, dcur_ref, ys_ref, ybuf, sem, 0, False)

    h = _combined_tile(tm, r_ref, hm_ref, ys_ref, ybuf, sem, slot)
    h_ref[...] = h
    _issue_row_gathers(tm, dnext_ref, ys_ref, ybuf, sem, 1 - slot, True)
    _proj_body(_rms(h, g_ref[...]).astype(BF16), first, w_ref, bg_ref, rot_ref, cw_ref, o_ref, carry_s)

    @pl.when(i == n_tiles - 1)
    def _():
        _rows_copy(ys_ref, ybuf.at[1 - slot], sem.at[1 - slot], 2 * tm * ROW_WORDS).wait()


def _combine_proj(ys, dest, route, h2, g1, w_in, b_gate, rot, conv_w, seq, tm):
    t = h2.shape[0]
    n_tiles = t // tm
    row = lambda i: (i, 0)
    return pl.pallas_call(
        functools.partial(_combine_proj_kernel, tm, n_tiles, seq // tm),
        grid=(n_tiles,),
        in_specs=_combine_specs(tm, n_tiles) + _proj_specs(w_in, conv_w, tm) + [pl.BlockSpec(memory_space=pl.ANY)],
        out_specs=[pl.BlockSpec((tm, P_COLS), row), pl.BlockSpec((tm, D_MODEL), row)],
        out_shape=[jax.ShapeDtypeStruct((t, P_COLS), BF16), jax.ShapeDtypeStruct((t, D_MODEL), F32)],
        scratch_shapes=_combine_scratch(tm) + [pltpu.VMEM((8, D_MODEL), F32)],
        compiler_params=pltpu.CompilerParams(dimension_semantics=("arbitrary",), vmem_limit_bytes=VMEM_BIG),
        name="combine_proj",
    )(dest, dest, route, h2, g1, w_in, b_gate, rot, conv_w, ys)


def _rotary_table(positions):
    half = ROT_DIM // 2
    inv = ROPE_THETA ** (-jnp.arange(0, ROT_DIM, 2, dtype=F32) / ROT_DIM)
    ang = positions.reshape(-1).astype(F32)[:, None] * inv
    cs1 = jnp.concatenate([jnp.cos(ang), jnp.sin(ang), jnp.ones((ang.shape[0], 1), F32)], axis=1)
    lane = jnp.arange(LANES) % HEAD_DIM
    j = jnp.arange(2 * half + 1)[:, None]
    a = jnp.where(lane < half, j == lane, jnp.where(lane < ROT_DIM, j == lane - half, j == 2 * half))
    b = -((lane < half) & (j == half + lane)).astype(F32)
    c = ((lane >= half) & (lane < ROT_DIM) & (j == lane)).astype(F32)
    place = jnp.concatenate([a.astype(F32), b, c], axis=1)
    return jnp.dot(cs1, place, precision=lax.Precision.HIGHEST)


def _sink_table(sinks):
    s = sinks.astype(F32).reshape(N_KV_HEADS, 4, 2).transpose(0, 2, 1)
    s = jnp.broadcast_to(s[..., None], (N_KV_HEADS, 2, 4, CHUNK)).reshape(2 * N_KV_HEADS, 4 * CHUNK)
    return jnp.concatenate([s, jnp.zeros((8 - 2 * N_KV_HEADS, 4 * CHUNK), F32)], axis=0)


def _router_params(w_group, b_group, w_route, b_route):
    d = w_group.shape[0]
    pad = LANES - N_GROUPS - N_EXPERTS
    wt = jnp.concatenate([w_group.T, w_route.transpose(0, 2, 1).reshape(N_EXPERTS, d), jnp.zeros((pad, d), F32)], axis=0)
    hi = wt.astype(BF16)
    lo = (wt - hi.astype(F32)).astype(BF16)
    bias = jnp.concatenate([b_group, b_route.reshape(-1), jnp.zeros((pad,), F32)])
    return jnp.concatenate([hi, lo], axis=0), jnp.broadcast_to(bias[:, None], (LANES, LANES))


def _moe_plan(route_t, counts, n_tok, tm):
    n_rows = -(-(2 * n_tok) // MOE_BLOCK) * MOE_BLOCK + N_EXPERTS * MOE_BLOCK
    n_blocks = n_rows // MOE_BLOCK
    ids = jnp.arange(N_EXPERTS, dtype=jnp.int32)
    cnt = counts[R_ROW0:R_ROW0 + N_EXPERTS, 0].astype(jnp.int32)
    padded = (cnt + MOE_BLOCK - 1) // MOE_BLOCK * MOE_BLOCK
    pad_end = jnp.sum(jnp.where(ids[None, :] <= ids[:, None], padded[None, :], 0), axis=1)
    pad_start = pad_end - padded
    e = route_t[R_E0:R_E1 + 1].astype(jnp.int32)
    rank = route_t[R_RANK0:R_RANK1 + 1].astype(jnp.int32)
    dest = jnp.sum(jnp.where(e[..., None] == ids, pad_start, 0), axis=-1) + rank
    dest = dest.reshape(2, n_tok // tm, tm).transpose(1, 0, 2).reshape(-1)
    n_used = pad_end[-1] // MOE_BLOCK
    blk = jnp.arange(n_blocks, dtype=jnp.int32)
    be = jnp.minimum(jnp.sum((pad_end[None, :] <= blk[:, None] * MOE_BLOCK).astype(jnp.int32), axis=1), N_EXPERTS - 1)
    be_last = jnp.sum(jnp.where(blk == n_used - 1, be, 0))
    be = jnp.where(blk < n_used, be, be_last)
    new_expert = jnp.concatenate([jnp.ones((1,), jnp.int32), (be[1:] != be[:-1]).astype(jnp.int32)])
    later = blk[None, :] > blk[:, None]
    seq_slot = (jnp.sum(jnp.where(later.T | (blk[None, :] == blk[:, None]), new_expert[None, :], 0), axis=1) - 1) % 2
    next_start = jnp.min(jnp.where(later & (new_expert[None, :] == 1), blk[None, :], n_blocks), axis=1)
    next_expert = jnp.sum(jnp.where(blk[None, :] == next_start[:, None], be[None, :], 0), axis=1)
    next_expert = jnp.where(next_start < n_blocks, next_expert, -1)
    last = pad_end // MOE_BLOCK - 1
    is_last = jnp.any((blk[:, None] == last[None, :]) & (padded[None, :] > 0), axis=1)
    zero_mask = (is_last | (blk >= n_used)).astype(jnp.int32)
    plan = (be, new_expert, next_expert.astype(jnp.int32), seq_slot.astype(jnp.int32), n_used.astype(jnp.int32).reshape(1))
    return n_rows, dest, plan, zero_mask


def kernel(x, positions, norm1_g, w_in, b_gate, conv_w, sinks, w_conv_out, w_attn_out, w_o, norm2_g, w_group, b_group, w_route, b_route, w1, w3, w2, final_g):
    b, s, d = x.shape
    t = b * s
    depth = w_in.shape[0]
    tm = 512
    rot = _rotary_table(positions)
    h = x.reshape(t, d)
    ys = dest = route = None
    for l in range(depth):
        w_in_l = w_in[l].astype(BF16)
        if l == 0:
            p = _proj(h, norm1_g[l][None], w_in_l, b_gate[l][None], rot, conv_w[l], s, tm)
        else:
            p, h = _combine_proj(ys, dest, route, h, norm1_g[l][None], w_in_l, b_gate[l][None], rot, conv_w[l], s, tm)
        g2 = norm2_g[l][None]
        wt_router, bt_router = _router_params(w_group[l], b_group[l], w_route[l], b_route[l])
        h, route, route_t, counts, hp = _mix(p, h, _sink_table(sinks[l]), w_conv_out[l].astype(BF16),
                                    w_attn_out[l].astype(BF16), w_o[l].astype(BF16), g2, wt_router, bt_router, s, tm)
        n_rows, dest, plan, zero_mask = _moe_plan(route_t, counts, t, tm)
        xs = _dispatch(hp, dest, zero_mask, n_rows, tm)
        ys = _experts(xs, plan, w1, w3, w2, l)
    return _combine_final(ys, dest, route, h, final_g[None], tm).reshape(b, s, d)
```

```python
import functools

import jax
import jax.numpy as jnp
from jax import lax
from jax.experimental import pallas as pl
from jax.experimental.pallas import tpu as pltpu

F32 = jnp.float32
BF16 = jnp.bfloat16
U32 = jnp.uint32

D_MODEL = 1024
HALF = D_MODEL // 2
RMS_EPS = 1e-5
CHUNK = 64
WINDOW_CHUNKS = 2
HEAD_DIM = 64
N_Q_HEADS = 16
N_KV_HEADS = 2
ROT_DIM = 16
ROPE_THETA = 500000.0
N_GROUPS = 8
EXPERTS_PER_GROUP = 8
N_EXPERTS = 64
EXPERT_FF = 512
MOE_BLOCK = 256
STEP_BLOCKS = 2
LANES = 128

W_CB, W_CC, W_CX, W_Q, W_K, W_G = 0, 1024, 2048, 3072, 4096, 4352
P_CI, P_Q, P_KV, P_G = 0, 1024, 2048, 2304
P_COLS = 4352
HALO = WINDOW_CHUNKS * CHUNK
KEYS = HALO + CHUNK
SCORES_AHEAD = 4

VMEM_BIG = 56 * 1024 * 1024
NT = (((1,), (1,)), ((), ()))


def _rms(x, g):
    ms = jnp.mean(x * x, axis=-1, keepdims=True)
    return x * lax.rsqrt(ms + RMS_EPS) * g


def _pack_pair(lo, hi):
    def rnd(x):
        return lax.bitcast_convert_type(x, U32) + U32(0x8000)

    return (rnd(hi) & U32(0xFFFF0000)) | (rnd(lo) >> 16)


def _unpack_pair(w):
    lo = lax.bitcast_convert_type(w << 16, F32)
    hi = lax.bitcast_convert_type(w & U32(0xFFFF0000), F32)
    return lo, hi


ROW_WORDS = HALF // LANES


def _store_rows(ref, lead, n, x):
    for j in range(ROW_WORDS):
        w = _pack_pair(x[:, j * LANES:(j + 1) * LANES], x[:, HALF + j * LANES:HALF + (j + 1) * LANES])
        ref[lead + (pl.ds(j, n, stride=ROW_WORDS), slice(None))] = w


def _load_rows(ref, lead, row0, n):
    parts = [_unpack_pair(ref[lead + (pl.ds(row0 * ROW_WORDS + j, n, stride=ROW_WORDS), slice(None))])
             for j in range(ROW_WORDS)]
    return jnp.concatenate([p[0] for p in parts] + [p[1] for p in parts], axis=1)


def _proj_kernel(tiles_per_seq, x_ref, g_ref, w_ref, bg_ref, rot_ref, cw_ref, o_ref, carry_s):
    first = (pl.program_id(0) % tiles_per_seq) == 0

    @pl.when(pl.program_id(0) == 0)
    def _():
        carry_s[...] = jnp.zeros_like(carry_s)

    _proj_body(_rms(x_ref[...], g_ref[...]).astype(BF16), first, True, w_ref, bg_ref, rot_ref, cw_ref, o_ref, carry_s)


def _proj_body(xn, first, gates_first, w_ref, bg_ref, rot_ref, cw_ref, o_ref, carry_s):
    tm = xn.shape[0]

    def mm(c0, n):
        return jnp.dot(xn, w_ref[:, c0:c0 + n], preferred_element_type=F32)

    ra = rot_ref[:, 0:LANES]
    rb = rot_ref[:, LANES:2 * LANES]
    rc = rot_ref[:, 2 * LANES:3 * LANES]

    def rot(t):
        return t * ra + pltpu.roll(t, LANES - ROT_DIM // 2, 1) * rb + pltpu.roll(t, ROT_DIM // 2, 1) * rc

    nc = 512
    row = lax.broadcasted_iota(jnp.int32, (tm, 1), 0)
    cw = cw_ref[...]

    def gates():
        for j in range(4):
            gt = mm(W_G + j * nc, nc) + bg_ref[:, j * nc:(j + 1) * nc]
            o_ref[:, P_G + j * nc:P_G + (j + 1) * nc] = jax.nn.sigmoid(gt).astype(BF16)

    def queries():
        for j in range(2):
            q = mm(W_Q + j * nc, nc)
            for p in range(nc // LANES):
                t = rot(q[:, p * LANES:(p + 1) * LANES]) * (HEAD_DIM ** -0.5)
                o_ref[:, P_Q + j * nc + p * LANES:P_Q + j * nc + (p + 1) * LANES] = t.astype(BF16)

    def conv():
        for j in range(2):
            cs = slice(j * nc, (j + 1) * nc)
            u = mm(W_CB + j * nc, nc) * mm(W_CX + j * nc, nc)
            prev = carry_s[:, cs]
            h1 = jnp.where(first, 0.0, prev[7:8])
            h2 = jnp.where(first, 0.0, prev[6:7])
            s1 = jnp.where(row == 0, h1, pltpu.roll(u, 1, 0))
            s2 = jnp.where(row == 0, h2, jnp.where(row == 1, h1, pltpu.roll(u, 2, 0)))
            y = cw[2:3, cs] * u + cw[1:2, cs] * s1 + cw[0:1, cs] * s2
            carry_s[:, cs] = u[tm - 8:tm]
            o_ref[:, P_CI + j * nc:P_CI + (j + 1) * nc] = (mm(W_CC + j * nc, nc) * y).astype(BF16)

    def keys_values():
        kv = mm(W_K, 2 * LANES)
        o_ref[:, P_KV:P_KV + LANES] = rot(kv[:, :LANES]).astype(BF16)
        o_ref[:, P_KV + LANES:P_KV + 2 * LANES] = kv[:, LANES:].astype(BF16)

    for group in (gates, queries, conv, keys_values) if gates_first else (conv, queries, keys_values, gates):
        group()


def _proj_specs(w_in, conv_w, tm):
    fixed = lambda i: (0, 0)
    return [
        pl.BlockSpec((1, D_MODEL), fixed),
        pl.BlockSpec(w_in.shape, fixed, pipeline_mode=pl.Buffered(1)),
        pl.BlockSpec((1, 2 * D_MODEL), fixed),
        pl.BlockSpec((tm, 3 * LANES), lambda i: (i, 0)),
        pl.BlockSpec(conv_w.shape, fixed),
    ]


def _proj(x2, g1, w_in, b_gate, rot, conv_w, seq, tm):
    t = x2.shape[0]
    row = lambda i: (i, 0)
    return pl.pallas_call(
        functools.partial(_proj_kernel, seq // tm),
        grid=(t // tm,),
        in_specs=[pl.BlockSpec((tm, D_MODEL), row)] + _proj_specs(w_in, conv_w, tm),
        out_specs=pl.BlockSpec((tm, P_COLS), row),
        out_shape=jax.ShapeDtypeStruct((t, P_COLS), BF16),
        scratch_shapes=[pltpu.VMEM((8, D_MODEL), F32)],
        compiler_params=pltpu.CompilerParams(dimension_semantics=("arbitrary",), vmem_limit_bytes=VMEM_BIG),
        name="proj",
    )(x2, g1, w_in, b_gate, rot, conv_w)


R_E0, R_E1, R_RANK0, R_RANK1, R_W0, R_W1 = range(6)
R_ROW0 = N_GROUPS


def _route_tile(tm, h, g_ref, wt_ref, bt_ref, r_ref, rt_ref, cnt_ref, base_s):
    hn = _rms(h, g_ref[...])
    hi = hn.astype(BF16)
    lo = (hn - hi.astype(F32)).astype(BF16)
    both = lax.dot_general(wt_ref[...], hi, NT, preferred_element_type=F32)
    lt = (both[:LANES] + both[LANES:] + lax.dot_general(wt_ref[0:LANES, :], lo, NT, preferred_element_type=F32)
          + bt_ref[:, 0:1])
    row = lax.broadcasted_iota(jnp.int32, (LANES, tm), 0)
    neg = -jnp.inf

    def first_max(v):
        m = jnp.max(v, axis=0, keepdims=True)
        return m, jnp.min(jnp.where(v == m, row, LANES), axis=0, keepdims=True)

    gl = jnp.where(row < N_GROUPS, lt, neg)
    gmax, gtop = first_max(gl)
    pg = 1.0 / jnp.sum(jnp.exp(gl - gmax), axis=0, keepdims=True)
    in_group = (row >= R_ROW0) & (row < R_ROW0 + N_EXPERTS) & (((row - R_ROW0) >> 3) == gtop)
    el = jnp.where(in_group, lt, neg)
    m1, i1 = first_max(el)
    m2, i2 = first_max(jnp.where(row == i1, neg, el))
    e2 = jnp.exp(m2 - m1)
    den = 1.0 + e2
    w0 = pg * (1.0 / den)
    w1 = pg * (e2 / den)
    hit0 = row == i1
    hit1 = row == i2
    a = jnp.where(hit0 | hit1, 1.0, 0.0)
    rr = lax.broadcasted_iota(jnp.int32, (tm, tm), 0)
    cc = lax.broadcasted_iota(jnp.int32, (tm, tm), 1)
    tri = jnp.where(rr < cc, 1.0, 0.0).astype(BF16)
    before = jnp.dot(a.astype(BF16), tri, preferred_element_type=F32) + base_s[:, 0:1]
    rank0 = jnp.sum(jnp.where(hit0, before, 0.0), axis=0, keepdims=True)
    rank1 = jnp.sum(jnp.where(hit1, before, 0.0), axis=0, keepdims=True)
    base_s[...] = base_s[...] + jnp.sum(a, axis=1, keepdims=True)
    cnt_ref[...] = base_s[...]
    vals = ((i1 - R_ROW0).astype(F32), (i2 - R_ROW0).astype(F32), rank0, rank1, w0, w1)
    slab = jnp.zeros((LANES, tm), F32)
    for k, v in enumerate(vals):
        slab = jnp.where(row == k, v, slab)
    r_ref[...] = slab.T
    rt_ref[...] = slab[0:8]
    return hn


def _mix_kernel(tm, tiles_per_seq, p_ref, hkv_ref, x_ref, sink_ref, wco_ref, wao_ref, wo_ref,
                g2_ref, wt_ref, bt_ref, o_ref, r_ref, rt_ref, cnt_ref, hp_ref, ke_s, ko_s, vta_s, vtb_s, attn_s, base_s):
    i = pl.program_id(0)
    first = (i % tiles_per_seq) == 0
    lo = lax.broadcasted_iota(jnp.int32, (1, LANES), 1) < HEAD_DIM

    @pl.when(i == 0)
    def _():
        base_s[...] = jnp.zeros_like(base_s)

    kv_all = jnp.concatenate([hkv_ref[...], p_ref[:, P_KV:P_KV + 2 * LANES]], axis=0)
    k = kv_all[:, :LANES].astype(F32)
    kr = pltpu.roll(k, HEAD_DIM, 1)
    ke_s[0] = jnp.where(lo, k, 0.0).astype(BF16)
    ko_s[0] = jnp.where(lo, 0.0, kr).astype(BF16)
    ke_s[1] = jnp.where(lo, kr, 0.0).astype(BF16)
    ko_s[1] = jnp.where(lo, 0.0, k).astype(BF16)
    v = kv_all[:, LANES:].astype(F32)
    n_keys = HALO + tm
    for j in range(n_keys // LANES):
        vta_s[:, j * LANES:(j + 1) * LANES] = v[j * LANES:(j + 1) * LANES].T.astype(BF16)
    for j in range(n_keys // LANES - 1):
        vtb_s[:, j * LANES:(j + 1) * LANES] = v[CHUNK + j * LANES:CHUNK + (j + 1) * LANES].T.astype(BF16)
    vtb_s[:, n_keys - LANES:n_keys - CHUNK] = v[n_keys - CHUNK:n_keys].T.astype(BF16)

    krow = lax.broadcasted_iota(jnp.int32, (2 * KEYS, 1), 0)
    krow = jnp.where(krow >= KEYS, krow - KEYS, krow)

    def scores(c, g):
        r0 = c * CHUNK
        qbase = P_Q + g * 4 * LANES
        q = jnp.concatenate(
            [p_ref[r0:r0 + CHUNK, qbase + p * LANES:qbase + (p + 1) * LANES] for p in range(4)], axis=0)
        kk = jnp.concatenate([ke_s[g, r0:r0 + KEYS, :], ko_s[g, r0:r0 + KEYS, :]], axis=0)
        st = lax.dot_general(kk, q, NT, preferred_element_type=F32)
        if r0 < HALO:
            nbad = jnp.where(first, HALO - r0, 0)
            st = jnp.where(krow < nbad, -jnp.inf, st)
        return st

    def finish(c, g, st):
        r0 = c * CHUNK
        vt_s, v0 = (vta_s, r0) if c % 2 == 0 else (vtb_s, r0 - CHUNK)
        vt = vt_s[g * HEAD_DIM:(g + 1) * HEAD_DIM, v0:v0 + KEYS]
        outs = []
        for par in range(2):
            s = st[par * KEYS:(par + 1) * KEYS]
            sc = sink_ref[2 * g + par:2 * g + par + 1, :]
            m = jnp.maximum(jnp.max(s, axis=0, keepdims=True), sc)
            p = jnp.exp(s - m)
            l = jnp.sum(p, axis=0, keepdims=True) + jnp.exp(sc - m)
            outs.append(jnp.dot(vt, p.astype(BF16), preferred_element_type=F32) / l)
        ot = jnp.concatenate(outs, axis=0)
        for p2 in range(2):
            blk = ot[:, p2 * LANES:(p2 + 1) * LANES].T
            for h in range(2):
                c0 = (g * 4 + 2 * p2 + h) * LANES
                attn_s[r0:r0 + CHUNK, c0:c0 + LANES] = blk[h * CHUNK:(h + 1) * CHUNK].astype(BF16)

    units = [(c, g) for c in range(tm // CHUNK) for g in range(N_KV_HEADS)]
    pending = [scores(*u) for u in units[:SCORES_AHEAD]]
    for n, (c, g) in enumerate(units):
        st = pending.pop(0)
        if n + SCORES_AHEAD < len(units):
            pending.append(scores(*units[n + SCORES_AHEAD]))
        finish(c, g, st)

    conv_out = jnp.dot(p_ref[:, P_CI:P_CI + D_MODEL], wco_ref[...], preferred_element_type=F32)
    attn_out = jnp.dot(attn_s[...], wao_ref[...], preferred_element_type=F32)
    gc = p_ref[:, P_G:P_G + D_MODEL].astype(F32)
    ga = p_ref[:, P_G + D_MODEL:P_G + 2 * D_MODEL].astype(F32)
    merged = (gc * conv_out + ga * attn_out).astype(BF16)
    h_new = x_ref[...] + jnp.dot(merged, wo_ref[...], preferred_element_type=F32)
    o_ref[...] = h_new
    hn = _route_tile(tm, h_new, g2_ref, wt_ref, bt_ref, r_ref, rt_ref, cnt_ref, base_s)
    _store_rows(hp_ref, (), tm, hn)


def _mix(p, x2, sink_tab, w_co, w_ao, w_o, g2, wt_router, bt_router, seq, tm):
    t = x2.shape[0]
    row = lambda i: (i, 0)
    fixed = lambda i: (0, 0)
    sq = (D_MODEL, D_MODEL)
    n_keys = HALO + tm
    return pl.pallas_call(
        functools.partial(_mix_kernel, tm, seq // tm),
        grid=(t // tm,),
        in_specs=[
            pl.BlockSpec((tm, P_COLS), row),
            pl.BlockSpec((HALO, 2 * LANES), lambda i: (jnp.maximum(i * (tm // HALO) - 1, 0), P_KV // (2 * LANES))),
            pl.BlockSpec((tm, D_MODEL), row),
            pl.BlockSpec(sink_tab.shape, fixed),
            pl.BlockSpec(sq, fixed),
            pl.BlockSpec(sq, fixed),
            pl.BlockSpec(sq, fixed),
            pl.BlockSpec((1, D_MODEL), fixed),
            pl.BlockSpec((2 * LANES, D_MODEL), fixed),
            pl.BlockSpec((LANES, LANES), fixed),
        ],
        out_specs=[
            pl.BlockSpec((tm, D_MODEL), row),
            pl.BlockSpec((tm, LANES), row),
            pl.BlockSpec((8, tm), lambda i: (0, i)),
            pl.BlockSpec((LANES, LANES), fixed),
            pl.BlockSpec((tm * ROW_WORDS, LANES), row),
        ],
        out_shape=[
            jax.ShapeDtypeStruct((t, D_MODEL), F32),
            jax.ShapeDtypeStruct((t, LANES), F32),
            jax.ShapeDtypeStruct((8, t), F32),
            jax.ShapeDtypeStruct((LANES, LANES), F32),
            jax.ShapeDtypeStruct((t * ROW_WORDS, LANES), U32),
        ],
        scratch_shapes=[
            pltpu.VMEM((N_KV_HEADS, n_keys, LANES), BF16),
            pltpu.VMEM((N_KV_HEADS, n_keys, LANES), BF16),
            pltpu.VMEM((2 * HEAD_DIM, n_keys), BF16),
            pltpu.VMEM((2 * HEAD_DIM, n_keys), BF16),
            pltpu.VMEM((tm, D_MODEL), BF16),
            pltpu.VMEM((LANES, LANES), F32),
        ],
        compiler_params=pltpu.CompilerParams(dimension_semantics=("arbitrary",), vmem_limit_bytes=VMEM_BIG),
        name="mix",
    )(p, p, x2, sink_tab, w_co, w_ao, w_o, g2, wt_router, bt_router)


def _rows_copy(src_ref, dst_ref, sem, n):
    return pltpu.make_async_copy(src_ref.at[pl.ds(0, n)], dst_ref.at[pl.ds(0, n)], sem)


def _dispatch_kernel(tm, n_tiles, n_blocks, zm_ref, d_ref, hp_ref, xs_ref, hn_s, zbuf, zsem, sem):
    i = pl.program_id(0)
    blk_rows = MOE_BLOCK * ROW_WORDS

    @pl.when(i == 0)
    def _():
        zbuf[...] = jnp.zeros_like(zbuf)

        def zero_copy(b):
            r = pl.multiple_of(b * blk_rows, blk_rows)
            return pltpu.make_async_copy(zbuf, xs_ref.at[pl.ds(r, blk_rows)], zsem)

        def start(b, c):
            @pl.when(zm_ref[b] == 1)
            def _():
                zero_copy(b).start()
            return c

        def wait(b, c):
            @pl.when(zm_ref[b] == 1)
            def _():
                zero_copy(b).wait()
            return c

        lax.fori_loop(0, n_blocks, start, 0)
        lax.fori_loop(0, n_blocks, wait, 0)

    slot = i % 2
    hn_s[slot] = hp_ref[...]

    def body(t, c):
        src = hn_s.at[slot, pl.ds(pl.multiple_of(t * ROW_WORDS, ROW_WORDS), ROW_WORDS)]
        for k in range(2):
            d = pl.multiple_of(d_ref[k * tm + t] * ROW_WORDS, ROW_WORDS)
            pltpu.make_async_copy(src, xs_ref.at[pl.ds(d, ROW_WORDS)], sem.at[slot]).start(priority=k)
        return c

    lax.fori_loop(0, tm, body, 0, unroll=8)

    def drain(s):
        for _ in range(2):
            _rows_copy(hn_s.at[s], xs_ref, sem.at[s], tm * ROW_WORDS).wait()

    @pl.when(i > 0)
    def _():
        drain(1 - slot)

    @pl.when(i == n_tiles - 1)
    def _():
        drain(slot)


def _dispatch(hp, dest, zero_mask, n_rows, tm):
    n_tiles = hp.shape[0] // (tm * ROW_WORDS)
    n_blocks = n_rows // MOE_BLOCK
    return pl.pallas_call(
        functools.partial(_dispatch_kernel, tm, n_tiles, n_blocks),
        grid_spec=pltpu.PrefetchScalarGridSpec(
            num_scalar_prefetch=1,
            grid=(n_tiles,),
            in_specs=[
                pl.BlockSpec((2 * tm,), lambda i, zm: (i,), memory_space=pltpu.SMEM),
                pl.BlockSpec((tm * ROW_WORDS, LANES), lambda i, zm: (i, 0)),
            ],
            out_specs=pl.BlockSpec(memory_space=pl.ANY),
            scratch_shapes=[
                pltpu.VMEM((2, tm * ROW_WORDS, LANES), U32),
                pltpu.VMEM((MOE_BLOCK * ROW_WORDS, LANES), U32),
                pltpu.SemaphoreType.DMA(()),
                pltpu.SemaphoreType.DMA((2,)),
            ],
        ),
        out_shape=jax.ShapeDtypeStruct((n_rows * ROW_WORDS, LANES), U32),
        compiler_params=pltpu.CompilerParams(dimension_semantics=("arbitrary",)),
        name="dispatch",
    )(zero_mask, dest, hp)


def _expert_kernel(layer, be_ref, nw_ref, nx_ref, sl_ref, nu_ref, x_ref, w1_hbm, w3_hbm, w2_hbm, o_ref,
                   w1_f, w3_f, w2_f, w1_s, w3_s, w2_s, sem):
    b0 = STEP_BLOCKS * pl.program_id(0)
    n_used = nu_ref[0]

    def fetch(e, slot):
        return (pltpu.make_async_copy(w1_hbm.at[layer, e], w1_f.at[slot], sem.at[slot]),
                pltpu.make_async_copy(w3_hbm.at[layer, e], w3_f.at[slot], sem.at[slot]),
                pltpu.make_async_copy(w2_hbm.at[layer, e], w2_f.at[slot], sem.at[slot]))

    def switch(b):
        @pl.when((b < n_used) & (nw_ref[b] == 1))
        def _():
            slot = sl_ref[b]

            @pl.when(b == 0)
            def _():
                for c in fetch(be_ref[0], 0):
                    c.start()

            for c in fetch(be_ref[b], slot):
                c.wait()
            w1_s[slot] = w1_f[slot].astype(BF16)
            w3_s[slot] = w3_f[slot].astype(BF16)
            w2_s[slot] = w2_f[slot].astype(BF16)

            @pl.when(nx_ref[b] >= 0)
            def _():
                for c in fetch(nx_ref[b], 1 - slot):
                    c.start()

    @pl.when(b0 < n_used)
    def _():
        for k in range(STEP_BLOCKS):
            switch(b0 + k)
        nb = STEP_BLOCKS
        slots = [sl_ref[b0 + k] for k in range(nb)]
        xs = [_load_rows(x_ref, (), k * MOE_BLOCK, MOE_BLOCK).astype(BF16) for k in range(nb)]
        h1 = [jnp.dot(xs[k], w1_s[slots[k]], preferred_element_type=F32) for k in range(nb)]
        h3 = [jnp.dot(xs[k], w3_s[slots[k]], preferred_element_type=F32) for k in range(nb)]
        for k in range(nb):
            hid = (h1[k] * jax.nn.sigmoid(h1[k]) * h3[k]).astype(BF16)
            y = jnp.dot(hid, w2_s[slots[k]], preferred_element_type=F32)
            for j in range(ROW_WORDS):
                w = _pack_pair(y[:, j * LANES:(j + 1) * LANES], y[:, HALF + j * LANES:HALF + (j + 1) * LANES])
                o_ref[pl.ds(k * MOE_BLOCK * ROW_WORDS + j, MOE_BLOCK, stride=ROW_WORDS), :] = w

    @pl.when(b0 >= n_used)
    def _():
        o_ref[...] = jnp.zeros_like(o_ref)


def _experts(xs, plan, w1, w3, w2, layer):
    n_rows = xs.shape[0] // ROW_WORDS
    blk = (STEP_BLOCKS * MOE_BLOCK * ROW_WORDS, LANES)
    any_spec = pl.BlockSpec(memory_space=pl.ANY)
    return pl.pallas_call(
        functools.partial(_expert_kernel, layer),
        grid_spec=pltpu.PrefetchScalarGridSpec(
            num_scalar_prefetch=5,
            grid=(n_rows // (STEP_BLOCKS * MOE_BLOCK),),
            in_specs=[
                pl.BlockSpec(blk, lambda p, be, nw, nx, sl, nu: (jnp.minimum(p, (nu[0] - 1) // STEP_BLOCKS), 0)),
                any_spec, any_spec, any_spec,
            ],
            out_specs=pl.BlockSpec(blk, lambda p, be, nw, nx, sl, nu: (p, 0)),
            scratch_shapes=[
                pltpu.VMEM((2, D_MODEL, EXPERT_FF), F32),
                pltpu.VMEM((2, D_MODEL, EXPERT_FF), F32),
                pltpu.VMEM((2, EXPERT_FF, D_MODEL), F32),
                pltpu.VMEM((2, D_MODEL, EXPERT_FF), BF16),
                pltpu.VMEM((2, D_MODEL, EXPERT_FF), BF16),
                pltpu.VMEM((2, EXPERT_FF, D_MODEL), BF16),
                pltpu.SemaphoreType.DMA((2,)),
            ],
        ),
        out_shape=jax.ShapeDtypeStruct(xs.shape, U32),
        compiler_params=pltpu.CompilerParams(dimension_semantics=("arbitrary",), vmem_limit_bytes=VMEM_BIG),
        name="experts",
    )(*plan, xs, w1, w3, w2)


def _issue_row_gathers(tm, d_ref, ys_ref, ybuf, sem, slot, unrolled):
    def one(t):
        for k in range(2):
            d = pl.multiple_of(d_ref[k * tm + t] * ROW_WORDS, ROW_WORDS)
            r = (k * tm + t) * ROW_WORDS
            if not isinstance(r, int):
                r = pl.multiple_of(r, ROW_WORDS)
            pltpu.make_async_copy(
                ys_ref.at[pl.ds(d, ROW_WORDS)], ybuf.at[slot, pl.ds(r, ROW_WORDS)], sem.at[slot]).start(priority=k)

    if unrolled:
        for t in range(tm):
            one(t)
    else:
        def body(t, c):
            one(t)
            return c

        lax.fori_loop(0, tm, body, 0, unroll=8)


def _combined_tile(tm, r_ref, h_ref, ys_ref, ybuf, sem, slot):
    _rows_copy(ys_ref, ybuf.at[slot], sem.at[slot], 2 * tm * ROW_WORDS).wait()
    r = r_ref[...]
    w0 = r[:, R_W0:R_W0 + 1]
    w1 = r[:, R_W1:R_W1 + 1]
    return h_ref[...] + (w0 * _load_rows(ybuf, (slot,), 0, tm) + w1 * _load_rows(ybuf, (slot,), tm, tm))


def _combine_kernel(tm, n_tiles, dcur_ref, dnext_ref, r_ref, h_ref, fg_ref, ys_ref, o_ref, ybuf, sem):
    i = pl.program_id(0)

    @pl.when(i == 0)
    def _():
        _issue_row_gathers(tm, dcur_ref, ys_ref, ybuf, sem, 0, False)

    @pl.when(i + 1 < n_tiles)
    def _():
        _issue_row_gathers(tm, dnext_ref, ys_ref, ybuf, sem, (i + 1) % 2, False)

    o_ref[...] = _rms(_combined_tile(tm, r_ref, h_ref, ys_ref, ybuf, sem, i % 2), fg_ref[...])


def _combine_specs(tm, n_tiles):
    row = lambda i: (i, 0)
    return [
        pl.BlockSpec((2 * tm,), lambda i: (i,), memory_space=pltpu.SMEM),
        pl.BlockSpec((2 * tm,), lambda i: (jnp.minimum(i + 1, n_tiles - 1),), memory_space=pltpu.SMEM),
        pl.BlockSpec((tm, LANES), row),
        pl.BlockSpec((tm, D_MODEL), row),
    ]


def _combine_scratch(tm):
    return [pltpu.VMEM((2, 2 * tm * ROW_WORDS, LANES), U32), pltpu.SemaphoreType.DMA((2,))]


def _combine_final(ys, dest, route, h2, final_g, tm):
    t = h2.shape[0]
    n_tiles = t // tm
    return pl.pallas_call(
        functools.partial(_combine_kernel, tm, n_tiles),
        grid=(n_tiles,),
        in_specs=_combine_specs(tm, n_tiles)
        + [pl.BlockSpec((1, D_MODEL), lambda i: (0, 0)), pl.BlockSpec(memory_space=pl.ANY)],
        out_specs=pl.BlockSpec((tm, D_MODEL), lambda i: (i, 0)),
        out_shape=jax.ShapeDtypeStruct((t, D_MODEL), F32),
        scratch_shapes=_combine_scratch(tm),
        compiler_params=pltpu.CompilerParams(dimension_semantics=("arbitrary",)),
        name="combine",
    )(dest, dest, route, h2, final_g, ys)


def _combine_proj_kernel(tm, n_tiles, tiles_per_seq, dcur_ref, dnext_ref, r_ref, hm_ref, g_ref, w_ref, bg_ref,
                         rot_ref, cw_ref, ys_ref, o_ref, h_ref, ybuf, sem, carry_s):
    i = pl.program_id(0)
    slot = i % 2
    first = (i % tiles_per_seq) == 0

    @pl.when(i == 0)
    def _():
        carry_s[...] = jnp.zeros_like(carry_s)
        _issue_row_gathers(tm, dcur_ref, ys_ref, ybuf, sem, 0, False)

    h = _combined_tile(tm, r_ref, hm_ref, ys_ref, ybuf, sem, slot)
    h_ref[...] = h
    _issue_row_gathers(tm, dnext_ref, ys_ref, ybuf, sem, 1 - slot, True)
    _proj_body(_rms(h, g_ref[...]).astype(BF16), first, False, w_ref, bg_ref, rot_ref, cw_ref, o_ref, carry_s)

    @pl.when(i == n_tiles - 1)
    def _():
        _rows_copy(ys_ref, ybuf.at[1 - slot], sem.at[1 - slot], 2 * tm * ROW_WORDS).wait()


def _combine_proj(ys, dest, route, h2, g1, w_in, b_gate, rot, conv_w, seq, tm):
    t = h2.shape[0]
    n_tiles = t // tm
    row = lambda i: (i, 0)
    return pl.pallas_call(
        functools.partial(_combine_proj_kernel, tm, n_tiles, seq // tm),
        grid=(n_tiles,),
        in_specs=_combine_specs(tm, n_tiles) + _proj_specs(w_in, conv_w, tm) + [pl.BlockSpec(memory_space=pl.ANY)],
        out_specs=[pl.BlockSpec((tm, P_COLS), row), pl.BlockSpec((tm, D_MODEL), row)],
        out_shape=[jax.ShapeDtypeStruct((t, P_COLS), BF16), jax.ShapeDtypeStruct((t, D_MODEL), F32)],
        scratch_shapes=_combine_scratch(tm) + [pltpu.VMEM((8, D_MODEL), F32)],
        compiler_params=pltpu.CompilerParams(dimension_semantics=("arbitrary",), vmem_limit_bytes=VMEM_BIG),
        name="combine_proj",
    )(dest, dest, route, h2, g1, w_in, b_gate, rot, conv_w, ys)


def _rotary_table(positions):
    half = ROT_DIM // 2
    inv = ROPE_THETA ** (-jnp.arange(0, ROT_DIM, 2, dtype=F32) / ROT_DIM)
    ang = positions.reshape(-1).astype(F32)[:, None] * inv
    cs1 = jnp.concatenate([jnp.cos(ang), jnp.sin(ang), jnp.ones((ang.shape[0], 1), F32)], axis=1)
    lane = jnp.arange(LANES) % HEAD_DIM
    j = jnp.arange(2 * half + 1)[:, None]
    a = jnp.where(lane < half, j == lane, jnp.where(lane < ROT_DIM, j == lane - half, j == 2 * half))
    b = -((lane < half) & (j == half + lane)).astype(F32)
    c = ((lane >= half) & (lane < ROT_DIM) & (j == lane)).astype(F32)
    place = jnp.concatenate([a.astype(F32), b, c], axis=1)
    return jnp.dot(cs1, place, precision=lax.Precision.HIGHEST)


def _sink_table(sinks):
    s = sinks.astype(F32).reshape(N_KV_HEADS, 4, 2).transpose(0, 2, 1)
    s = jnp.broadcast_to(s[..., None], (N_KV_HEADS, 2, 4, CHUNK)).reshape(2 * N_KV_HEADS, 4 * CHUNK)
    return jnp.concatenate([s, jnp.zeros((8 - 2 * N_KV_HEADS, 4 * CHUNK), F32)], axis=0)


def _router_params(w_group, b_group, w_route, b_route):
    d = w_group.shape[0]
    pad = LANES - N_GROUPS - N_EXPERTS
    wt = jnp.concatenate([w_group.T, w_route.transpose(0, 2, 1).reshape(N_EXPERTS, d), jnp.zeros((pad, d), F32)], axis=0)
    hi = wt.astype(BF16)
    lo = (wt - hi.astype(F32)).astype(BF16)
    bias = jnp.concatenate([b_group, b_route.reshape(-1), jnp.zeros((pad,), F32)])
    return jnp.concatenate([hi, lo], axis=0), jnp.broadcast_to(bias[:, None], (LANES, LANES))


def _moe_plan(route_t, counts, n_tok, tm):
    n_rows = -(-(2 * n_tok) // MOE_BLOCK) * MOE_BLOCK + N_EXPERTS * MOE_BLOCK
    n_blocks = n_rows // MOE_BLOCK
    ids = jnp.arange(N_EXPERTS, dtype=jnp.int32)
    cnt = counts[R_ROW0:R_ROW0 + N_EXPERTS, 0].astype(jnp.int32)
    padded = (cnt + MOE_BLOCK - 1) // MOE_BLOCK * MOE_BLOCK
    pad_end = jnp.sum(jnp.where(ids[None, :] <= ids[:, None], padded[None, :], 0), axis=1)
    pad_start = pad_end - padded
    e = route_t[R_E0:R_E1 + 1].astype(jnp.int32)
    rank = route_t[R_RANK0:R_RANK1 + 1].astype(jnp.int32)
    dest = jnp.sum(jnp.where(e[..., None] == ids, pad_start, 0), axis=-1) + rank
    dest = dest.reshape(2, n_tok // tm, tm).transpose(1, 0, 2).reshape(-1)
    n_used = pad_end[-1] // MOE_BLOCK
    blk = jnp.arange(n_blocks, dtype=jnp.int32)
    be = jnp.minimum(jnp.sum((pad_end[None, :] <= blk[:, None] * MOE_BLOCK).astype(jnp.int32), axis=1), N_EXPERTS - 1)
    be_last = jnp.sum(jnp.where(blk == n_used - 1, be, 0))
    be = jnp.where(blk < n_used, be, be_last)
    new_expert = jnp.concatenate([jnp.ones((1,), jnp.int32), (be[1:] != be[:-1]).astype(jnp.int32)])
    later = blk[None, :] > blk[:, None]
    seq_slot = (jnp.sum(jnp.where(later.T | (blk[None, :] == blk[:, None]), new_expert[None, :], 0), axis=1) - 1) % 2
    next_start = jnp.min(jnp.where(later & (new_expert[None, :] == 1), blk[None, :], n_blocks), axis=1)
    next_expert = jnp.sum(jnp.where(blk[None, :] == next_start[:, None], be[None, :], 0), axis=1)
    next_expert = jnp.where(next_start < n_blocks, next_expert, -1)
    last = pad_end // MOE_BLOCK - 1
    is_last = jnp.any((blk[:, None] == last[None, :]) & (padded[None, :] > 0), axis=1)
    zero_mask = (is_last | (blk >= n_used)).astype(jnp.int32)
    plan = (be, new_expert, next_expert.astype(jnp.int32), seq_slot.astype(jnp.int32), n_used.astype(jnp.int32).reshape(1))
    return n_rows, dest, plan, zero_mask


def kernel(x, positions, norm1_g, w_in, b_gate, conv_w, sinks, w_conv_out, w_attn_out, w_o, norm2_g, w_group, b_group, w_route, b_route, w1, w3, w2, final_g):
    b, s, d = x.shape
    t = b * s
    depth = w_in.shape[0]
    tm = 512
    rot = _rotary_table(positions)
    h = x.reshape(t, d)
    ys = dest = route = None
    for l in range(depth):
        w_in_l = w_in[l].astype(BF16)
        if l == 0:
            p = _proj(h, norm1_g[l][None], w_in_l, b_gate[l][None], rot, conv_w[l], s, tm)
        else:
            p, h = _combine_proj(ys, dest, route, h, norm1_g[l][None], w_in_l, b_gate[l][None], rot, conv_w[l], s, tm)
        g2 = norm2_g[l][None]
        wt_router, bt_router = _router_params(w_group[l], b_group[l], w_route[l], b_route[l])
        h, route, route_t, counts, hp = _mix(p, h, _sink_table(sinks[l]), w_conv_out[l].astype(BF16),
                                    w_attn_out[l].astype(BF16), w_o[l].astype(BF16), g2, wt_router, bt_router, s, tm)
        n_rows, dest, plan, zero_mask = _moe_plan(route_t, counts, t, tm)
        xs = _dispatch(hp, dest, zero_mask, n_rows, tm)
        ys = _experts(xs, plan, w1, w3, w2, l)
    return _combine_final(ys, dest, route, h, final_g[None], tm).reshape(b, s, d)
```

```python
import functools

import jax
import jax.numpy as jnp
from jax import lax
from jax.experimental import pallas as pl
from jax.experimental.pallas import tpu as pltpu

F32 = jnp.float32
BF16 = jnp.bfloat16
U32 = jnp.uint32

D_MODEL = 1024
HALF = D_MODEL // 2
RMS_EPS = 1e-5
CHUNK = 64
WINDOW_CHUNKS = 2
HEAD_DIM = 64
N_Q_HEADS = 16
N_KV_HEADS = 2
ROT_DIM = 16
ROPE_THETA = 500000.0
N_GROUPS = 8
EXPERTS_PER_GROUP = 8
N_EXPERTS = 64
EXPERT_FF = 512
MOE_BLOCK = 256
STEP_BLOCKS = 4
W_SLOTS = 4
LANES = 128

W_CB, W_CC, W_CX, W_Q, W_K, W_G = 0, 1024, 2048, 3072, 4096, 4352
P_CI, P_Q, P_KV, P_G = 0, 1024, 2048, 2304
P_COLS = 4352
HALO = WINDOW_CHUNKS * CHUNK
KEYS = HALO + CHUNK
SCORES_AHEAD = 4

ROW_TILE = 512
VMEM_BIG = 56 * 1024 * 1024
NT = (((1,), (1,)), ((), ()))


def _rms(x, g):
    ms = jnp.mean(x * x, axis=-1, keepdims=True)
    return x * lax.rsqrt(ms + RMS_EPS) * g


def _pack_pair(lo, hi):
    def rnd(x):
        return lax.bitcast_convert_type(x, U32) + U32(0x8000)

    return (rnd(hi) & U32(0xFFFF0000)) | (rnd(lo) >> 16)


def _unpack_pair(w):
    lo = lax.bitcast_convert_type(w << 16, F32)
    hi = lax.bitcast_convert_type(w & U32(0xFFFF0000), F32)
    return lo, hi


ROW_WORDS = HALF // LANES


def _store_rows(ref, lead, n, x):
    for j in range(ROW_WORDS):
        w = _pack_pair(x[:, j * LANES:(j + 1) * LANES], x[:, HALF + j * LANES:HALF + (j + 1) * LANES])
        ref[lead + (pl.ds(j, n, stride=ROW_WORDS), slice(None))] = w


def _load_rows(ref, lead, row0, n):
    parts = [_unpack_pair(ref[lead + (pl.ds(row0 * ROW_WORDS + j, n, stride=ROW_WORDS), slice(None))])
             for j in range(ROW_WORDS)]
    return jnp.concatenate([p[0] for p in parts] + [p[1] for p in parts], axis=1)


def _proj_kernel(tiles_per_seq, x_ref, g_ref, w_ref, bg_ref, rot_ref, cw_ref, o_ref, carry_s):
    first = (pl.program_id(0) % tiles_per_seq) == 0

    @pl.when(pl.program_id(0) == 0)
    def _():
        carry_s[...] = jnp.zeros_like(carry_s)

    _proj_body(_rms(x_ref[...], g_ref[...]).astype(BF16), first, True, w_ref, bg_ref, rot_ref, cw_ref, o_ref, carry_s)


def _proj_body(xn, first, gates_first, w_ref, bg_ref, rot_ref, cw_ref, o_ref, carry_s):
    tm = xn.shape[0]

    def mm(c0, n):
        return jnp.dot(xn, w_ref[:, c0:c0 + n], preferred_element_type=F32)

    ra = rot_ref[:, 0:LANES]
    rb = rot_ref[:, LANES:2 * LANES]
    rc = rot_ref[:, 2 * LANES:3 * LANES]

    def rot(t):
        return t * ra + pltpu.roll(t, LANES - ROT_DIM // 2, 1) * rb + pltpu.roll(t, ROT_DIM // 2, 1) * rc

    nc = 512
    row = lax.broadcasted_iota(jnp.int32, (tm, 1), 0)
    cw = cw_ref[...]

    def gates():
        for j in range(4):
            gt = mm(W_G + j * nc, nc) + bg_ref[:, j * nc:(j + 1) * nc]
            o_ref[:, P_G + j * nc:P_G + (j + 1) * nc] = jax.nn.sigmoid(gt).astype(BF16)

    def queries():
        for j in range(2):
            q = mm(W_Q + j * nc, nc)
            for p in range(nc // LANES):
                t = rot(q[:, p * LANES:(p + 1) * LANES]) * (HEAD_DIM ** -0.5)
                o_ref[:, P_Q + j * nc + p * LANES:P_Q + j * nc + (p + 1) * LANES] = t.astype(BF16)

    def conv():
        for j in range(2):
            cs = slice(j * nc, (j + 1) * nc)
            u = mm(W_CB + j * nc, nc) * mm(W_CX + j * nc, nc)
            prev = carry_s[:, cs]
            h1 = jnp.where(first, 0.0, prev[7:8])
            h2 = jnp.where(first, 0.0, prev[6:7])
            s1 = jnp.where(row == 0, h1, pltpu.roll(u, 1, 0))
            s2 = jnp.where(row == 0, h2, jnp.where(row == 1, h1, pltpu.roll(u, 2, 0)))
            y = cw[2:3, cs] * u + cw[1:2, cs] * s1 + cw[0:1, cs] * s2
            carry_s[:, cs] = u[tm - 8:tm]
            o_ref[:, P_CI + j * nc:P_CI + (j + 1) * nc] = (mm(W_CC + j * nc, nc) * y).astype(BF16)

    def keys_values():
        kv = mm(W_K, 2 * LANES)
        o_ref[:, P_KV:P_KV + LANES] = rot(kv[:, :LANES]).astype(BF16)
        o_ref[:, P_KV + LANES:P_KV + 2 * LANES] = kv[:, LANES:].astype(BF16)

    for group in (gates, queries, conv, keys_values) if gates_first else (conv, queries, keys_values, gates):
        group()


def _proj_specs(w_in, conv_w, tm):
    fixed = lambda i: (0, 0)
    return [
        pl.BlockSpec((1, D_MODEL), fixed),
        pl.BlockSpec(w_in.shape, fixed, pipeline_mode=pl.Buffered(1)),
        pl.BlockSpec((1, 2 * D_MODEL), fixed),
        pl.BlockSpec((tm, 3 * LANES), lambda i: (i, 0)),
        pl.BlockSpec(conv_w.shape, fixed),
    ]


def _proj(x2, g1, w_in, b_gate, rot, conv_w, seq, tm):
    t = x2.shape[0]
    row = lambda i: (i, 0)
    return pl.pallas_call(
        functools.partial(_proj_kernel, seq // tm),
        grid=(t // tm,),
        in_specs=[pl.BlockSpec((tm, D_MODEL), row)] + _proj_specs(w_in, conv_w, tm),
        out_specs=pl.BlockSpec((tm, P_COLS), row),
        out_shape=jax.ShapeDtypeStruct((t, P_COLS), BF16),
        scratch_shapes=[pltpu.VMEM((8, D_MODEL), F32)],
        compiler_params=pltpu.CompilerParams(dimension_semantics=("arbitrary",), vmem_limit_bytes=VMEM_BIG),
        name="proj",
    )(x2, g1, w_in, b_gate, rot, conv_w)


R_E0, R_E1, R_RANK0, R_RANK1, R_W0, R_W1 = range(6)
R_ROW0 = N_GROUPS


def _route_tile(tm, h, g_ref, wt_ref, bt_ref, tri_ref, r_ref, rt_ref, cnt_ref, base_s):
    hn = _rms(h, g_ref[...])
    hi = hn.astype(BF16)
    lo = (hn - hi.astype(F32)).astype(BF16)
    both = lax.dot_general(wt_ref[...], hi, NT, preferred_element_type=F32)
    lt = (both[:LANES] + both[LANES:] + lax.dot_general(wt_ref[0:LANES, :], lo, NT, preferred_element_type=F32)
          + bt_ref[:, 0:1])
    row = lax.broadcasted_iota(jnp.int32, (LANES, tm), 0)
    neg = -jnp.inf

    def first_max(v):
        m = jnp.max(v, axis=0, keepdims=True)
        return m, jnp.min(jnp.where(v == m, row, LANES), axis=0, keepdims=True)

    gl = jnp.where(row < N_GROUPS, lt, neg)
    gmax, gtop = first_max(gl)
    pg = 1.0 / jnp.sum(jnp.exp(gl - gmax), axis=0, keepdims=True)
    in_group = (row >= R_ROW0) & (row < R_ROW0 + N_EXPERTS) & (((row - R_ROW0) >> 3) == gtop)
    el = jnp.where(in_group, lt, neg)
    m1, i1 = first_max(el)
    m2, i2 = first_max(jnp.where(row == i1, neg, el))
    e2 = jnp.exp(m2 - m1)
    den = 1.0 + e2
    w0 = pg * (1.0 / den)
    w1 = pg * (e2 / den)
    hit0 = row == i1
    hit1 = row == i2
    a = jnp.where(hit0 | hit1, 1.0, 0.0)
    before = jnp.dot(a.astype(BF16), tri_ref[...], preferred_element_type=F32) + base_s[:, 0:1]
    rank0 = jnp.sum(jnp.where(hit0, before, 0.0), axis=0, keepdims=True)
    rank1 = jnp.sum(jnp.where(hit1, before, 0.0), axis=0, keepdims=True)
    base_s[...] = base_s[...] + jnp.sum(a, axis=1, keepdims=True)
    cnt_ref[...] = base_s[...]
    vals = ((i1 - R_ROW0).astype(F32), (i2 - R_ROW0).astype(F32), rank0, rank1, w0, w1)
    slab = jnp.zeros((LANES, tm), F32)
    for k, v in enumerate(vals):
        slab = jnp.where(row == k, v, slab)
    r_ref[...] = slab.T
    rt_ref[...] = slab[0:8]
    return hn


def _mix_kernel(tm, tiles_per_seq, p_ref, hkv_ref, x_ref, sink_ref, wco_ref, wao_ref, wo_ref,
                g2_ref, wt_ref, bt_ref, tri_ref, o_ref, r_ref, rt_ref, cnt_ref, hp_ref, ke_s, ko_s, vta_s, vtb_s, attn_s,
                base_s):
    i = pl.program_id(0)
    first = (i % tiles_per_seq) == 0
    lo = lax.broadcasted_iota(jnp.int32, (1, LANES), 1) < HEAD_DIM

    @pl.when(i == 0)
    def _():
        base_s[...] = jnp.zeros_like(base_s)

    kv_all = jnp.concatenate([hkv_ref[...], p_ref[:, P_KV:P_KV + 2 * LANES]], axis=0)
    k = kv_all[:, :LANES].astype(F32)
    kr = pltpu.roll(k, HEAD_DIM, 1)
    ke_s[0] = jnp.where(lo, k, 0.0).astype(BF16)
    ko_s[0] = jnp.where(lo, 0.0, kr).astype(BF16)
    ke_s[1] = jnp.where(lo, kr, 0.0).astype(BF16)
    ko_s[1] = jnp.where(lo, 0.0, k).astype(BF16)
    v = kv_all[:, LANES:].astype(F32)
    n_keys = HALO + tm
    for j in range(n_keys // LANES):
        vta_s[:, j * LANES:(j + 1) * LANES] = v[j * LANES:(j + 1) * LANES].T.astype(BF16)
    for j in range(n_keys // LANES - 1):
        vtb_s[:, j * LANES:(j + 1) * LANES] = v[CHUNK + j * LANES:CHUNK + (j + 1) * LANES].T.astype(BF16)
    vtb_s[:, n_keys - LANES:n_keys - CHUNK] = v[n_keys - CHUNK:n_keys].T.astype(BF16)

    krow = lax.broadcasted_iota(jnp.int32, (2 * KEYS, 1), 0)
    krow = jnp.where(krow >= KEYS, krow - KEYS, krow)

    def scores(c, g):
        r0 = c * CHUNK
        qbase = P_Q + g * 4 * LANES
        q = jnp.concatenate(
            [p_ref[r0:r0 + CHUNK, qbase + p * LANES:qbase + (p + 1) * LANES] for p in range(4)], axis=0)
        kk = jnp.concatenate([ke_s[g, r0:r0 + KEYS, :], ko_s[g, r0:r0 + KEYS, :]], axis=0)
        st = lax.dot_general(kk, q, NT, preferred_element_type=F32)
        if r0 < HALO:
            nbad = jnp.where(first, HALO - r0, 0)
            st = jnp.where(krow < nbad, -jnp.inf, st)
        return st

    def finish(c, g, st):
        r0 = c * CHUNK
        vt_s, v0 = (vta_s, r0) if c % 2 == 0 else (vtb_s, r0 - CHUNK)
        vt = vt_s[g * HEAD_DIM:(g + 1) * HEAD_DIM, v0:v0 + KEYS]
        outs = []
        for par in range(2):
            s = st[par * KEYS:(par + 1) * KEYS]
            sc = sink_ref[2 * g + par:2 * g + par + 1, :]
            m = jnp.maximum(jnp.max(s, axis=0, keepdims=True), sc)
            p = jnp.exp(s - m)
            l = jnp.sum(p, axis=0, keepdims=True) + jnp.exp(sc - m)
            outs.append(jnp.dot(vt, p.astype(BF16), preferred_element_type=F32) / l)
        ot = jnp.concatenate(outs, axis=0)
        for p2 in range(2):
            blk = ot[:, p2 * LANES:(p2 + 1) * LANES].T
            for h in range(2):
                c0 = (g * 4 + 2 * p2 + h) * LANES
                attn_s[r0:r0 + CHUNK, c0:c0 + LANES] = blk[h * CHUNK:(h + 1) * CHUNK].astype(BF16)

    units = [(c, g) for c in range(tm // CHUNK) for g in range(N_KV_HEADS)]
    pending = [scores(*u) for u in units[:SCORES_AHEAD]]
    conv_blocks = []
    for n, (c, g) in enumerate(units):
        st = pending.pop(0)
        if n + SCORES_AHEAD < len(units):
            pending.append(scores(*units[n + SCORES_AHEAD]))
        finish(c, g, st)
        if n % 4 == 3:
            cb0 = (n // 4) * (D_MODEL // 4)
            conv_blocks.append(jnp.dot(p_ref[:, P_CI:P_CI + D_MODEL], wco_ref[:, cb0:cb0 + D_MODEL // 4],
                                       preferred_element_type=F32))

    conv_out = jnp.concatenate(conv_blocks, axis=1)
    attn_out = jnp.dot(attn_s[...], wao_ref[...], preferred_element_type=F32)
    gc = p_ref[:, P_G:P_G + D_MODEL].astype(F32)
    ga = p_ref[:, P_G + D_MODEL:P_G + 2 * D_MODEL].astype(F32)
    merged = (gc * conv_out + ga * attn_out).astype(BF16)
    h_new = x_ref[...] + jnp.dot(merged, wo_ref[...], preferred_element_type=F32)
    o_ref[...] = h_new
    hn = _route_tile(tm, h_new, g2_ref, wt_ref, bt_ref, tri_ref, r_ref, rt_ref, cnt_ref, base_s)
    _store_rows(hp_ref, (), tm, hn)


def _mix(p, x2, sink_tab, w_co, w_ao, w_o, g2, wt_router, bt_router, seq, tm):
    t = x2.shape[0]
    row = lambda i: (i, 0)
    fixed = lambda i: (0, 0)
    sq = (D_MODEL, D_MODEL)
    n_keys = HALO + tm
    return pl.pallas_call(
        functools.partial(_mix_kernel, tm, seq // tm),
        grid=(t // tm,),
        in_specs=[
            pl.BlockSpec((tm, P_COLS), row),
            pl.BlockSpec((HALO, 2 * LANES), lambda i: (jnp.maximum(i * (tm // HALO) - 1, 0), P_KV // (2 * LANES))),
            pl.BlockSpec((tm, D_MODEL), row),
            pl.BlockSpec(sink_tab.shape, fixed),
            pl.BlockSpec(sq, fixed),
            pl.BlockSpec(sq, fixed),
            pl.BlockSpec(sq, fixed),
            pl.BlockSpec((1, D_MODEL), fixed),
            pl.BlockSpec((2 * LANES, D_MODEL), fixed),
            pl.BlockSpec((LANES, LANES), fixed),
            pl.BlockSpec((tm, tm), fixed),
        ],
        out_specs=[
            pl.BlockSpec((tm, D_MODEL), row),
            pl.BlockSpec((tm, LANES), row),
            pl.BlockSpec((8, tm), lambda i: (0, i)),
            pl.BlockSpec((LANES, LANES), fixed),
            pl.BlockSpec((tm * ROW_WORDS, LANES), row),
        ],
        out_shape=[
            jax.ShapeDtypeStruct((t, D_MODEL), F32),
            jax.ShapeDtypeStruct((t, LANES), F32),
            jax.ShapeDtypeStruct((8, t), F32),
            jax.ShapeDtypeStruct((LANES, LANES), F32),
            jax.ShapeDtypeStruct((t * ROW_WORDS, LANES), U32),
        ],
        scratch_shapes=[
            pltpu.VMEM((N_KV_HEADS, n_keys, LANES), BF16),
            pltpu.VMEM((N_KV_HEADS, n_keys, LANES), BF16),
            pltpu.VMEM((2 * HEAD_DIM, n_keys), BF16),
            pltpu.VMEM((2 * HEAD_DIM, n_keys), BF16),
            pltpu.VMEM((tm, D_MODEL), BF16),
            pltpu.VMEM((LANES, LANES), F32),
        ],
        compiler_params=pltpu.CompilerParams(dimension_semantics=("arbitrary",), vmem_limit_bytes=VMEM_BIG),
        name="mix",
    )(p, p, x2, sink_tab, w_co, w_ao, w_o, g2, wt_router, bt_router, jnp.triu(jnp.ones((tm, tm), BF16), 1))


def _rows_copy(src_ref, dst_ref, sem, n):
    return pltpu.make_async_copy(src_ref.at[pl.ds(0, n)], dst_ref.at[pl.ds(0, n)], sem)


def _dispatch_kernel(tm, n_tiles, n_blocks, zm_ref, d_ref, hp_ref, xs_ref, hn_s, zbuf, zsem, sem):
    i = pl.program_id(0)
    blk_rows = MOE_BLOCK * ROW_WORDS

    @pl.when(i == 0)
    def _():
        zbuf[...] = jnp.zeros_like(zbuf)

        def zero_copy(b):
            r = pl.multiple_of(b * blk_rows, blk_rows)
            return pltpu.make_async_copy(zbuf, xs_ref.at[pl.ds(r, blk_rows)], zsem)

        def start(b, c):
            @pl.when(zm_ref[b] == 1)
            def _():
                zero_copy(b).start()
            return c

        def wait(b, c):
            @pl.when(zm_ref[b] == 1)
            def _():
                zero_copy(b).wait()
            return c

        lax.fori_loop(0, n_blocks, start, 0)
        lax.fori_loop(0, n_blocks, wait, 0)

    slot = i % 2
    hn_s[slot] = hp_ref[...]

    def body(t, c):
        src = hn_s.at[slot, pl.ds(pl.multiple_of(t * ROW_WORDS, ROW_WORDS), ROW_WORDS)]
        for k in range(2):
            d = pl.multiple_of(d_ref[k * tm + t] * ROW_WORDS, ROW_WORDS)
            pltpu.make_async_copy(src, xs_ref.at[pl.ds(d, ROW_WORDS)], sem.at[slot]).start(priority=k)
        return c

    lax.fori_loop(0, tm, body, 0, unroll=8)

    def drain(s):
        for _ in range(2):
            _rows_copy(hn_s.at[s], xs_ref, sem.at[s], tm * ROW_WORDS).wait()

    @pl.when(i > 0)
    def _():
        drain(1 - slot)

    @pl.when(i == n_tiles - 1)
    def _():
        drain(slot)


def _dispatch(hp, dest, zero_mask, n_rows, tm):
    n_tiles = hp.shape[0] // (tm * ROW_WORDS)
    n_blocks = n_rows // MOE_BLOCK
    return pl.pallas_call(
        functools.partial(_dispatch_kernel, tm, n_tiles, n_blocks),
        grid_spec=pltpu.PrefetchScalarGridSpec(
            num_scalar_prefetch=1,
            grid=(n_tiles,),
            in_specs=[
                pl.BlockSpec((2 * tm,), lambda i, zm: (i,), memory_space=pltpu.SMEM),
                pl.BlockSpec((tm * ROW_WORDS, LANES), lambda i, zm: (i, 0)),
            ],
            out_specs=pl.BlockSpec(memory_space=pl.ANY),
            scratch_shapes=[
                pltpu.VMEM((2, tm * ROW_WORDS, LANES), U32),
                pltpu.VMEM((MOE_BLOCK * ROW_WORDS, LANES), U32),
                pltpu.SemaphoreType.DMA(()),
                pltpu.SemaphoreType.DMA((2,)),
            ],
        ),
        out_shape=jax.ShapeDtypeStruct((n_rows * ROW_WORDS, LANES), U32),
        compiler_params=pltpu.CompilerParams(dimension_semantics=("arbitrary",)),
        name="dispatch",
    )(zero_mask, dest, hp)


def _expert_kernel(layer, be_ref, nw_ref, nx_ref, sl_ref, nu_ref, x_ref, w1_hbm, w3_hbm, w2_hbm, o_ref,
                   w1_f, w3_f, w2_f, w1_s, w3_s, w2_s, sem):
    b0 = STEP_BLOCKS * pl.program_id(0)
    n_used = nu_ref[0]

    def fetch(e, slot):
        return (pltpu.make_async_copy(w1_hbm.at[layer, e], w1_f.at[slot], sem.at[slot]),
                pltpu.make_async_copy(w3_hbm.at[layer, e], w3_f.at[slot], sem.at[slot]),
                pltpu.make_async_copy(w2_hbm.at[layer, e], w2_f.at[slot], sem.at[slot]))

    def switch(b):
        @pl.when((b < n_used) & (nw_ref[b] == 1))
        def _():
            slot = sl_ref[b]
            stage = slot % 2

            @pl.when(b == 0)
            def _():
                for c in fetch(be_ref[0], 0):
                    c.start()

            for c in fetch(be_ref[b], stage):
                c.wait()
            w1_s[slot] = w1_f[stage].astype(BF16)
            w3_s[slot] = w3_f[stage].astype(BF16)
            w2_s[slot] = w2_f[stage].astype(BF16)

            @pl.when(nx_ref[b] >= 0)
            def _():
                for c in fetch(nx_ref[b], 1 - stage):
                    c.start()

    @pl.when(b0 < n_used)
    def _():
        for k in range(STEP_BLOCKS):
            switch(b0 + k)
        nb = STEP_BLOCKS
        slots = [sl_ref[b0 + k] for k in range(nb)]
        xs = [_load_rows(x_ref, (), k * MOE_BLOCK, MOE_BLOCK).astype(BF16) for k in range(nb)]
        h1 = [jnp.dot(xs[k], w1_s[slots[k]], preferred_element_type=F32) for k in range(nb)]
        h3 = [jnp.dot(xs[k], w3_s[slots[k]], preferred_element_type=F32) for k in range(nb)]
        for k in range(nb):
            hid = (h1[k] * jax.nn.sigmoid(h1[k]) * h3[k]).astype(BF16)
            y = jnp.dot(hid, w2_s[slots[k]], preferred_element_type=F32)
            for j in range(ROW_WORDS):
                w = _pack_pair(y[:, j * LANES:(j + 1) * LANES], y[:, HALF + j * LANES:HALF + (j + 1) * LANES])
                o_ref[pl.ds(k * MOE_BLOCK * ROW_WORDS + j, MOE_BLOCK, stride=ROW_WORDS), :] = w

    @pl.when(b0 >= n_used)
    def _():
        o_ref[...] = jnp.zeros_like(o_ref)


def _experts(xs, plan, w1, w3, w2, layer):
    n_rows = xs.shape[0] // ROW_WORDS
    blk = (STEP_BLOCKS * MOE_BLOCK * ROW_WORDS, LANES)
    any_spec = pl.BlockSpec(memory_space=pl.ANY)
    return pl.pallas_call(
        functools.partial(_expert_kernel, layer),
        grid_spec=pltpu.PrefetchScalarGridSpec(
            num_scalar_prefetch=5,
            grid=(n_rows // (STEP_BLOCKS * MOE_BLOCK),),
            in_specs=[
                pl.BlockSpec(blk, lambda p, be, nw, nx, sl, nu: (jnp.minimum(p, (nu[0] - 1) // STEP_BLOCKS), 0)),
                any_spec, any_spec, any_spec,
            ],
            out_specs=pl.BlockSpec(blk, lambda p, be, nw, nx, sl, nu: (p, 0)),
            scratch_shapes=[
                pltpu.VMEM((2, D_MODEL, EXPERT_FF), F32),
                pltpu.VMEM((2, D_MODEL, EXPERT_FF), F32),
                pltpu.VMEM((2, EXPERT_FF, D_MODEL), F32),
                pltpu.VMEM((W_SLOTS, D_MODEL, EXPERT_FF), BF16),
                pltpu.VMEM((W_SLOTS, D_MODEL, EXPERT_FF), BF16),
                pltpu.VMEM((W_SLOTS, EXPERT_FF, D_MODEL), BF16),
                pltpu.SemaphoreType.DMA((2,)),
            ],
        ),
        out_shape=jax.ShapeDtypeStruct(xs.shape, U32),
        compiler_params=pltpu.CompilerParams(dimension_semantics=("arbitrary",), vmem_limit_bytes=VMEM_BIG),
        name="experts",
    )(*plan, xs, w1, w3, w2)


def _issue_row_gathers(tm, d_ref, ys_ref, ybuf, sem, slot, unrolled):
    def one(t):
        for k in range(2):
            d = pl.multiple_of(d_ref[k * tm + t] * ROW_WORDS, ROW_WORDS)
            r = (k * tm + t) * ROW_WORDS
            if not isinstance(r, int):
                r = pl.multiple_of(r, ROW_WORDS)
            pltpu.make_async_copy(
                ys_ref.at[pl.ds(d, ROW_WORDS)], ybuf.at[slot, pl.ds(r, ROW_WORDS)], sem.at[slot]).start(priority=k)

    if unrolled:
        for t in range(tm):
            one(t)
    else:
        def body(t, c):
            one(t)
            return c

        lax.fori_loop(0, tm, body, 0, unroll=8)


def _combined_tile(tm, r_ref, h_ref, ys_ref, ybuf, sem, slot):
    _rows_copy(ys_ref, ybuf.at[slot], sem.at[slot], 2 * tm * ROW_WORDS).wait()
    r = r_ref[...]
    w0 = r[:, R_W0:R_W0 + 1]
    w1 = r[:, R_W1:R_W1 + 1]
    return h_ref[...] + (w0 * _load_rows(ybuf, (slot,), 0, tm) + w1 * _load_rows(ybuf, (slot,), tm, tm))


def _combine_kernel(tm, n_tiles, dcur_ref, dnext_ref, r_ref, h_ref, fg_ref, ys_ref, o_ref, ybuf, sem):
    i = pl.program_id(0)

    @pl.when(i == 0)
    def _():
        _issue_row_gathers(tm, dcur_ref, ys_ref, ybuf, sem, 0, False)

    @pl.when(i + 1 < n_tiles)
    def _():
        _issue_row_gathers(tm, dnext_ref, ys_ref, ybuf, sem, (i + 1) % 2, False)

    o_ref[...] = _rms(_combined_tile(tm, r_ref, h_ref, ys_ref, ybuf, sem, i % 2), fg_ref[...])


def _combine_specs(tm, n_tiles):
    row = lambda i: (i, 0)
    return [
        pl.BlockSpec((2 * tm,), lambda i: (i,), memory_space=pltpu.SMEM),
        pl.BlockSpec((2 * tm,), lambda i: (jnp.minimum(i + 1, n_tiles - 1),), memory_space=pltpu.SMEM),
        pl.BlockSpec((tm, LANES), row),
        pl.BlockSpec((tm, D_MODEL), row),
    ]


def _combine_scratch(tm):
    return [pltpu.VMEM((2, 2 * tm * ROW_WORDS, LANES), U32), pltpu.SemaphoreType.DMA((2,))]


def _combine_final(ys, dest, route, h2, final_g, tm):
    t = h2.shape[0]
    n_tiles = t // tm
    return pl.pallas_call(
        functools.partial(_combine_kernel, tm, n_tiles),
        grid=(n_tiles,),
        in_specs=_combine_specs(tm, n_tiles)
        + [pl.BlockSpec((1, D_MODEL), lambda i: (0, 0)), pl.BlockSpec(memory_space=pl.ANY)],
        out_specs=pl.BlockSpec((tm, D_MODEL), lambda i: (i, 0)),
        out_shape=jax.ShapeDtypeStruct((t, D_MODEL), F32),
        scratch_shapes=_combine_scratch(tm),
        compiler_params=pltpu.CompilerParams(dimension_semantics=("arbitrary",)),
        name="combine",
    )(dest, dest, route, h2, final_g, ys)


def _combine_proj_kernel(tm, n_tiles, tiles_per_seq, dcur_ref, dnext_ref, r_ref, hm_ref, g_ref, w_ref, bg_ref,
                         rot_ref, cw_ref, ys_ref, o_ref, h_ref, ybuf, sem, carry_s):
    i = pl.program_id(0)
    slot = i % 2
    first = (i % tiles_per_seq) == 0

    @pl.when(i == 0)
    def _():
        carry_s[...] = jnp.zeros_like(carry_s)
        _issue_row_gathers(tm, dcur_ref, ys_ref, ybuf, sem, 0, False)

    h = _combined_tile(tm, r_ref, hm_ref, ys_ref, ybuf, sem, slot)
    h_ref[...] = h
    _issue_row_gathers(tm, dnext_ref, ys_ref, ybuf, sem, 1 - slot, True)
    _proj_body(_rms(h, g_ref[...]).astype(BF16), first, False, w_ref, bg_ref, rot_ref, cw_ref, o_ref, carry_s)

    @pl.when(i == n_tiles - 1)
    def _():
        _rows_copy(ys_ref, ybuf.at[1 - slot], sem.at[1 - slot], 2 * tm * ROW_WORDS).wait()


def _combine_proj(ys, dest, route, h2, g1, w_in, b_gate, rot, conv_w, seq, tm):
    t = h2.shape[0]
    n_tiles = t // tm
    row = lambda i: (i, 0)
    return pl.pallas_call(
        functools.partial(_combine_proj_kernel, tm, n_tiles, seq // tm),
        grid=(n_tiles,),
        in_specs=_combine_specs(tm, n_tiles) + _proj_specs(w_in, conv_w, tm) + [pl.BlockSpec(memory_space=pl.ANY)],
        out_specs=[pl.BlockSpec((tm, P_COLS), row), pl.BlockSpec((tm, D_MODEL), row)],
        out_shape=[jax.ShapeDtypeStruct((t, P_COLS), BF16), jax.ShapeDtypeStruct((t, D_MODEL), F32)],
        scratch_shapes=_combine_scratch(tm) + [pltpu.VMEM((8, D_MODEL), F32)],
        compiler_params=pltpu.CompilerParams(dimension_semantics=("arbitrary",), vmem_limit_bytes=VMEM_BIG),
        name="combine_proj",
    )(dest, dest, route, h2, g1, w_in, b_gate, rot, conv_w, ys)


def _rotary_table(positions):
    half = ROT_DIM // 2
    inv = ROPE_THETA ** (-jnp.arange(0, ROT_DIM, 2, dtype=F32) / ROT_DIM)
    ang = positions.reshape(-1).astype(F32)[:, None] * inv
    cs1 = jnp.concatenate([jnp.cos(ang), jnp.sin(ang), jnp.ones((ang.shape[0], 1), F32)], axis=1)
    lane = jnp.arange(LANES) % HEAD_DIM
    j = jnp.arange(2 * half + 1)[:, None]
    a = jnp.where(lane < half, j == lane, jnp.where(lane < ROT_DIM, j == lane - half, j == 2 * half))
    b = -((lane < half) & (j == half + lane)).astype(F32)
    c = ((lane >= half) & (lane < ROT_DIM) & (j == lane)).astype(F32)
    place = jnp.concatenate([a.astype(F32), b, c], axis=1)
    return jnp.dot(cs1, place, precision=lax.Precision.HIGHEST)


def _sink_table(sinks):
    s = sinks.astype(F32).reshape(N_KV_HEADS, 4, 2).transpose(0, 2, 1)
    s = jnp.broadcast_to(s[..., None], (N_KV_HEADS, 2, 4, CHUNK)).reshape(2 * N_KV_HEADS, 4 * CHUNK)
    return jnp.concatenate([s, jnp.zeros((8 - 2 * N_KV_HEADS, 4 * CHUNK), F32)], axis=0)


def _router_params(w_group, b_group, w_route, b_route):
    d = w_group.shape[0]
    pad = LANES - N_GROUPS - N_EXPERTS
    wt = jnp.concatenate([w_group.T, w_route.transpose(0, 2, 1).reshape(N_EXPERTS, d), jnp.zeros((pad, d), F32)], axis=0)
    hi = wt.astype(BF16)
    lo = (wt - hi.astype(F32)).astype(BF16)
    bias = jnp.concatenate([b_group, b_route.reshape(-1), jnp.zeros((pad,), F32)])
    return jnp.concatenate([hi, lo], axis=0), jnp.broadcast_to(bias[:, None], (LANES, LANES))


def _moe_plan(route_t, counts, n_tok, tm):
    n_rows = -(-(2 * n_tok) // MOE_BLOCK) * MOE_BLOCK + N_EXPERTS * MOE_BLOCK
    n_blocks = n_rows // MOE_BLOCK
    ids = jnp.arange(N_EXPERTS, dtype=jnp.int32)
    cnt = counts[R_ROW0:R_ROW0 + N_EXPERTS, 0].astype(jnp.int32)
    padded = (cnt + MOE_BLOCK - 1) // MOE_BLOCK * MOE_BLOCK
    pad_end = jnp.sum(jnp.where(ids[None, :] <= ids[:, None], padded[None, :], 0), axis=1)
    pad_start = pad_end - padded
    e = route_t[R_E0:R_E1 + 1].astype(jnp.int32)
    rank = route_t[R_RANK0:R_RANK1 + 1].astype(jnp.int32)
    dest = jnp.sum(jnp.where(e[..., None] == ids, pad_start, 0), axis=-1) + rank
    dest = dest.reshape(2, n_tok // tm, tm).transpose(1, 0, 2).reshape(-1)
    n_used = pad_end[-1] // MOE_BLOCK
    blk = jnp.arange(n_blocks, dtype=jnp.int32)
    be = jnp.minimum(jnp.sum((pad_end[None, :] <= blk[:, None] * MOE_BLOCK).astype(jnp.int32), axis=1), N_EXPERTS - 1)
    be_last = jnp.sum(jnp.where(blk == n_used - 1, be, 0))
    be = jnp.where(blk < n_used, be, be_last)
    new_expert = jnp.concatenate([jnp.ones((1,), jnp.int32), (be[1:] != be[:-1]).astype(jnp.int32)])
    later = blk[None, :] > blk[:, None]
    seq_slot = (jnp.sum(jnp.where(later.T | (blk[None, :] == blk[:, None]), new_expert[None, :], 0), axis=1) - 1) % W_SLOTS
    next_start = jnp.min(jnp.where(later & (new_expert[None, :] == 1), blk[None, :], n_blocks), axis=1)
    next_expert = jnp.sum(jnp.where(blk[None, :] == next_start[:, None], be[None, :], 0), axis=1)
    next_expert = jnp.where(next_start < n_blocks, next_expert, -1)
    last = pad_end // MOE_BLOCK - 1
    is_last = jnp.any((blk[:, None] == last[None, :]) & (padded[None, :] > 0), axis=1)
    zero_mask = (is_last | (blk >= n_used)).astype(jnp.int32)
    plan = (be, new_expert, next_expert.astype(jnp.int32), seq_slot.astype(jnp.int32), n_used.astype(jnp.int32).reshape(1))
    return n_rows, dest, plan, zero_mask


def kernel(x, positions, norm1_g, w_in, b_gate, conv_w, sinks, w_conv_out, w_attn_out, w_o, norm2_g, w_group, b_group, w_route, b_route, w1, w3, w2, final_g):
    b, s, d = x.shape
    t = b * s
    depth = w_in.shape[0]
    tm = ROW_TILE
    rot = _rotary_table(positions)
    h = x.reshape(t, d)
    ys = dest = route = None
    for l in range(depth):
        w_in_l = w_in[l].astype(BF16)
        if l == 0:
            p = _proj(h, norm1_g[l][None], w_in_l, b_gate[l][None], rot, conv_w[l], s, tm)
        else:
            p, h = _combine_proj(ys, dest, route, h, norm1_g[l][None], w_in_l, b_gate[l][None], rot, conv_w[l], s, tm)
        g2 = norm2_g[l][None]
        wt_router, bt_router = _router_params(w_group[l], b_group[l], w_route[l], b_route[l])
        h, route, route_t, counts, hp = _mix(p, h, _sink_table(sinks[l]), w_conv_out[l].astype(BF16),
                                    w_attn_out[l].astype(BF16), w_o[l].astype(BF16), g2, wt_router, bt_router, s, tm)
        n_rows, dest, plan, zero_mask = _moe_plan(route_t, counts, t, tm)
        xs = _dispatch(hp, dest, zero_mask, n_rows, tm)
        ys = _experts(xs, plan, w1, w3, w2, l)
    return _combine_final(ys, dest, route, h, final_g[None], tm).reshape(b, s, d)
```

```python
import functools

import jax
import jax.numpy as jnp
from jax import lax
from jax.experimental import pallas as pl
from jax.experimental.pallas import tpu as pltpu

F32 = jnp.float32
BF16 = jnp.bfloat16
U32 = jnp.uint32

D_MODEL = 1024
HALF = D_MODEL // 2
RMS_EPS = 1e-5
CHUNK = 64
WINDOW_CHUNKS = 2
HEAD_DIM = 64
N_Q_HEADS = 16
N_KV_HEADS = 2
ROT_DIM = 16
ROPE_THETA = 500000.0
N_GROUPS = 8
EXPERTS_PER_GROUP = 8
N_EXPERTS = 64
EXPERT_FF = 512
MOE_BLOCK = 256
STEP_BLOCKS = 4
W_SLOTS = 4
LANES = 128

W_CB, W_CC, W_CX, W_Q, W_K, W_G = 0, 1024, 2048, 3072, 4096, 4352
P_CI, P_Q, P_KV, P_G = 0, 1024, 2048, 2304
P_COLS = 4352
HALO = WINDOW_CHUNKS * CHUNK
KEYS = HALO + CHUNK
LOG2E = 1.4426950408889634
Q_SCALE = HEAD_DIM ** -0.5 * LOG2E
SCORES_AHEAD = 4

ROW_TILE = 512
PROJ_TILE = 1024
VMEM_BIG = 56 * 1024 * 1024
NT = (((1,), (1,)), ((), ()))


def _rms(x, g):
    ms = jnp.mean(x * x, axis=-1, keepdims=True)
    return x * lax.rsqrt(ms + RMS_EPS) * g


def _pack_pair(lo, hi):
    def rnd(x):
        return lax.bitcast_convert_type(x, U32) + U32(0x8000)

    return (rnd(hi) & U32(0xFFFF0000)) | (rnd(lo) >> 16)


def _unpack_pair(w):
    lo = lax.bitcast_convert_type(w << 16, F32)
    hi = lax.bitcast_convert_type(w & U32(0xFFFF0000), F32)
    return lo, hi


ROW_WORDS = HALF // LANES


def _store_rows(ref, lead, n, x):
    for j in range(ROW_WORDS):
        w = _pack_pair(x[:, j * LANES:(j + 1) * LANES], x[:, HALF + j * LANES:HALF + (j + 1) * LANES])
        ref[lead + (pl.ds(j, n, stride=ROW_WORDS), slice(None))] = w


def _load_rows(ref, lead, row0, n):
    parts = [_unpack_pair(ref[lead + (pl.ds(row0 * ROW_WORDS + j, n, stride=ROW_WORDS), slice(None))])
             for j in range(ROW_WORDS)]
    return jnp.concatenate([p[0] for p in parts] + [p[1] for p in parts], axis=1)


def _proj_kernel(tiles_per_seq, x_ref, g_ref, w_ref, bg_ref, rot_ref, cw_ref, o_ref, carry_s):
    first = (pl.program_id(0) % tiles_per_seq) == 0

    @pl.when(pl.program_id(0) == 0)
    def _():
        carry_s[...] = jnp.zeros_like(carry_s)

    _proj_body(_rms(x_ref[...], g_ref[...]).astype(BF16), first, True, w_ref, bg_ref, rot_ref, cw_ref, o_ref, carry_s)


def _proj_body(xn, first, gates_first, w_ref, bg_ref, rot_ref, cw_ref, o_ref, carry_s):
    tm = xn.shape[0]

    def mm(c0, n):
        return jnp.dot(xn, w_ref[:, c0:c0 + n], preferred_element_type=F32)

    ra = rot_ref[:, 0:LANES]
    rb = rot_ref[:, LANES:2 * LANES]
    rc = rot_ref[:, 2 * LANES:3 * LANES]

    def rot(t):
        return t * ra + pltpu.roll(t, LANES - ROT_DIM // 2, 1) * rb + pltpu.roll(t, ROT_DIM // 2, 1) * rc

    nc = 512
    row = lax.broadcasted_iota(jnp.int32, (tm, 1), 0)
    cw = cw_ref[...]

    def gates():
        for j in range(4):
            gt = mm(W_G + j * nc, nc) + bg_ref[:, j * nc:(j + 1) * nc]
            o_ref[:, P_G + j * nc:P_G + (j + 1) * nc] = jax.nn.sigmoid(gt).astype(BF16)

    def queries():
        for j in range(2):
            q = mm(W_Q + j * nc, nc)
            for p in range(nc // LANES):
                t = rot(q[:, p * LANES:(p + 1) * LANES]) * Q_SCALE
                o_ref[:, P_Q + j * nc + p * LANES:P_Q + j * nc + (p + 1) * LANES] = t.astype(BF16)

    def conv():
        for j in range(2):
            cs = slice(j * nc, (j + 1) * nc)
            u = mm(W_CB + j * nc, nc) * mm(W_CX + j * nc, nc)
            prev = carry_s[:, cs]
            h1 = jnp.where(first, 0.0, prev[7:8])
            h2 = jnp.where(first, 0.0, prev[6:7])
            s1 = jnp.where(row == 0, h1, pltpu.roll(u, 1, 0))
            s2 = jnp.where(row == 0, h2, jnp.where(row == 1, h1, pltpu.roll(u, 2, 0)))
            y = cw[2:3, cs] * u + cw[1:2, cs] * s1 + cw[0:1, cs] * s2
            carry_s[:, cs] = u[tm - 8:tm]
            o_ref[:, P_CI + j * nc:P_CI + (j + 1) * nc] = (mm(W_CC + j * nc, nc) * y).astype(BF16)

    def keys_values():
        kv = mm(W_K, 2 * LANES)
        o_ref[:, P_KV:P_KV + LANES] = rot(kv[:, :LANES]).astype(BF16)
        o_ref[:, P_KV + LANES:P_KV + 2 * LANES] = kv[:, LANES:].astype(BF16)

    for group in (gates, queries, conv, keys_values) if gates_first else (conv, queries, keys_values, gates):
        group()


def _proj_specs(w_in, conv_w, tm):
    fixed = lambda i: (0, 0)
    return [
        pl.BlockSpec((1, D_MODEL), fixed),
        pl.BlockSpec(w_in.shape, fixed, pipeline_mode=pl.Buffered(1)),
        pl.BlockSpec((1, 2 * D_MODEL), fixed),
        pl.BlockSpec((tm, 3 * LANES), lambda i: (i, 0)),
        pl.BlockSpec(conv_w.shape, fixed),
    ]


def _proj(x2, g1, w_in, b_gate, rot, conv_w, seq, tm):
    t = x2.shape[0]
    row = lambda i: (i, 0)
    return pl.pallas_call(
        functools.partial(_proj_kernel, seq // tm),
        grid=(t // tm,),
        in_specs=[pl.BlockSpec((tm, D_MODEL), row)] + _proj_specs(w_in, conv_w, tm),
        out_specs=pl.BlockSpec((tm, P_COLS), row),
        out_shape=jax.ShapeDtypeStruct((t, P_COLS), BF16),
        scratch_shapes=[pltpu.VMEM((8, D_MODEL), F32)],
        compiler_params=pltpu.CompilerParams(dimension_semantics=("arbitrary",), vmem_limit_bytes=VMEM_BIG),
        name="proj",
    )(x2, g1, w_in, b_gate, rot, conv_w)


R_E0, R_E1, R_RANK0, R_RANK1, R_W0, R_W1 = range(6)
R_ROW0 = N_GROUPS


def _route_tile(tm, h, g_ref, wt_ref, bt_ref, tri_ref, r_ref, rt_ref, cnt_ref, base_s):
    hn = _rms(h, g_ref[...])
    hi = hn.astype(BF16)
    lo = (hn - hi.astype(F32)).astype(BF16)
    both = lax.dot_general(wt_ref[...], hi, NT, preferred_element_type=F32)
    lt = (both[:LANES] + both[LANES:] + lax.dot_general(wt_ref[0:LANES, :], lo, NT, preferred_element_type=F32)
          + bt_ref[:, 0:1])
    row = lax.broadcasted_iota(jnp.int32, (LANES, tm), 0)
    neg = -jnp.inf

    def first_max(v):
        m = jnp.max(v, axis=0, keepdims=True)
        return m, jnp.min(jnp.where(v == m, row, LANES), axis=0, keepdims=True)

    gl = jnp.where(row < N_GROUPS, lt, neg)
    gmax, gtop = first_max(gl)
    pg = 1.0 / jnp.sum(jnp.exp(gl - gmax), axis=0, keepdims=True)
    in_group = (row >= R_ROW0) & (row < R_ROW0 + N_EXPERTS) & (((row - R_ROW0) >> 3) == gtop)
    el = jnp.where(in_group, lt, neg)
    m1, i1 = first_max(el)
    m2, i2 = first_max(jnp.where(row == i1, neg, el))
    e2 = jnp.exp(m2 - m1)
    den = 1.0 + e2
    w0 = pg * (1.0 / den)
    w1 = pg * (e2 / den)
    hit0 = row == i1
    hit1 = row == i2
    a = jnp.where(hit0 | hit1, 1.0, 0.0)
    before = jnp.dot(a.astype(BF16), tri_ref[...], preferred_element_type=F32) + base_s[:, 0:1]
    rank0 = jnp.sum(jnp.where(hit0, before, 0.0), axis=0, keepdims=True)
    rank1 = jnp.sum(jnp.where(hit1, before, 0.0), axis=0, keepdims=True)
    base_s[...] = base_s[...] + jnp.sum(a, axis=1, keepdims=True)
    cnt_ref[...] = base_s[...]
    vals = ((i1 - R_ROW0).astype(F32), (i2 - R_ROW0).astype(F32), rank0, rank1, w0, w1)
    slab = jnp.zeros((LANES, tm), F32)
    for k, v in enumerate(vals):
        slab = jnp.where(row == k, v, slab)
    r_ref[...] = slab.T
    rt_ref[...] = slab[0:8]


def _mix_kernel(tm, tiles_per_seq, p_ref, hkv_ref, x_ref, sink_ref, wco_ref, wao_ref, wo_ref,
                g2_ref, wt_ref, bt_ref, tri_ref, o_ref, r_ref, rt_ref, cnt_ref, ke_s, ko_s, vta_s, vtb_s, attn_s, base_s):
    i = pl.program_id(0)
    first = (i % tiles_per_seq) == 0
    lo = lax.broadcasted_iota(jnp.int32, (1, LANES), 1) < HEAD_DIM

    @pl.when(i == 0)
    def _():
        base_s[...] = jnp.zeros_like(base_s)

    kv_all = jnp.concatenate([hkv_ref[...], p_ref[:, P_KV:P_KV + 2 * LANES]], axis=0)
    k = kv_all[:, :LANES].astype(F32)
    kr = pltpu.roll(k, HEAD_DIM, 1)
    ke_s[0] = jnp.where(lo, k, 0.0).astype(BF16)
    ko_s[0] = jnp.where(lo, 0.0, kr).astype(BF16)
    ke_s[1] = jnp.where(lo, kr, 0.0).astype(BF16)
    ko_s[1] = jnp.where(lo, 0.0, k).astype(BF16)
    v = kv_all[:, LANES:].astype(F32)
    n_keys = HALO + tm
    for j in range(n_keys // LANES):
        vta_s[:, j * LANES:(j + 1) * LANES] = v[j * LANES:(j + 1) * LANES].T.astype(BF16)
    for j in range(n_keys // LANES - 1):
        vtb_s[:, j * LANES:(j + 1) * LANES] = v[CHUNK + j * LANES:CHUNK + (j + 1) * LANES].T.astype(BF16)
    vtb_s[:, n_keys - LANES:n_keys - CHUNK] = v[n_keys - CHUNK:n_keys].T.astype(BF16)

    krow = lax.broadcasted_iota(jnp.int32, (2 * KEYS, 1), 0)
    krow = jnp.where(krow >= KEYS, krow - KEYS, krow)

    def scores(c, g):
        r0 = c * CHUNK
        qbase = P_Q + g * 4 * LANES
        q = jnp.concatenate(
            [p_ref[r0:r0 + CHUNK, qbase + p * LANES:qbase + (p + 1) * LANES] for p in range(4)], axis=0)
        kk = jnp.concatenate([ke_s[g, r0:r0 + KEYS, :], ko_s[g, r0:r0 + KEYS, :]], axis=0)
        st = lax.dot_general(kk, q, NT, preferred_element_type=F32)
        if r0 < HALO:
            nbad = jnp.where(first, HALO - r0, 0)
            st = jnp.where(krow < nbad, -jnp.inf, st)
        return st

    def finish(c, g, st):
        r0 = c * CHUNK
        vt_s, v0 = (vta_s, r0) if c % 2 == 0 else (vtb_s, r0 - CHUNK)
        vt = vt_s[g * HEAD_DIM:(g + 1) * HEAD_DIM, v0:v0 + KEYS]
        outs = []
        for par in range(2):
            s = st[par * KEYS:(par + 1) * KEYS]
            sc = sink_ref[2 * g + par:2 * g + par + 1, :]
            m = jnp.maximum(jnp.max(s, axis=0, keepdims=True), sc)
            p = jnp.exp2(s - m)
            l = jnp.sum(p, axis=0, keepdims=True) + jnp.exp2(sc - m)
            outs.append(jnp.dot(vt, p.astype(BF16), preferred_element_type=F32) / l)
        ot = jnp.concatenate(outs, axis=0)
        for p2 in range(2):
            blk = ot[:, p2 * LANES:(p2 + 1) * LANES].T
            for h in range(2):
                c0 = (g * 4 + 2 * p2 + h) * LANES
                attn_s[r0:r0 + CHUNK, c0:c0 + LANES] = blk[h * CHUNK:(h + 1) * CHUNK].astype(BF16)

    units = [(c, g) for c in range(tm // CHUNK) for g in range(N_KV_HEADS)]
    pending = [scores(*u) for u in units[:SCORES_AHEAD]]
    conv_blocks = []
    for n, (c, g) in enumerate(units):
        st = pending.pop(0)
        if n + SCORES_AHEAD < len(units):
            pending.append(scores(*units[n + SCORES_AHEAD]))
        finish(c, g, st)
        if n % 4 == 3:
            cb0 = (n // 4) * (D_MODEL // 4)
            conv_blocks.append(jnp.dot(p_ref[:, P_CI:P_CI + D_MODEL], wco_ref[:, cb0:cb0 + D_MODEL // 4],
                                       preferred_element_type=F32))

    conv_out = jnp.concatenate(conv_blocks, axis=1)
    attn_out = jnp.dot(attn_s[...], wao_ref[...], preferred_element_type=F32)
    gc = p_ref[:, P_G:P_G + D_MODEL].astype(F32)
    ga = p_ref[:, P_G + D_MODEL:P_G + 2 * D_MODEL].astype(F32)
    merged = (gc * conv_out + ga * attn_out).astype(BF16)
    h_new = x_ref[...] + jnp.dot(merged, wo_ref[...], preferred_element_type=F32)
    o_ref[...] = h_new
    _route_tile(tm, h_new, g2_ref, wt_ref, bt_ref, tri_ref, r_ref, rt_ref, cnt_ref, base_s)


def _mix(p, x2, sink_tab, w_co, w_ao, w_o, g2, wt_router, bt_router, seq, tm):
    t = x2.shape[0]
    row = lambda i: (i, 0)
    fixed = lambda i: (0, 0)
    sq = (D_MODEL, D_MODEL)
    n_keys = HALO + tm
    return pl.pallas_call(
        functools.partial(_mix_kernel, tm, seq // tm),
        grid=(t // tm,),
        in_specs=[
            pl.BlockSpec((tm, P_COLS), row),
            pl.BlockSpec((HALO, 2 * LANES), lambda i: (jnp.maximum(i * (tm // HALO) - 1, 0), P_KV // (2 * LANES))),
            pl.BlockSpec((tm, D_MODEL), row),
            pl.BlockSpec(sink_tab.shape, fixed),
            pl.BlockSpec(sq, fixed),
            pl.BlockSpec(sq, fixed),
            pl.BlockSpec(sq, fixed),
            pl.BlockSpec((1, D_MODEL), fixed),
            pl.BlockSpec((2 * LANES, D_MODEL), fixed),
            pl.BlockSpec((LANES, LANES), fixed),
            pl.BlockSpec((tm, tm), fixed),
        ],
        out_specs=[
            pl.BlockSpec((tm, D_MODEL), row),
            pl.BlockSpec((tm, LANES), row),
            pl.BlockSpec((8, tm), lambda i: (0, i)),
            pl.BlockSpec((LANES, LANES), fixed),
        ],
        out_shape=[
            jax.ShapeDtypeStruct((t, D_MODEL), F32),
            jax.ShapeDtypeStruct((t, LANES), F32),
            jax.ShapeDtypeStruct((8, t), F32),
            jax.ShapeDtypeStruct((LANES, LANES), F32),
        ],
        scratch_shapes=[
            pltpu.VMEM((N_KV_HEADS, n_keys, LANES), BF16),
            pltpu.VMEM((N_KV_HEADS, n_keys, LANES), BF16),
            pltpu.VMEM((2 * HEAD_DIM, n_keys), BF16),
            pltpu.VMEM((2 * HEAD_DIM, n_keys), BF16),
            pltpu.VMEM((tm, D_MODEL), BF16),
            pltpu.VMEM((LANES, LANES), F32),
        ],
        compiler_params=pltpu.CompilerParams(dimension_semantics=("arbitrary",), vmem_limit_bytes=VMEM_BIG),
        name="mix",
    )(p, p, x2, sink_tab, w_co, w_ao, w_o, g2, wt_router, bt_router, jnp.triu(jnp.ones((tm, tm), BF16), 1))


def _rows_copy(src_ref, dst_ref, sem, n):
    return pltpu.make_async_copy(src_ref.at[pl.ds(0, n)], dst_ref.at[pl.ds(0, n)], sem)


def _dispatch_kernel(tm, n_tiles, n_blocks, zm_ref, d_ref, h_ref, g_ref, xs_ref, hn_s, zbuf, zsem, sem):
    i = pl.program_id(0)
    blk_rows = MOE_BLOCK * ROW_WORDS

    @pl.when(i == 0)
    def _():
        zbuf[...] = jnp.zeros_like(zbuf)

        def zero_copy(b):
            r = pl.multiple_of(b * blk_rows, blk_rows)
            return pltpu.make_async_copy(zbuf, xs_ref.at[pl.ds(r, blk_rows)], zsem)

        def start(b, c):
            @pl.when(zm_ref[b] == 1)
            def _():
                zero_copy(b).start()
            return c

        def wait(b, c):
            @pl.when(zm_ref[b] == 1)
            def _():
                zero_copy(b).wait()
            return c

        lax.fori_loop(0, n_blocks, start, 0)
        lax.fori_loop(0, n_blocks, wait, 0)

    slot = i % 2
    _store_rows(hn_s, (slot,), tm, _rms(h_ref[...], g_ref[...]))

    def body(t, c):
        src = hn_s.at[slot, pl.ds(pl.multiple_of(t * ROW_WORDS, ROW_WORDS), ROW_WORDS)]
        for k in range(2):
            d = pl.multiple_of(d_ref[k * tm + t] * ROW_WORDS, ROW_WORDS)
            pltpu.make_async_copy(src, xs_ref.at[pl.ds(d, ROW_WORDS)], sem.at[slot]).start(priority=k)
        return c

    lax.fori_loop(0, tm, body, 0, unroll=8)

    def drain(s):
        for _ in range(2):
            _rows_copy(hn_s.at[s], xs_ref, sem.at[s], tm * ROW_WORDS).wait()

    @pl.when(i > 0)
    def _():
        drain(1 - slot)

    @pl.when(i == n_tiles - 1)
    def _():
        drain(slot)


def _dispatch(h2, g2, dest, zero_mask, n_rows, tm):
    n_tiles = h2.shape[0] // tm
    n_blocks = n_rows // MOE_BLOCK
    return pl.pallas_call(
        functools.partial(_dispatch_kernel, tm, n_tiles, n_blocks),
        grid_spec=pltpu.PrefetchScalarGridSpec(
            num_scalar_prefetch=1,
            grid=(n_tiles,),
            in_specs=[
                pl.BlockSpec((2 * tm,), lambda i, zm: (i,), memory_space=pltpu.SMEM),
                pl.BlockSpec((tm, D_MODEL), lambda i, zm: (i, 0)),
                pl.BlockSpec((1, D_MODEL), lambda i, zm: (0, 0)),
            ],
            out_specs=pl.BlockSpec(memory_space=pl.ANY),
            scratch_shapes=[
                pltpu.VMEM((2, tm * ROW_WORDS, LANES), U32),
                pltpu.VMEM((MOE_BLOCK * ROW_WORDS, LANES), U32),
                pltpu.SemaphoreType.DMA(()),
                pltpu.SemaphoreType.DMA((2,)),
            ],
        ),
        out_shape=jax.ShapeDtypeStruct((n_rows * ROW_WORDS, LANES), U32),
        compiler_params=pltpu.CompilerParams(dimension_semantics=("arbitrary",)),
        name="dispatch",
    )(zero_mask, dest, h2, g2)


def _expert_kernel(layer, be_ref, nw_ref, nx_ref, sl_ref, nu_ref, x_ref, w1_hbm, w3_hbm, w2_hbm, o_ref,
                   w1_f, w3_f, w2_f, w1_s, w3_s, w2_s, sem):
    b0 = STEP_BLOCKS * pl.program_id(0)
    n_used = nu_ref[0]

    def fetch(e, slot):
        return (pltpu.make_async_copy(w1_hbm.at[layer, e], w1_f.at[slot], sem.at[slot]),
                pltpu.make_async_copy(w3_hbm.at[layer, e], w3_f.at[slot], sem.at[slot]),
                pltpu.make_async_copy(w2_hbm.at[layer, e], w2_f.at[slot], sem.at[slot]))

    def switch(b):
        @pl.when((b < n_used) & (nw_ref[b] == 1))
        def _():
            slot = sl_ref[b]
            stage = slot % 2

            @pl.when(b == 0)
            def _():
                for c in fetch(be_ref[0], 0):
                    c.start()

            for c in fetch(be_ref[b], stage):
                c.wait()
            w1_s[slot] = w1_f[stage].astype(BF16)
            w3_s[slot] = w3_f[stage].astype(BF16)
            w2_s[slot] = w2_f[stage].astype(BF16)

            @pl.when(nx_ref[b] >= 0)
            def _():
                for c in fetch(nx_ref[b], 1 - stage):
                    c.start()

    @pl.when(b0 < n_used)
    def _():
        for k in range(STEP_BLOCKS):
            switch(b0 + k)
        nb = STEP_BLOCKS
        slots = [sl_ref[b0 + k] for k in range(nb)]
        xs = [_load_rows(x_ref, (), k * MOE_BLOCK, MOE_BLOCK).astype(BF16) for k in range(nb)]
        h1 = [jnp.dot(xs[k], w1_s[slots[k]], preferred_element_type=F32) for k in range(nb)]
        h3 = [jnp.dot(xs[k], w3_s[slots[k]], preferred_element_type=F32) for k in range(nb)]
        for k in range(nb):
            hid = (h1[k] * jax.nn.sigmoid(h1[k]) * h3[k]).astype(BF16)
            y = jnp.dot(hid, w2_s[slots[k]], preferred_element_type=F32)
            for j in range(ROW_WORDS):
                w = _pack_pair(y[:, j * LANES:(j + 1) * LANES], y[:, HALF + j * LANES:HALF + (j + 1) * LANES])
                o_ref[pl.ds(k * MOE_BLOCK * ROW_WORDS + j, MOE_BLOCK, stride=ROW_WORDS), :] = w

    @pl.when(b0 >= n_used)
    def _():
        o_ref[...] = jnp.zeros_like(o_ref)


def _experts(xs, plan, w1, w3, w2, layer):
    n_rows = xs.shape[0] // ROW_WORDS
    blk = (STEP_BLOCKS * MOE_BLOCK * ROW_WORDS, LANES)
    any_spec = pl.BlockSpec(memory_space=pl.ANY)
    return pl.pallas_call(
        functools.partial(_expert_kernel, layer),
        grid_spec=pltpu.PrefetchScalarGridSpec(
            num_scalar_prefetch=5,
            grid=(n_rows // (STEP_BLOCKS * MOE_BLOCK),),
            in_specs=[
                pl.BlockSpec(blk, lambda p, be, nw, nx, sl, nu: (jnp.minimum(p, (nu[0] - 1) // STEP_BLOCKS), 0)),
                any_spec, any_spec, any_spec,
            ],
            out_specs=pl.BlockSpec(blk, lambda p, be, nw, nx, sl, nu: (p, 0)),
            scratch_shapes=[
                pltpu.VMEM((2, D_MODEL, EXPERT_FF), F32),
                pltpu.VMEM((2, D_MODEL, EXPERT_FF), F32),
                pltpu.VMEM((2, EXPERT_FF, D_MODEL), F32),
                pltpu.VMEM((W_SLOTS, D_MODEL, EXPERT_FF), BF16),
                pltpu.VMEM((W_SLOTS, D_MODEL, EXPERT_FF), BF16),
                pltpu.VMEM((W_SLOTS, EXPERT_FF, D_MODEL), BF16),
                pltpu.SemaphoreType.DMA((2,)),
            ],
        ),
        out_shape=jax.ShapeDtypeStruct(xs.shape, U32),
        compiler_params=pltpu.CompilerParams(dimension_semantics=("arbitrary",), vmem_limit_bytes=VMEM_BIG),
        name="experts",
    )(*plan, xs, w1, w3, w2)


def _issue_row_gathers(tm, d_ref, ys_ref, ybuf, sem, slot, unrolled):
    def one(t):
        for k in range(2):
            d = pl.multiple_of(d_ref[k * tm + t] * ROW_WORDS, ROW_WORDS)
            r = (k * tm + t) * ROW_WORDS
            if not isinstance(r, int):
                r = pl.multiple_of(r, ROW_WORDS)
            pltpu.make_async_copy(
                ys_ref.at[pl.ds(d, ROW_WORDS)], ybuf.at[slot, pl.ds(r, ROW_WORDS)], sem.at[slot]).start(priority=k)

    if unrolled:
        for t in range(tm):
            one(t)
    else:
        def body(t, c):
            one(t)
            return c

        lax.fori_loop(0, tm, body, 0, unroll=8)


def _combined_tile(tm, r_ref, h_ref, ys_ref, ybuf, sem, slot):
    _rows_copy(ys_ref, ybuf.at[slot], sem.at[slot], 2 * tm * ROW_WORDS).wait()
    r = r_ref[...]
    w0 = r[:, R_W0:R_W0 + 1]
    w1 = r[:, R_W1:R_W1 + 1]
    return h_ref[...] + (w0 * _load_rows(ybuf, (slot,), 0, tm) + w1 * _load_rows(ybuf, (slot,), tm, tm))


def _combine_kernel(tm, n_tiles, dcur_ref, dnext_ref, r_ref, h_ref, fg_ref, ys_ref, o_ref, ybuf, sem):
    i = pl.program_id(0)

    @pl.when(i == 0)
    def _():
        _issue_row_gathers(tm, dcur_ref, ys_ref, ybuf, sem, 0, False)

    @pl.when(i + 1 < n_tiles)
    def _():
        _issue_row_gathers(tm, dnext_ref, ys_ref, ybuf, sem, (i + 1) % 2, False)

    o_ref[...] = _rms(_combined_tile(tm, r_ref, h_ref, ys_ref, ybuf, sem, i % 2), fg_ref[...])


def _combine_specs(tm, n_tiles):
    row = lambda i: (i, 0)
    return [
        pl.BlockSpec((2 * tm,), lambda i: (i,), memory_space=pltpu.SMEM),
        pl.BlockSpec((2 * tm,), lambda i: (jnp.minimum(i + 1, n_tiles - 1),), memory_space=pltpu.SMEM),
        pl.BlockSpec((tm, LANES), row),
        pl.BlockSpec((tm, D_MODEL), row),
    ]


def _combine_scratch(tm):
    return [pltpu.VMEM((2, 2 * tm * ROW_WORDS, LANES), U32), pltpu.SemaphoreType.DMA((2,))]


def _combine_final(ys, dest, route, h2, final_g, tm):
    t = h2.shape[0]
    n_tiles = t // tm
    return pl.pallas_call(
        functools.partial(_combine_kernel, tm, n_tiles),
        grid=(n_tiles,),
        in_specs=_combine_specs(tm, n_tiles)
        + [pl.BlockSpec((1, D_MODEL), lambda i: (0, 0)), pl.BlockSpec(memory_space=pl.ANY)],
        out_specs=pl.BlockSpec((tm, D_MODEL), lambda i: (i, 0)),
        out_shape=jax.ShapeDtypeStruct((t, D_MODEL), F32),
        scratch_shapes=_combine_scratch(tm),
        compiler_params=pltpu.CompilerParams(dimension_semantics=("arbitrary",)),
        name="combine",
    )(dest, dest, route, h2, final_g, ys)


def _combine_proj_kernel(tm, n_tiles, tiles_per_seq, dcur_ref, dnext_ref, r_ref, hm_ref, g_ref, w_ref, bg_ref,
                         rot_ref, cw_ref, ys_ref, o_ref, h_ref, ybuf, sem, carry_s):
    i = pl.program_id(0)
    slot = i % 2
    first = (i % tiles_per_seq) == 0

    @pl.when(i == 0)
    def _():
        carry_s[...] = jnp.zeros_like(carry_s)
        _issue_row_gathers(tm, dcur_ref, ys_ref, ybuf, sem, 0, False)

    h = _combined_tile(tm, r_ref, hm_ref, ys_ref, ybuf, sem, slot)
    h_ref[...] = h
    _issue_row_gathers(tm, dnext_ref, ys_ref, ybuf, sem, 1 - slot, True)
    _proj_body(_rms(h, g_ref[...]).astype(BF16), first, False, w_ref, bg_ref, rot_ref, cw_ref, o_ref, carry_s)

    @pl.when(i == n_tiles - 1)
    def _():
        _rows_copy(ys_ref, ybuf.at[1 - slot], sem.at[1 - slot], 2 * tm * ROW_WORDS).wait()


def _combine_proj(ys, dest, route, h2, g1, w_in, b_gate, rot, conv_w, seq, tm):
    t = h2.shape[0]
    n_tiles = t // tm
    row = lambda i: (i, 0)
    return pl.pallas_call(
        functools.partial(_combine_proj_kernel, tm, n_tiles, seq // tm),
        grid=(n_tiles,),
        in_specs=_combine_specs(tm, n_tiles) + _proj_specs(w_in, conv_w, tm) + [pl.BlockSpec(memory_space=pl.ANY)],
        out_specs=[pl.BlockSpec((tm, P_COLS), row), pl.BlockSpec((tm, D_MODEL), row)],
        out_shape=[jax.ShapeDtypeStruct((t, P_COLS), BF16), jax.ShapeDtypeStruct((t, D_MODEL), F32)],
        scratch_shapes=_combine_scratch(tm) + [pltpu.VMEM((8, D_MODEL), F32)],
        compiler_params=pltpu.CompilerParams(dimension_semantics=("arbitrary",), vmem_limit_bytes=VMEM_BIG),
        name="combine_proj",
    )(dest, dest, route, h2, g1, w_in, b_gate, rot, conv_w, ys)


def _rotary_table(positions):
    half = ROT_DIM // 2
    inv = ROPE_THETA ** (-jnp.arange(0, ROT_DIM, 2, dtype=F32) / ROT_DIM)
    ang = positions.reshape(-1).astype(F32)[:, None] * inv
    cs1 = jnp.concatenate([jnp.cos(ang), jnp.sin(ang), jnp.ones((ang.shape[0], 1), F32)], axis=1)
    lane = jnp.arange(LANES) % HEAD_DIM
    j = jnp.arange(2 * half + 1)[:, None]
    a = jnp.where(lane < half, j == lane, jnp.where(lane < ROT_DIM, j == lane - half, j == 2 * half))
    b = -((lane < half) & (j == half + lane)).astype(F32)
    c = ((lane >= half) & (lane < ROT_DIM) & (j == lane)).astype(F32)
    place = jnp.concatenate([a.astype(F32), b, c], axis=1)
    return jnp.dot(cs1, place, precision=lax.Precision.HIGHEST)


def _sink_table(sinks):
    s = (sinks.astype(F32) * LOG2E).reshape(N_KV_HEADS, 4, 2).transpose(0, 2, 1)
    s = jnp.broadcast_to(s[..., None], (N_KV_HEADS, 2, 4, CHUNK)).reshape(2 * N_KV_HEADS, 4 * CHUNK)
    return jnp.concatenate([s, jnp.zeros((8 - 2 * N_KV_HEADS, 4 * CHUNK), F32)], axis=0)


def _router_params(w_group, b_group, w_route, b_route):
    d = w_group.shape[0]
    pad = LANES - N_GROUPS - N_EXPERTS
    wt = jnp.concatenate([w_group.T, w_route.transpose(0, 2, 1).reshape(N_EXPERTS, d), jnp.zeros((pad, d), F32)], axis=0)
    hi = wt.astype(BF16)
    lo = (wt - hi.astype(F32)).astype(BF16)
    bias = jnp.concatenate([b_group, b_route.reshape(-1), jnp.zeros((pad,), F32)])
    return jnp.concatenate([hi, lo], axis=0), jnp.broadcast_to(bias[:, None], (LANES, LANES))


def _moe_plan(route_t, counts, n_tok, tm):
    n_rows = -(-(2 * n_tok) // MOE_BLOCK) * MOE_BLOCK + N_EXPERTS * MOE_BLOCK
    n_blocks = n_rows // MOE_BLOCK
    ids = jnp.arange(N_EXPERTS, dtype=jnp.int32)
    cnt = counts[R_ROW0:R_ROW0 + N_EXPERTS, 0].astype(jnp.int32)
    padded = (cnt + MOE_BLOCK - 1) // MOE_BLOCK * MOE_BLOCK
    pad_end = jnp.sum(jnp.where(ids[None, :] <= ids[:, None], padded[None, :], 0), axis=1)
    pad_start = pad_end - padded
    e = route_t[R_E0:R_E1 + 1].astype(jnp.int32)
    rank = route_t[R_RANK0:R_RANK1 + 1].astype(jnp.int32)
    dest = jnp.sum(jnp.where(e[..., None] == ids, pad_start, 0), axis=-1) + rank
    dest = dest.reshape(2, n_tok // tm, tm).transpose(1, 0, 2).reshape(-1)
    n_used = pad_end[-1] // MOE_BLOCK
    blk = jnp.arange(n_blocks, dtype=jnp.int32)
    be = jnp.minimum(jnp.sum((pad_end[None, :] <= blk[:, None] * MOE_BLOCK).astype(jnp.int32), axis=1), N_EXPERTS - 1)
    be_last = jnp.sum(jnp.where(blk == n_used - 1, be, 0))
    be = jnp.where(blk < n_used, be, be_last)
    new_expert = jnp.concatenate([jnp.ones((1,), jnp.int32), (be[1:] != be[:-1]).astype(jnp.int32)])
    later = blk[None, :] > blk[:, None]
    seq_slot = (jnp.sum(jnp.where(later.T | (blk[None, :] == blk[:, None]), new_expert[None, :], 0), axis=1) - 1) % W_SLOTS
    next_start = jnp.min(jnp.where(later & (new_expert[None, :] == 1), blk[None, :], n_blocks), axis=1)
    next_expert = jnp.sum(jnp.where(blk[None, :] == next_start[:, None], be[None, :], 0), axis=1)
    next_expert = jnp.where(next_start < n_blocks, next_expert, -1)
    last = pad_end // MOE_BLOCK - 1
    is_last = jnp.any((blk[:, None] == last[None, :]) & (padded[None, :] > 0), axis=1)
    zero_mask = (is_last | (blk >= n_used)).astype(jnp.int32)
    plan = (be, new_expert, next_expert.astype(jnp.int32), seq_slot.astype(jnp.int32), n_used.astype(jnp.int32).reshape(1))
    return n_rows, dest, plan, zero_mask


def kernel(x, positions, norm1_g, w_in, b_gate, conv_w, sinks, w_conv_out, w_attn_out, w_o, norm2_g, w_group, b_group, w_route, b_route, w1, w3, w2, final_g):
    b, s, d = x.shape
    t = b * s
    depth = w_in.shape[0]
    tm = ROW_TILE
    rot = _rotary_table(positions)
    h = x.reshape(t, d)
    ys = dest = route = None
    for l in range(depth):
        w_in_l = w_in[l].astype(BF16)
        if l == 0:
            p = _proj(h, norm1_g[l][None], w_in_l, b_gate[l][None], rot, conv_w[l], s, PROJ_TILE)
        else:
            p, h = _combine_proj(ys, dest, route, h, norm1_g[l][None], w_in_l, b_gate[l][None], rot, conv_w[l], s, tm)
        g2 = norm2_g[l][None]
        wt_router, bt_router = _router_params(w_group[l], b_group[l], w_route[l], b_route[l])
        h, route, route_t, counts = _mix(p, h, _sink_table(sinks[l]), w_conv_out[l].astype(BF16),
                                    w_attn_out[l].astype(BF16), w_o[l].astype(BF16), g2, wt_router, bt_router, s, tm)
        n_rows, dest, plan, zero_mask = _moe_plan(route_t, counts, t, tm)
        xs = _dispatch(h, g2, dest, zero_mask, n_rows, tm)
        ys = _experts(xs, plan, w1, w3, w2, l)
    return _combine_final(ys, dest, route, h, final_g[None], tm).reshape(b, s, d)
```

```python
import functools

import jax
import jax.numpy as jnp
from jax import lax
from jax.experimental import pallas as pl
from jax.experimental.pallas import tpu as pltpu

F32 = jnp.float32
BF16 = jnp.bfloat16
U32 = jnp.uint32

D_MODEL = 1024
HALF = D_MODEL // 2
RMS_EPS = 1e-5
CHUNK = 64
WINDOW_CHUNKS = 2
HEAD_DIM = 64
N_Q_HEADS = 16
N_KV_HEADS = 2
ROT_DIM = 16
ROPE_THETA = 500000.0
N_GROUPS = 8
EXPERTS_PER_GROUP = 8
N_EXPERTS = 64
EXPERT_FF = 512
MOE_BLOCK = 256
STEP_BLOCKS = 4
W_SLOTS = 4
LANES = 128

W_CB, W_CC, W_CX, W_Q, W_K, W_G = 0, 1024, 2048, 3072, 4096, 4352
P_CI, P_Q, P_KV, P_G = 0, 1024, 2048, 2304
P_COLS = 4352
HALO = WINDOW_CHUNKS * CHUNK
KEYS = HALO + CHUNK
LOG2E = 1.4426950408889634
Q_SCALE = HEAD_DIM ** -0.5 * LOG2E
SCORES_AHEAD = 4

ROW_TILE = 512
PROJ_TILE = 1024
VMEM_BIG = 56 * 1024 * 1024
NT = (((1,), (1,)), ((), ()))


def _rms(x, g):
    ms = jnp.mean(x * x, axis=-1, keepdims=True)
    return x * lax.rsqrt(ms + RMS_EPS) * g


def _pack_pair(lo, hi):
    def rnd(x):
        return lax.bitcast_convert_type(x, U32) + U32(0x8000)

    return (rnd(hi) & U32(0xFFFF0000)) | (rnd(lo) >> 16)


def _unpack_pair(w):
    lo = lax.bitcast_convert_type(w << 16, F32)
    hi = lax.bitcast_convert_type(w & U32(0xFFFF0000), F32)
    return lo, hi


ROW_WORDS = HALF // LANES


def _store_rows(ref, lead, n, x):
    for j in range(ROW_WORDS):
        w = _pack_pair(x[:, j * LANES:(j + 1) * LANES], x[:, HALF + j * LANES:HALF + (j + 1) * LANES])
        ref[lead + (pl.ds(j, n, stride=ROW_WORDS), slice(None))] = w


def _load_rows(ref, lead, row0, n):
    parts = [_unpack_pair(ref[lead + (pl.ds(row0 * ROW_WORDS + j, n, stride=ROW_WORDS), slice(None))])
             for j in range(ROW_WORDS)]
    return jnp.concatenate([p[0] for p in parts] + [p[1] for p in parts], axis=1)


def _proj_kernel(tiles_per_seq, x_ref, g_ref, w_ref, bg_ref, rot_ref, cw_ref, o_ref, carry_s):
    first = (pl.program_id(0) % tiles_per_seq) == 0

    @pl.when(pl.program_id(0) == 0)
    def _():
        carry_s[...] = jnp.zeros_like(carry_s)

    _proj_body(_rms(x_ref[...], g_ref[...]).astype(BF16), first, True, w_ref, bg_ref, rot_ref, cw_ref, o_ref, carry_s)


def _proj_body(xn, first, gates_first, w_ref, bg_ref, rot_ref, cw_ref, o_ref, carry_s):
    tm = xn.shape[0]

    def mm(c0, n):
        return jnp.dot(xn, w_ref[0, :, c0:c0 + n], preferred_element_type=F32)

    ra = rot_ref[:, 0:LANES]
    rb = rot_ref[:, LANES:2 * LANES]
    rc = rot_ref[:, 2 * LANES:3 * LANES]

    def rot(t):
        return t * ra + pltpu.roll(t, LANES - ROT_DIM // 2, 1) * rb + pltpu.roll(t, ROT_DIM // 2, 1) * rc

    nc = 512
    row = lax.broadcasted_iota(jnp.int32, (tm, 1), 0)
    cw = cw_ref[...]

    def gates():
        for j in range(4):
            gt = mm(W_G + j * nc, nc) + bg_ref[:, j * nc:(j + 1) * nc]
            o_ref[:, P_G + j * nc:P_G + (j + 1) * nc] = jax.nn.sigmoid(gt).astype(BF16)

    def queries():
        for j in range(2):
            q = mm(W_Q + j * nc, nc)
            for p in range(nc // LANES):
                t = rot(q[:, p * LANES:(p + 1) * LANES]) * Q_SCALE
                o_ref[:, P_Q + j * nc + p * LANES:P_Q + j * nc + (p + 1) * LANES] = t.astype(BF16)

    def conv():
        for j in range(2):
            cs = slice(j * nc, (j + 1) * nc)
            u = mm(W_CB + j * nc, nc) * mm(W_CX + j * nc, nc)
            prev = carry_s[:, cs]
            h1 = jnp.where(first, 0.0, prev[7:8])
            h2 = jnp.where(first, 0.0, prev[6:7])
            s1 = jnp.where(row == 0, h1, pltpu.roll(u, 1, 0))
            s2 = jnp.where(row == 0, h2, jnp.where(row == 1, h1, pltpu.roll(u, 2, 0)))
            y = cw[2:3, cs] * u + cw[1:2, cs] * s1 + cw[0:1, cs] * s2
            carry_s[:, cs] = u[tm - 8:tm]
            o_ref[:, P_CI + j * nc:P_CI + (j + 1) * nc] = (mm(W_CC + j * nc, nc) * y).astype(BF16)

    def keys_values():
        kv = mm(W_K, 2 * LANES)
        o_ref[:, P_KV:P_KV + LANES] = rot(kv[:, :LANES]).astype(BF16)
        o_ref[:, P_KV + LANES:P_KV + 2 * LANES] = kv[:, LANES:].astype(BF16)

    for group in (gates, queries, conv, keys_values) if gates_first else (conv, queries, keys_values, gates):
        group()


def _proj_specs(w_in, conv_w, tm, layer):
    fixed = lambda i: (0, 0)
    return [
        pl.BlockSpec((1, D_MODEL), fixed),
        pl.BlockSpec((1,) + w_in.shape[1:], lambda i: (layer, 0, 0), pipeline_mode=pl.Buffered(1)),
        pl.BlockSpec((1, 2 * D_MODEL), fixed),
        pl.BlockSpec((tm, 3 * LANES), lambda i: (i, 0)),
        pl.BlockSpec(conv_w.shape, fixed),
    ]


def _proj(x2, g1, w_in, b_gate, rot, conv_w, seq, tm, layer):
    t = x2.shape[0]
    row = lambda i: (i, 0)
    return pl.pallas_call(
        functools.partial(_proj_kernel, seq // tm),
        grid=(t // tm,),
        in_specs=[pl.BlockSpec((tm, D_MODEL), row)] + _proj_specs(w_in, conv_w, tm, layer),
        out_specs=pl.BlockSpec((tm, P_COLS), row),
        out_shape=jax.ShapeDtypeStruct((t, P_COLS), BF16),
        scratch_shapes=[pltpu.VMEM((8, D_MODEL), F32)],
        compiler_params=pltpu.CompilerParams(dimension_semantics=("arbitrary",), vmem_limit_bytes=VMEM_BIG),
        name="proj",
    )(x2, g1, w_in, b_gate, rot, conv_w)


R_E0, R_E1, R_RANK0, R_RANK1, R_W0, R_W1 = range(6)
R_ROW0 = N_GROUPS


def _route_tile(tm, h, g_ref, wt_ref, bt_ref, tri_ref, r_ref, rt_ref, cnt_ref, base_s):
    hn = _rms(h, g_ref[...])
    hi = hn.astype(BF16)
    lo = (hn - hi.astype(F32)).astype(BF16)
    both = lax.dot_general(wt_ref[...], hi, NT, preferred_element_type=F32)
    lt = (both[:LANES] + both[LANES:] + lax.dot_general(wt_ref[0:LANES, :], lo, NT, preferred_element_type=F32)
          + bt_ref[:, 0:1])
    row = lax.broadcasted_iota(jnp.int32, (LANES, tm), 0)
    neg = -jnp.inf

    def first_max(v):
        m = jnp.max(v, axis=0, keepdims=True)
        return m, jnp.min(jnp.where(v == m, row, LANES), axis=0, keepdims=True)

    gl = jnp.where(row < N_GROUPS, lt, neg)
    gmax, gtop = first_max(gl)
    pg = 1.0 / jnp.sum(jnp.exp(gl - gmax), axis=0, keepdims=True)
    in_group = (row >= R_ROW0) & (row < R_ROW0 + N_EXPERTS) & (((row - R_ROW0) >> 3) == gtop)
    el = jnp.where(in_group, lt, neg)
    m1, i1 = first_max(el)
    m2, i2 = first_max(jnp.where(row == i1, neg, el))
    e2 = jnp.exp(m2 - m1)
    den = 1.0 + e2
    w0 = pg * (1.0 / den)
    w1 = pg * (e2 / den)
    hit0 = row == i1
    hit1 = row == i2
    a = jnp.where(hit0 | hit1, 1.0, 0.0)
    before = jnp.dot(a.astype(BF16), tri_ref[...], preferred_element_type=F32) + base_s[:, 0:1]
    rank0 = jnp.sum(jnp.where(hit0, before, 0.0), axis=0, keepdims=True)
    rank1 = jnp.sum(jnp.where(hit1, before, 0.0), axis=0, keepdims=True)
    base_s[...] = base_s[...] + jnp.sum(a, axis=1, keepdims=True)
    cnt_ref[...] = base_s[...]
    vals = ((i1 - R_ROW0).astype(F32), (i2 - R_ROW0).astype(F32), rank0, rank1, w0, w1)
    slab = jnp.zeros((LANES, tm), F32)
    for k, v in enumerate(vals):
        slab = jnp.where(row == k, v, slab)
    r_ref[...] = slab.T
    rt_ref[...] = slab[0:8]


def _mix_kernel(tm, tiles_per_seq, p_ref, hkv_ref, x_ref, sink_ref, wco_ref, wao_ref, wo_ref,
                g2_ref, wt_ref, bt_ref, tri_ref, o_ref, r_ref, rt_ref, cnt_ref, ke_s, ko_s, vta_s, vtb_s, attn_s, base_s):
    i = pl.program_id(0)
    first = (i % tiles_per_seq) == 0
    lo = lax.broadcasted_iota(jnp.int32, (1, LANES), 1) < HEAD_DIM

    @pl.when(i == 0)
    def _():
        base_s[...] = jnp.zeros_like(base_s)

    kv_all = jnp.concatenate([hkv_ref[...], p_ref[:, P_KV:P_KV + 2 * LANES]], axis=0)
    k = kv_all[:, :LANES].astype(F32)
    kr = pltpu.roll(k, HEAD_DIM, 1)
    ke_s[0] = jnp.where(lo, k, 0.0).astype(BF16)
    ko_s[0] = jnp.where(lo, 0.0, kr).astype(BF16)
    ke_s[1] = jnp.where(lo, kr, 0.0).astype(BF16)
    ko_s[1] = jnp.where(lo, 0.0, k).astype(BF16)
    v = kv_all[:, LANES:].astype(F32)
    n_keys = HALO + tm
    for j in range(n_keys // LANES):
        vta_s[:, j * LANES:(j + 1) * LANES] = v[j * LANES:(j + 1) * LANES].T.astype(BF16)
    for j in range(n_keys // LANES - 1):
        vtb_s[:, j * LANES:(j + 1) * LANES] = v[CHUNK + j * LANES:CHUNK + (j + 1) * LANES].T.astype(BF16)
    vtb_s[:, n_keys - LANES:n_keys - CHUNK] = v[n_keys - CHUNK:n_keys].T.astype(BF16)

    krow = lax.broadcasted_iota(jnp.int32, (2 * KEYS, 1), 0)
    krow = jnp.where(krow >= KEYS, krow - KEYS, krow)

    def scores(c, g):
        r0 = c * CHUNK
        qbase = P_Q + g * 4 * LANES
        q = jnp.concatenate(
            [p_ref[r0:r0 + CHUNK, qbase + p * LANES:qbase + (p + 1) * LANES] for p in range(4)], axis=0)
        kk = jnp.concatenate([ke_s[g, r0:r0 + KEYS, :], ko_s[g, r0:r0 + KEYS, :]], axis=0)
        st = lax.dot_general(kk, q, NT, preferred_element_type=F32)
        if r0 < HALO:
            nbad = jnp.where(first, HALO - r0, 0)
            st = jnp.where(krow < nbad, -jnp.inf, st)
        return st

    def finish(c, g, st):
        r0 = c * CHUNK
        vt_s, v0 = (vta_s, r0) if c % 2 == 0 else (vtb_s, r0 - CHUNK)
        vt = vt_s[g * HEAD_DIM:(g + 1) * HEAD_DIM, v0:v0 + KEYS]
        outs = []
        for par in range(2):
            s = st[par * KEYS:(par + 1) * KEYS]
            sc = sink_ref[2 * g + par:2 * g + par + 1, :]
            m = jnp.maximum(jnp.max(s, axis=0, keepdims=True), sc)
            p = jnp.exp2(s - m)
            l = jnp.sum(p, axis=0, keepdims=True) + jnp.exp2(sc - m)
            outs.append(jnp.dot(vt, p.astype(BF16), preferred_element_type=F32) / l)
        ot = jnp.concatenate(outs, axis=0)
        for p2 in range(2):
            blk = ot[:, p2 * LANES:(p2 + 1) * LANES].T
            for h in range(2):
                c0 = (g * 4 + 2 * p2 + h) * LANES
                attn_s[r0:r0 + CHUNK, c0:c0 + LANES] = blk[h * CHUNK:(h + 1) * CHUNK].astype(BF16)

    units = [(c, g) for c in range(tm // CHUNK) for g in range(N_KV_HEADS)]
    pending = [scores(*u) for u in units[:SCORES_AHEAD]]
    conv_blocks = []
    for n, (c, g) in enumerate(units):
        st = pending.pop(0)
        if n + SCORES_AHEAD < len(units):
            pending.append(scores(*units[n + SCORES_AHEAD]))
        finish(c, g, st)
        if n % 4 == 3:
            cb0 = (n // 4) * (D_MODEL // 4)
            conv_blocks.append(jnp.dot(p_ref[:, P_CI:P_CI + D_MODEL], wco_ref[0, :, cb0:cb0 + D_MODEL // 4],
                                       preferred_element_type=F32))

    conv_out = jnp.concatenate(conv_blocks, axis=1)
    attn_out = jnp.dot(attn_s[...], wao_ref[0], preferred_element_type=F32)
    gc = p_ref[:, P_G:P_G + D_MODEL].astype(F32)
    ga = p_ref[:, P_G + D_MODEL:P_G + 2 * D_MODEL].astype(F32)
    merged = (gc * conv_out + ga * attn_out).astype(BF16)
    h_new = x_ref[...] + jnp.dot(merged, wo_ref[0], preferred_element_type=F32)
    o_ref[...] = h_new
    _route_tile(tm, h_new, g2_ref, wt_ref, bt_ref, tri_ref, r_ref, rt_ref, cnt_ref, base_s)


def _mix(p, x2, sink_tab, w_co, w_ao, w_o, g2, wt_router, bt_router, seq, tm, layer):
    t = x2.shape[0]
    row = lambda i: (i, 0)
    fixed = lambda i: (0, 0)
    sq = (1, D_MODEL, D_MODEL)
    wsel = lambda i: (layer, 0, 0)
    n_keys = HALO + tm
    return pl.pallas_call(
        functools.partial(_mix_kernel, tm, seq // tm),
        grid=(t // tm,),
        in_specs=[
            pl.BlockSpec((tm, P_COLS), row),
            pl.BlockSpec((HALO, 2 * LANES), lambda i: (jnp.maximum(i * (tm // HALO) - 1, 0), P_KV // (2 * LANES))),
            pl.BlockSpec((tm, D_MODEL), row),
            pl.BlockSpec(sink_tab.shape, fixed),
            pl.BlockSpec(sq, wsel),
            pl.BlockSpec(sq, wsel),
            pl.BlockSpec(sq, wsel),
            pl.BlockSpec((1, D_MODEL), fixed),
            pl.BlockSpec((2 * LANES, D_MODEL), fixed),
            pl.BlockSpec((LANES, LANES), fixed),
            pl.BlockSpec((tm, tm), fixed),
        ],
        out_specs=[
            pl.BlockSpec((tm, D_MODEL), row),
            pl.BlockSpec((tm, LANES), row),
            pl.BlockSpec((8, tm), lambda i: (0, i)),
            pl.BlockSpec((LANES, LANES), fixed),
        ],
        out_shape=[
            jax.ShapeDtypeStruct((t, D_MODEL), F32),
            jax.ShapeDtypeStruct((t, LANES), F32),
            jax.ShapeDtypeStruct((8, t), F32),
            jax.ShapeDtypeStruct((LANES, LANES), F32),
        ],
        scratch_shapes=[
            pltpu.VMEM((N_KV_HEADS, n_keys, LANES), BF16),
            pltpu.VMEM((N_KV_HEADS, n_keys, LANES), BF16),
            pltpu.VMEM((2 * HEAD_DIM, n_keys), BF16),
            pltpu.VMEM((2 * HEAD_DIM, n_keys), BF16),
            pltpu.VMEM((tm, D_MODEL), BF16),
            pltpu.VMEM((LANES, LANES), F32),
        ],
        compiler_params=pltpu.CompilerParams(dimension_semantics=("arbitrary",), vmem_limit_bytes=VMEM_BIG),
        name="mix",
    )(p, p, x2, sink_tab, w_co, w_ao, w_o, g2, wt_router, bt_router, jnp.triu(jnp.ones((tm, tm), BF16), 1))


def _rows_copy(src_ref, dst_ref, sem, n):
    return pltpu.make_async_copy(src_ref.at[pl.ds(0, n)], dst_ref.at[pl.ds(0, n)], sem)


def _dispatch_kernel(tm, n_tiles, n_blocks, zm_ref, d_ref, h_ref, g_ref, xs_ref, hn_s, zbuf, zsem, sem):
    i = pl.program_id(0)
    blk_rows = MOE_BLOCK * ROW_WORDS

    @pl.when(i == 0)
    def _():
        zbuf[...] = jnp.zeros_like(zbuf)

        def zero_copy(b):
            r = pl.multiple_of(b * blk_rows, blk_rows)
            return pltpu.make_async_copy(zbuf, xs_ref.at[pl.ds(r, blk_rows)], zsem)

        def start(b, c):
            @pl.when(zm_ref[b] == 1)
            def _():
                zero_copy(b).start()
            return c

        def wait(b, c):
            @pl.when(zm_ref[b] == 1)
            def _():
                zero_copy(b).wait()
            return c

        lax.fori_loop(0, n_blocks, start, 0)
        lax.fori_loop(0, n_blocks, wait, 0)

    slot = i % 2
    _store_rows(hn_s, (slot,), tm, _rms(h_ref[...], g_ref[...]))

    def body(t, c):
        src = hn_s.at[slot, pl.ds(pl.multiple_of(t * ROW_WORDS, ROW_WORDS), ROW_WORDS)]
        for k in range(2):
            d = pl.multiple_of(d_ref[k * tm + t] * ROW_WORDS, ROW_WORDS)
            pltpu.make_async_copy(src, xs_ref.at[pl.ds(d, ROW_WORDS)], sem.at[slot]).start(priority=k)
        return c

    lax.fori_loop(0, tm, body, 0, unroll=8)

    def drain(s):
        for _ in range(2):
            _rows_copy(hn_s.at[s], xs_ref, sem.at[s], tm * ROW_WORDS).wait()

    @pl.when(i > 0)
    def _():
        drain(1 - slot)

    @pl.when(i == n_tiles - 1)
    def _():
        drain(slot)


def _dispatch(h2, g2, dest, zero_mask, n_rows, tm):
    n_tiles = h2.shape[0] // tm
    n_blocks = n_rows // MOE_BLOCK
    return pl.pallas_call(
        functools.partial(_dispatch_kernel, tm, n_tiles, n_blocks),
        grid_spec=pltpu.PrefetchScalarGridSpec(
            num_scalar_prefetch=1,
            grid=(n_tiles,),
            in_specs=[
                pl.BlockSpec((2 * tm,), lambda i, zm: (i,), memory_space=pltpu.SMEM),
                pl.BlockSpec((tm, D_MODEL), lambda i, zm: (i, 0)),
                pl.BlockSpec((1, D_MODEL), lambda i, zm: (0, 0)),
            ],
            out_specs=pl.BlockSpec(memory_space=pl.ANY),
            scratch_shapes=[
                pltpu.VMEM((2, tm * ROW_WORDS, LANES), U32),
                pltpu.VMEM((MOE_BLOCK * ROW_WORDS, LANES), U32),
                pltpu.SemaphoreType.DMA(()),
                pltpu.SemaphoreType.DMA((2,)),
            ],
        ),
        out_shape=jax.ShapeDtypeStruct((n_rows * ROW_WORDS, LANES), U32),
        compiler_params=pltpu.CompilerParams(dimension_semantics=("arbitrary",)),
        name="dispatch",
    )(zero_mask, dest, h2, g2)


def _expert_kernel(layer, be_ref, nw_ref, nx_ref, sl_ref, nu_ref, x_ref, w1_hbm, w3_hbm, w2_hbm, o_ref,
                   w1_f, w3_f, w2_f, w1_s, w3_s, w2_s, sem):
    b0 = STEP_BLOCKS * pl.program_id(0)
    n_used = nu_ref[0]

    def fetch(e, slot):
        return (pltpu.make_async_copy(w1_hbm.at[layer, e], w1_f.at[slot], sem.at[slot]),
                pltpu.make_async_copy(w3_hbm.at[layer, e], w3_f.at[slot], sem.at[slot]),
                pltpu.make_async_copy(w2_hbm.at[layer, e], w2_f.at[slot], sem.at[slot]))

    def switch(b):
        @pl.when((b < n_used) & (nw_ref[b] == 1))
        def _():
            slot = sl_ref[b]
            stage = slot % 2

            @pl.when(b == 0)
            def _():
                for c in fetch(be_ref[0], 0):
                    c.start()

            for c in fetch(be_ref[b], stage):
                c.wait()
            w1_s[slot] = w1_f[stage].astype(BF16)
            w3_s[slot] = w3_f[stage].astype(BF16)
            w2_s[slot] = w2_f[stage].astype(BF16)

            @pl.when(nx_ref[b] >= 0)
            def _():
                for c in fetch(nx_ref[b], 1 - stage):
                    c.start()

    @pl.when(b0 < n_used)
    def _():
        for k in range(STEP_BLOCKS):
            switch(b0 + k)
        nb = STEP_BLOCKS
        slots = [sl_ref[b0 + k] for k in range(nb)]
        xs = [_load_rows(x_ref, (), k * MOE_BLOCK, MOE_BLOCK).astype(BF16) for k in range(nb)]
        h1 = [jnp.dot(xs[k], w1_s[slots[k]], preferred_element_type=F32) for k in range(nb)]
        h3 = [jnp.dot(xs[k], w3_s[slots[k]], preferred_element_type=F32) for k in range(nb)]
        for k in range(nb):
            hid = (h1[k] * jax.nn.sigmoid(h1[k]) * h3[k]).astype(BF16)
            y = jnp.dot(hid, w2_s[slots[k]], preferred_element_type=F32)
            for j in range(ROW_WORDS):
                w = _pack_pair(y[:, j * LANES:(j + 1) * LANES], y[:, HALF + j * LANES:HALF + (j + 1) * LANES])
                o_ref[pl.ds(k * MOE_BLOCK * ROW_WORDS + j, MOE_BLOCK, stride=ROW_WORDS), :] = w

    @pl.when(b0 >= n_used)
    def _():
        o_ref[...] = jnp.zeros_like(o_ref)


def _experts(xs, plan, w1, w3, w2, layer):
    n_rows = xs.shape[0] // ROW_WORDS
    blk = (STEP_BLOCKS * MOE_BLOCK * ROW_WORDS, LANES)
    any_spec = pl.BlockSpec(memory_space=pl.ANY)
    return pl.pallas_call(
        functools.partial(_expert_kernel, layer),
        grid_spec=pltpu.PrefetchScalarGridSpec(
            num_scalar_prefetch=5,
            grid=(n_rows // (STEP_BLOCKS * MOE_BLOCK),),
            in_specs=[
                pl.BlockSpec(blk, lambda p, be, nw, nx, sl, nu: (jnp.minimum(p, (nu[0] - 1) // STEP_BLOCKS), 0)),
                any_spec, any_spec, any_spec,
            ],
            out_specs=pl.BlockSpec(blk, lambda p, be, nw, nx, sl, nu: (p, 0)),
            scratch_shapes=[
                pltpu.VMEM((2, D_MODEL, EXPERT_FF), F32),
                pltpu.VMEM((2, D_MODEL, EXPERT_FF), F32),
                pltpu.VMEM((2, EXPERT_FF, D_MODEL), F32),
                pltpu.VMEM((W_SLOTS, D_MODEL, EXPERT_FF), BF16),
                pltpu.VMEM((W_SLOTS, D_MODEL, EXPERT_FF), BF16),
                pltpu.VMEM((W_SLOTS, EXPERT_FF, D_MODEL), BF16),
                pltpu.SemaphoreType.DMA((2,)),
            ],
        ),
        out_shape=jax.ShapeDtypeStruct(xs.shape, U32),
        compiler_params=pltpu.CompilerParams(dimension_semantics=("arbitrary",), vmem_limit_bytes=VMEM_BIG),
        name="experts",
    )(*plan, xs, w1, w3, w2)


def _issue_row_gathers(tm, d_ref, ys_ref, ybuf, sem, slot, unrolled):
    def one(t):
        for k in range(2):
            d = pl.multiple_of(d_ref[k * tm + t] * ROW_WORDS, ROW_WORDS)
            r = (k * tm + t) * ROW_WORDS
            if not isinstance(r, int):
                r = pl.multiple_of(r, ROW_WORDS)
            pltpu.make_async_copy(
                ys_ref.at[pl.ds(d, ROW_WORDS)], ybuf.at[slot, pl.ds(r, ROW_WORDS)], sem.at[slot]).start(priority=k)

    if unrolled:
        for t in range(tm):
            one(t)
    else:
        def body(t, c):
            one(t)
            return c

        lax.fori_loop(0, tm, body, 0, unroll=8)


def _combined_tile(tm, r_ref, h_ref, ys_ref, ybuf, sem, slot):
    _rows_copy(ys_ref, ybuf.at[slot], sem.at[slot], 2 * tm * ROW_WORDS).wait()
    r = r_ref[...]
    w0 = r[:, R_W0:R_W0 + 1]
    w1 = r[:, R_W1:R_W1 + 1]
    return h_ref[...] + (w0 * _load_rows(ybuf, (slot,), 0, tm) + w1 * _load_rows(ybuf, (slot,), tm, tm))


def _combine_kernel(tm, n_tiles, dcur_ref, dnext_ref, r_ref, h_ref, fg_ref, ys_ref, o_ref, ybuf, sem):
    i = pl.program_id(0)

    @pl.when(i == 0)
    def _():
        _issue_row_gathers(tm, dcur_ref, ys_ref, ybuf, sem, 0, False)

    @pl.when(i + 1 < n_tiles)
    def _():
        _issue_row_gathers(tm, dnext_ref, ys_ref, ybuf, sem, (i + 1) % 2, False)

    o_ref[...] = _rms(_combined_tile(tm, r_ref, h_ref, ys_ref, ybuf, sem, i % 2), fg_ref[...])


def _combine_specs(tm, n_tiles):
    row = lambda i: (i, 0)
    return [
        pl.BlockSpec((2 * tm,), lambda i: (i,), memory_space=pltpu.SMEM),
        pl.BlockSpec((2 * tm,), lambda i: (jnp.minimum(i + 1, n_tiles - 1),), memory_space=pltpu.SMEM),
        pl.BlockSpec((tm, LANES), row),
        pl.BlockSpec((tm, D_MODEL), row),
    ]


def _combine_scratch(tm):
    return [pltpu.VMEM((2, 2 * tm * ROW_WORDS, LANES), U32), pltpu.SemaphoreType.DMA((2,))]


def _combine_final(ys, dest, route, h2, final_g, tm):
    t = h2.shape[0]
    n_tiles = t // tm
    return pl.pallas_call(
        functools.partial(_combine_kernel, tm, n_tiles),
        grid=(n_tiles,),
        in_specs=_combine_specs(tm, n_tiles)
        + [pl.BlockSpec((1, D_MODEL), lambda i: (0, 0)), pl.BlockSpec(memory_space=pl.ANY)],
        out_specs=pl.BlockSpec((tm, D_MODEL), lambda i: (i, 0)),
        out_shape=jax.ShapeDtypeStruct((t, D_MODEL), F32),
        scratch_shapes=_combine_scratch(tm),
        compiler_params=pltpu.CompilerParams(dimension_semantics=("arbitrary",)),
        name="combine",
    )(dest, dest, route, h2, final_g, ys)


def _combine_proj_kernel(tm, n_tiles, tiles_per_seq, dcur_ref, dnext_ref, r_ref, hm_ref, g_ref, w_ref, bg_ref,
                         rot_ref, cw_ref, ys_ref, o_ref, h_ref, ybuf, sem, carry_s):
    i = pl.program_id(0)
    slot = i % 2
    first = (i % tiles_per_seq) == 0

    @pl.when(i == 0)
    def _():
        carry_s[...] = jnp.zeros_like(carry_s)
        _issue_row_gathers(tm, dcur_ref, ys_ref, ybuf, sem, 0, False)

    h = _combined_tile(tm, r_ref, hm_ref, ys_ref, ybuf, sem, slot)
    h_ref[...] = h
    _issue_row_gathers(tm, dnext_ref, ys_ref, ybuf, sem, 1 - slot, True)
    _proj_body(_rms(h, g_ref[...]).astype(BF16), first, False, w_ref, bg_ref, rot_ref, cw_ref, o_ref, carry_s)

    @pl.when(i == n_tiles - 1)
    def _():
        _rows_copy(ys_ref, ybuf.at[1 - slot], sem.at[1 - slot], 2 * tm * ROW_WORDS).wait()


def _combine_proj(ys, dest, route, h2, g1, w_in, b_gate, rot, conv_w, seq, tm, layer):
    t = h2.shape[0]
    n_tiles = t // tm
    row = lambda i: (i, 0)
    return pl.pallas_call(
        functools.partial(_combine_proj_kernel, tm, n_tiles, seq // tm),
        grid=(n_tiles,),
        in_specs=_combine_specs(tm, n_tiles) + _proj_specs(w_in, conv_w, tm, layer)
        + [pl.BlockSpec(memory_space=pl.ANY)],
        out_specs=[pl.BlockSpec((tm, P_COLS), row), pl.BlockSpec((tm, D_MODEL), row)],
        out_shape=[jax.ShapeDtypeStruct((t, P_COLS), BF16), jax.ShapeDtypeStruct((t, D_MODEL), F32)],
        scratch_shapes=_combine_scratch(tm) + [pltpu.VMEM((8, D_MODEL), F32)],
        compiler_params=pltpu.CompilerParams(dimension_semantics=("arbitrary",), vmem_limit_bytes=VMEM_BIG),
        name="combine_proj",
    )(dest, dest, route, h2, g1, w_in, b_gate, rot, conv_w, ys)


def _rotary_table(positions):
    half = ROT_DIM // 2
    inv = ROPE_THETA ** (-jnp.arange(0, ROT_DIM, 2, dtype=F32) / ROT_DIM)
    ang = positions.reshape(-1).astype(F32)[:, None] * inv
    cs1 = jnp.concatenate([jnp.cos(ang), jnp.sin(ang), jnp.ones((ang.shape[0], 1), F32)], axis=1)
    lane = jnp.arange(LANES) % HEAD_DIM
    j = jnp.arange(2 * half + 1)[:, None]
    a = jnp.where(lane < half, j == lane, jnp.where(lane < ROT_DIM, j == lane - half, j == 2 * half))
    b = -((lane < half) & (j == half + lane)).astype(F32)
    c = ((lane >= half) & (lane < ROT_DIM) & (j == lane)).astype(F32)
    place = jnp.concatenate([a.astype(F32), b, c], axis=1)
    return jnp.dot(cs1, place, precision=lax.Precision.HIGHEST)


def _sink_table(sinks):
    s = (sinks.astype(F32) * LOG2E).reshape(N_KV_HEADS, 4, 2).transpose(0, 2, 1)
    s = jnp.broadcast_to(s[..., None], (N_KV_HEADS, 2, 4, CHUNK)).reshape(2 * N_KV_HEADS, 4 * CHUNK)
    return jnp.concatenate([s, jnp.zeros((8 - 2 * N_KV_HEADS, 4 * CHUNK), F32)], axis=0)


def _router_params(w_group, b_group, w_route, b_route):
    d = w_group.shape[0]
    pad = LANES - N_GROUPS - N_EXPERTS
    wt = jnp.concatenate([w_group.T, w_route.transpose(0, 2, 1).reshape(N_EXPERTS, d), jnp.zeros((pad, d), F32)], axis=0)
    hi = wt.astype(BF16)
    lo = (wt - hi.astype(F32)).astype(BF16)
    bias = jnp.concatenate([b_group, b_route.reshape(-1), jnp.zeros((pad,), F32)])
    return jnp.concatenate([hi, lo], axis=0), jnp.broadcast_to(bias[:, None], (LANES, LANES))


def _moe_plan(route_t, counts, n_tok, tm):
    n_rows = -(-(2 * n_tok) // MOE_BLOCK) * MOE_BLOCK + N_EXPERTS * MOE_BLOCK
    n_blocks = n_rows // MOE_BLOCK
    ids = jnp.arange(N_EXPERTS, dtype=jnp.int32)
    cnt = counts[R_ROW0:R_ROW0 + N_EXPERTS, 0].astype(jnp.int32)
    padded = (cnt + MOE_BLOCK - 1) // MOE_BLOCK * MOE_BLOCK
    pad_end = jnp.sum(jnp.where(ids[None, :] <= ids[:, None], padded[None, :], 0), axis=1)
    pad_start = pad_end - padded
    e = route_t[R_E0:R_E1 + 1].astype(jnp.int32)
    rank = route_t[R_RANK0:R_RANK1 + 1].astype(jnp.int32)
    dest = jnp.sum(jnp.where(e[..., None] == ids, pad_start, 0), axis=-1) + rank
    dest = dest.reshape(2, n_tok // tm, tm).transpose(1, 0, 2).reshape(-1)
    n_used = pad_end[-1] // MOE_BLOCK
    blk = jnp.arange(n_blocks, dtype=jnp.int32)
    be = jnp.minimum(jnp.sum((pad_end[None, :] <= blk[:, None] * MOE_BLOCK).astype(jnp.int32), axis=1), N_EXPERTS - 1)
    be_last = jnp.sum(jnp.where(blk == n_used - 1, be, 0))
    be = jnp.where(blk < n_used, be, be_last)
    new_expert = jnp.concatenate([jnp.ones((1,), jnp.int32), (be[1:] != be[:-1]).astype(jnp.int32)])
    later = blk[None, :] > blk[:, None]
    seq_slot = (jnp.sum(jnp.where(later.T | (blk[None, :] == blk[:, None]), new_expert[None, :], 0), axis=1) - 1) % W_SLOTS
    next_start = jnp.min(jnp.where(later & (new_expert[None, :] == 1), blk[None, :], n_blocks), axis=1)
    next_expert = jnp.sum(jnp.where(blk[None, :] == next_start[:, None], be[None, :], 0), axis=1)
    next_expert = jnp.where(next_start < n_blocks, next_expert, -1)
    last = pad_end // MOE_BLOCK - 1
    is_last = jnp.any((blk[:, None] == last[None, :]) & (padded[None, :] > 0), axis=1)
    zero_mask = (is_last | (blk >= n_used)).astype(jnp.int32)
    plan = (be, new_expert, next_expert.astype(jnp.int32), seq_slot.astype(jnp.int32), n_used.astype(jnp.int32).reshape(1))
    return n_rows, dest, plan, zero_mask


def kernel(x, positions, norm1_g, w_in, b_gate, conv_w, sinks, w_conv_out, w_attn_out, w_o, norm2_g, w_group, b_group, w_route, b_route, w1, w3, w2, final_g):
    b, s, d = x.shape
    t = b * s
    depth = w_in.shape[0]
    tm = ROW_TILE
    rot = _rotary_table(positions)
    h = x.reshape(t, d)
    ys = dest = route = None
    w_in_b, w_co_b, w_ao_b, w_o_b = (w.astype(BF16) for w in (w_in, w_conv_out, w_attn_out, w_o))
    for l in range(depth):
        if l == 0:
            p = _proj(h, norm1_g[l][None], w_in_b, b_gate[l][None], rot, conv_w[l], s, PROJ_TILE, l)
        else:
            p, h = _combine_proj(ys, dest, route, h, norm1_g[l][None], w_in_b, b_gate[l][None], rot, conv_w[l], s, tm, l)
        g2 = norm2_g[l][None]
        wt_router, bt_router = _router_params(w_group[l], b_group[l], w_route[l], b_route[l])
        h, route, route_t, counts = _mix(p, h, _sink_table(sinks[l]), w_co_b, w_ao_b, w_o_b, g2, wt_router, bt_router,
                                         s, tm, l)
        n_rows, dest, plan, zero_mask = _moe_plan(route_t, counts, t, tm)
        xs = _dispatch(h, g2, dest, zero_mask, n_rows, tm)
        ys = _experts(xs, plan, w1, w3, w2, l)
    return _combine_final(ys, dest, route, h, final_g[None], tm).reshape(b, s, d)
```

```python
import functools

import jax
import jax.numpy as jnp
from jax import lax
from jax.experimental import pallas as pl
from jax.experimental.pallas import tpu as pltpu

F32 = jnp.float32
BF16 = jnp.bfloat16
U32 = jnp.uint32

D_MODEL = 1024
HALF = D_MODEL // 2
RMS_EPS = 1e-5
CHUNK = 64
WINDOW_CHUNKS = 2
HEAD_DIM = 64
N_Q_HEADS = 16
N_KV_HEADS = 2
ROT_DIM = 16
ROPE_THETA = 500000.0
N_GROUPS = 8
EXPERTS_PER_GROUP = 8
N_EXPERTS = 64
EXPERT_FF = 512
MOE_BLOCK = 256
STEP_BLOCKS = 4
W_SLOTS = 4
LANES = 128

W_CB, W_CC, W_CX, W_Q, W_K, W_G = 0, 1024, 2048, 3072, 4096, 4352
P_CI, P_Q, P_KV, P_G = 0, 1024, 2048, 2304
P_COLS = 4352
HALO = WINDOW_CHUNKS * CHUNK
KEYS = HALO + CHUNK
LOG2E = 1.4426950408889634
Q_SCALE = HEAD_DIM ** -0.5 * LOG2E
SCORES_AHEAD = 4

ROW_TILE = 512
PROJ_TILE = 1024
VMEM_BIG = 56 * 1024 * 1024
NT = (((1,), (1,)), ((), ()))


def _rms(x, g):
    ms = jnp.mean(x * x, axis=-1, keepdims=True)
    return x * lax.rsqrt(ms + RMS_EPS) * g


def _pack_pair(lo, hi):
    def rnd(x):
        return lax.bitcast_convert_type(x, U32) + U32(0x8000)

    return (rnd(hi) & U32(0xFFFF0000)) | (rnd(lo) >> 16)


def _unpack_pair(w):
    lo = lax.bitcast_convert_type(w << 16, F32)
    hi = lax.bitcast_convert_type(w & U32(0xFFFF0000), F32)
    return lo, hi


ROW_WORDS = HALF // LANES


def _store_rows(ref, lead, n, x):
    for j in range(ROW_WORDS):
        w = _pack_pair(x[:, j * LANES:(j + 1) * LANES], x[:, HALF + j * LANES:HALF + (j + 1) * LANES])
        ref[lead + (pl.ds(j, n, stride=ROW_WORDS), slice(None))] = w


def _load_rows(ref, lead, row0, n):
    parts = [_unpack_pair(ref[lead + (pl.ds(row0 * ROW_WORDS + j, n, stride=ROW_WORDS), slice(None))])
             for j in range(ROW_WORDS)]
    return jnp.concatenate([p[0] for p in parts] + [p[1] for p in parts], axis=1)


def _proj_kernel(tiles_per_seq, x_ref, g_ref, w_ref, bg_ref, rot_ref, cw_ref, o_ref, carry_s):
    first = (pl.program_id(0) % tiles_per_seq) == 0

    @pl.when(pl.program_id(0) == 0)
    def _():
        carry_s[...] = jnp.zeros_like(carry_s)

    _proj_body(_rms(x_ref[...], g_ref[...]).astype(BF16), first, True, w_ref, bg_ref, rot_ref, cw_ref, o_ref, carry_s)


def _proj_body(xn, first, gates_first, w_ref, bg_ref, rot_ref, cw_ref, o_ref, carry_s):
    tm = xn.shape[0]

    def mm(c0, n):
        return jnp.dot(xn, w_ref[0, :, c0:c0 + n], preferred_element_type=F32)

    ra = rot_ref[:, 0:LANES]
    rb = rot_ref[:, LANES:2 * LANES]
    rc = rot_ref[:, 2 * LANES:3 * LANES]

    def rot(t):
        return t * ra + pltpu.roll(t, LANES - ROT_DIM // 2, 1) * rb + pltpu.roll(t, ROT_DIM // 2, 1) * rc

    nc = 512
    row = lax.broadcasted_iota(jnp.int32, (tm, 1), 0)
    cw = cw_ref[...]

    def gates():
        for j in range(4):
            gt = mm(W_G + j * nc, nc) + bg_ref[:, j * nc:(j + 1) * nc]
            o_ref[:, P_G + j * nc:P_G + (j + 1) * nc] = jax.nn.sigmoid(gt).astype(BF16)

    def queries():
        for j in range(2):
            q = mm(W_Q + j * nc, nc)
            for p in range(nc // LANES):
                t = rot(q[:, p * LANES:(p + 1) * LANES]) * Q_SCALE
                o_ref[:, P_Q + j * nc + p * LANES:P_Q + j * nc + (p + 1) * LANES] = t.astype(BF16)

    def conv():
        for j in range(2):
            cs = slice(j * nc, (j + 1) * nc)
            u = mm(W_CB + j * nc, nc) * mm(W_CX + j * nc, nc)
            prev = carry_s[:, cs]
            h1 = jnp.where(first, 0.0, prev[7:8])
            h2 = jnp.where(first, 0.0, prev[6:7])
            s1 = jnp.where(row == 0, h1, pltpu.roll(u, 1, 0))
            s2 = jnp.where(row == 0, h2, jnp.where(row == 1, h1, pltpu.roll(u, 2, 0)))
            y = cw[2:3, cs] * u + cw[1:2, cs] * s1 + cw[0:1, cs] * s2
            carry_s[:, cs] = u[tm - 8:tm]
            o_ref[:, P_CI + j * nc:P_CI + (j + 1) * nc] = (mm(W_CC + j * nc, nc) * y).astype(BF16)

    def keys_values():
        kv = mm(W_K, 2 * LANES)
        o_ref[:, P_KV:P_KV + LANES] = rot(kv[:, :LANES]).astype(BF16)
        o_ref[:, P_KV + LANES:P_KV + 2 * LANES] = kv[:, LANES:].astype(BF16)

    for group in (gates, queries, conv, keys_values) if gates_first else (conv, queries, keys_values, gates):
        group()


def _proj_specs(w_in, conv_w, tm, layer):
    fixed = lambda i: (0, 0)
    return [
        pl.BlockSpec((1, D_MODEL), fixed),
        pl.BlockSpec((1,) + w_in.shape[1:], lambda i: (layer, 0, 0), pipeline_mode=pl.Buffered(1)),
        pl.BlockSpec((1, 2 * D_MODEL), fixed),
        pl.BlockSpec((tm, 3 * LANES), lambda i: (i, 0)),
        pl.BlockSpec(conv_w.shape, fixed),
    ]


def _proj(x2, g1, w_in, b_gate, rot, conv_w, seq, tm, layer):
    t = x2.shape[0]
    row = lambda i: (i, 0)
    return pl.pallas_call(
        functools.partial(_proj_kernel, seq // tm),
        grid=(t // tm,),
        in_specs=[pl.BlockSpec((tm, D_MODEL), row)] + _proj_specs(w_in, conv_w, tm, layer),
        out_specs=pl.BlockSpec((tm, P_COLS), row),
        out_shape=jax.ShapeDtypeStruct((t, P_COLS), BF16),
        scratch_shapes=[pltpu.VMEM((8, D_MODEL), F32)],
        compiler_params=pltpu.CompilerParams(dimension_semantics=("arbitrary",), vmem_limit_bytes=VMEM_BIG),
        name="proj",
    )(x2, g1, w_in, b_gate, rot, conv_w)


R_E0, R_E1, R_RANK0, R_RANK1, R_W0, R_W1 = range(6)
R_ROW0 = N_GROUPS


def _route_tile(tm, h, g_ref, wt_ref, bt_ref, tri_ref, r_ref, rt_ref, cnt_ref, base_s):
    hn = _rms(h, g_ref[...])
    hi = hn.astype(BF16)
    lo = (hn - hi.astype(F32)).astype(BF16)
    both = lax.dot_general(wt_ref[...], hi, NT, preferred_element_type=F32)
    lt = (both[:LANES] + both[LANES:] + lax.dot_general(wt_ref[0:LANES, :], lo, NT, preferred_element_type=F32)
          + bt_ref[:, 0:1])
    row = lax.broadcasted_iota(jnp.int32, (LANES, tm), 0)
    neg = -jnp.inf

    def first_max(v):
        m = jnp.max(v, axis=0, keepdims=True)
        return m, jnp.min(jnp.where(v == m, row, LANES), axis=0, keepdims=True)

    gl = jnp.where(row < N_GROUPS, lt, neg)
    gmax, gtop = first_max(gl)
    pg = 1.0 / jnp.sum(jnp.exp(gl - gmax), axis=0, keepdims=True)
    in_group = (row >= R_ROW0) & (row < R_ROW0 + N_EXPERTS) & (((row - R_ROW0) >> 3) == gtop)
    el = jnp.where(in_group, lt, neg)
    m1, i1 = first_max(el)
    m2, i2 = first_max(jnp.where(row == i1, neg, el))
    e2 = jnp.exp(m2 - m1)
    den = 1.0 + e2
    w0 = pg * (1.0 / den)
    w1 = pg * (e2 / den)
    hit0 = row == i1
    hit1 = row == i2
    a = jnp.where(hit0 | hit1, 1.0, 0.0)
    before = jnp.dot(a.astype(BF16), tri_ref[...], preferred_element_type=F32) + base_s[:, 0:1]
    rank0 = jnp.sum(jnp.where(hit0, before, 0.0), axis=0, keepdims=True)
    rank1 = jnp.sum(jnp.where(hit1, before, 0.0), axis=0, keepdims=True)
    base_s[...] = base_s[...] + jnp.sum(a, axis=1, keepdims=True)
    cnt_ref[...] = base_s[...]
    vals = ((i1 - R_ROW0).astype(F32), (i2 - R_ROW0).astype(F32), rank0, rank1, w0, w1)
    slab = jnp.zeros((LANES, tm), F32)
    for k, v in enumerate(vals):
        slab = jnp.where(row == k, v, slab)
    r_ref[...] = slab.T
    rt_ref[...] = slab[0:8]


def _mix_kernel(tm, tiles_per_seq, p_ref, hkv_ref, x_ref, sink_ref, wco_ref, wao_ref, wo_ref,
                g2_ref, wt_ref, bt_ref, tri_ref, o_ref, r_ref, rt_ref, cnt_ref, ke_s, ko_s, vta_s, vtb_s, attn_s, base_s):
    i = pl.program_id(0)
    first = (i % tiles_per_seq) == 0
    lo = lax.broadcasted_iota(jnp.int32, (1, LANES), 1) < HEAD_DIM

    @pl.when(i == 0)
    def _():
        base_s[...] = jnp.zeros_like(base_s)

    kv_all = jnp.concatenate([hkv_ref[...], p_ref[:, P_KV:P_KV + 2 * LANES]], axis=0)
    k = kv_all[:, :LANES].astype(F32)
    kr = pltpu.roll(k, HEAD_DIM, 1)
    ke_s[0] = jnp.where(lo, k, 0.0).astype(BF16)
    ko_s[0] = jnp.where(lo, 0.0, kr).astype(BF16)
    ke_s[1] = jnp.where(lo, kr, 0.0).astype(BF16)
    ko_s[1] = jnp.where(lo, 0.0, k).astype(BF16)
    v = kv_all[:, LANES:].astype(F32)
    n_keys = HALO + tm
    for j in range(n_keys // LANES):
        vta_s[:, j * LANES:(j + 1) * LANES] = v[j * LANES:(j + 1) * LANES].T.astype(BF16)
    for j in range(n_keys // LANES - 1):
        vtb_s[:, j * LANES:(j + 1) * LANES] = v[CHUNK + j * LANES:CHUNK + (j + 1) * LANES].T.astype(BF16)
    vtb_s[:, n_keys - LANES:n_keys - CHUNK] = v[n_keys - CHUNK:n_keys].T.astype(BF16)

    krow = lax.broadcasted_iota(jnp.int32, (2 * KEYS, 1), 0)
    krow = jnp.where(krow >= KEYS, krow - KEYS, krow)

    def scores(c, g):
        r0 = c * CHUNK
        qbase = P_Q + g * 4 * LANES
        q = jnp.concatenate(
            [p_ref[r0:r0 + CHUNK, qbase + p * LANES:qbase + (p + 1) * LANES] for p in range(4)], axis=0)
        kk = jnp.concatenate([ke_s[g, r0:r0 + KEYS, :], ko_s[g, r0:r0 + KEYS, :]], axis=0)
        st = lax.dot_general(kk, q, NT, preferred_element_type=F32)
        if r0 < HALO:
            nbad = jnp.where(first, HALO - r0, 0)
            st = jnp.where(krow < nbad, -jnp.inf, st)
        return st

    def finish(c, g, st):
        r0 = c * CHUNK
        vt_s, v0 = (vta_s, r0) if c % 2 == 0 else (vtb_s, r0 - CHUNK)
        vt = vt_s[g * HEAD_DIM:(g + 1) * HEAD_DIM, v0:v0 + KEYS]
        outs = []
        for par in range(2):
            s = st[par * KEYS:(par + 1) * KEYS]
            sc = sink_ref[2 * g + par:2 * g + par + 1, :]
            m = jnp.maximum(jnp.max(s, axis=0, keepdims=True), sc)
            p = jnp.exp2(s - m)
            l = jnp.sum(p, axis=0, keepdims=True) + jnp.exp2(sc - m)
            outs.append(jnp.dot(vt, p.astype(BF16), preferred_element_type=F32) / l)
        ot = jnp.concatenate(outs, axis=0)
        for p2 in range(2):
            blk = ot[:, p2 * LANES:(p2 + 1) * LANES].T
            for h in range(2):
                c0 = (g * 4 + 2 * p2 + h) * LANES
                attn_s[r0:r0 + CHUNK, c0:c0 + LANES] = blk[h * CHUNK:(h + 1) * CHUNK].astype(BF16)

    units = [(c, g) for c in range(tm // CHUNK) for g in range(N_KV_HEADS)]
    pending = [scores(*u) for u in units[:SCORES_AHEAD]]
    conv_blocks = []
    for n, (c, g) in enumerate(units):
        st = pending.pop(0)
        if n + SCORES_AHEAD < len(units):
            pending.append(scores(*units[n + SCORES_AHEAD]))
        finish(c, g, st)
        if n % 4 == 3:
            cb0 = (n // 4) * (D_MODEL // 4)
            conv_blocks.append(jnp.dot(p_ref[:, P_CI:P_CI + D_MODEL], wco_ref[0, :, cb0:cb0 + D_MODEL // 4],
                                       preferred_element_type=F32))

    conv_out = jnp.concatenate(conv_blocks, axis=1)
    attn_out = jnp.dot(attn_s[...], wao_ref[0], preferred_element_type=F32)
    gc = p_ref[:, P_G:P_G + D_MODEL].astype(F32)
    ga = p_ref[:, P_G + D_MODEL:P_G + 2 * D_MODEL].astype(F32)
    merged = (gc * conv_out + ga * attn_out).astype(BF16)
    h_new = x_ref[...] + jnp.dot(merged, wo_ref[0], preferred_element_type=F32)
    o_ref[...] = h_new
    _route_tile(tm, h_new, g2_ref, wt_ref, bt_ref, tri_ref, r_ref, rt_ref, cnt_ref, base_s)


def _mix(p, x2, sink_tab, w_co, w_ao, w_o, g2, wt_router, bt_router, seq, tm, layer):
    t = x2.shape[0]
    row = lambda i: (i, 0)
    fixed = lambda i: (0, 0)
    sq = (1, D_MODEL, D_MODEL)
    wsel = lambda i: (layer, 0, 0)
    n_keys = HALO + tm
    return pl.pallas_call(
        functools.partial(_mix_kernel, tm, seq // tm),
        grid=(t // tm,),
        in_specs=[
            pl.BlockSpec((tm, P_COLS), row),
            pl.BlockSpec((HALO, 2 * LANES), lambda i: (jnp.maximum(i * (tm // HALO) - 1, 0), P_KV // (2 * LANES))),
            pl.BlockSpec((tm, D_MODEL), row),
            pl.BlockSpec(sink_tab.shape, fixed),
            pl.BlockSpec(sq, wsel),
            pl.BlockSpec(sq, wsel),
            pl.BlockSpec(sq, wsel),
            pl.BlockSpec((1, D_MODEL), fixed),
            pl.BlockSpec((2 * LANES, D_MODEL), fixed),
            pl.BlockSpec((LANES, LANES), fixed),
            pl.BlockSpec((tm, tm), fixed),
        ],
        out_specs=[
            pl.BlockSpec((tm, D_MODEL), row),
            pl.BlockSpec((tm, LANES), row),
            pl.BlockSpec((8, tm), lambda i: (0, i)),
            pl.BlockSpec((LANES, LANES), fixed),
        ],
        out_shape=[
            jax.ShapeDtypeStruct((t, D_MODEL), F32),
            jax.ShapeDtypeStruct((t, LANES), F32),
            jax.ShapeDtypeStruct((8, t), F32),
            jax.ShapeDtypeStruct((LANES, LANES), F32),
        ],
        scratch_shapes=[
            pltpu.VMEM((N_KV_HEADS, n_keys, LANES), BF16),
            pltpu.VMEM((N_KV_HEADS, n_keys, LANES), BF16),
            pltpu.VMEM((2 * HEAD_DIM, n_keys), BF16),
            pltpu.VMEM((2 * HEAD_DIM, n_keys), BF16),
            pltpu.VMEM((tm, D_MODEL), BF16),
            pltpu.VMEM((LANES, LANES), F32),
        ],
        compiler_params=pltpu.CompilerParams(dimension_semantics=("arbitrary",), vmem_limit_bytes=VMEM_BIG),
        name="mix",
    )(p, p, x2, sink_tab, w_co, w_ao, w_o, g2, wt_router, bt_router, jnp.triu(jnp.ones((tm, tm), BF16), 1))


def _rows_copy(src_ref, dst_ref, sem, n):
    return pltpu.make_async_copy(src_ref.at[pl.ds(0, n)], dst_ref.at[pl.ds(0, n)], sem)


def _dispatch_kernel(tm, n_tiles, n_blocks, zm_ref, d_ref, h_ref, g_ref, xs_ref, hn_s, zbuf, zsem, sem):
    i = pl.program_id(0)
    blk_rows = MOE_BLOCK * ROW_WORDS

    @pl.when(i == 0)
    def _():
        zbuf[...] = jnp.zeros_like(zbuf)

        def zero_copy(b):
            r = pl.multiple_of(b * blk_rows, blk_rows)
            return pltpu.make_async_copy(zbuf, xs_ref.at[pl.ds(r, blk_rows)], zsem)

        def start(b, c):
            @pl.when(zm_ref[b] == 1)
            def _():
                zero_copy(b).start()
            return c

        def wait(b, c):
            @pl.when(zm_ref[b] == 1)
            def _():
                zero_copy(b).wait()
            return c

        lax.fori_loop(0, n_blocks, start, 0)
        lax.fori_loop(0, n_blocks, wait, 0)

    slot = i % 2
    _store_rows(hn_s, (slot,), tm, _rms(h_ref[...], g_ref[...]))

    def body(t, c):
        src = hn_s.at[slot, pl.ds(pl.multiple_of(t * ROW_WORDS, ROW_WORDS), ROW_WORDS)]
        for k in range(2):
            d = pl.multiple_of(d_ref[k * tm + t] * ROW_WORDS, ROW_WORDS)
            pltpu.make_async_copy(src, xs_ref.at[pl.ds(d, ROW_WORDS)], sem.at[slot]).start(priority=k)
        return c

    lax.fori_loop(0, tm, body, 0, unroll=8)

    def drain(s):
        for _ in range(2):
            _rows_copy(hn_s.at[s], xs_ref, sem.at[s], tm * ROW_WORDS).wait()

    @pl.when(i > 0)
    def _():
        drain(1 - slot)

    @pl.when(i == n_tiles - 1)
    def _():
        drain(slot)


def _dispatch(h2, g2, dest, zero_mask, n_rows, tm):
    n_tiles = h2.shape[0] // tm
    n_blocks = n_rows // MOE_BLOCK
    return pl.pallas_call(
        functools.partial(_dispatch_kernel, tm, n_tiles, n_blocks),
        grid_spec=pltpu.PrefetchScalarGridSpec(
            num_scalar_prefetch=1,
            grid=(n_tiles,),
            in_specs=[
                pl.BlockSpec((2 * tm,), lambda i, zm: (i,), memory_space=pltpu.SMEM),
                pl.BlockSpec((tm, D_MODEL), lambda i, zm: (i, 0)),
                pl.BlockSpec((1, D_MODEL), lambda i, zm: (0, 0)),
            ],
            out_specs=pl.BlockSpec(memory_space=pl.ANY),
            scratch_shapes=[
                pltpu.VMEM((2, tm * ROW_WORDS, LANES), U32),
                pltpu.VMEM((MOE_BLOCK * ROW_WORDS, LANES), U32),
                pltpu.SemaphoreType.DMA(()),
                pltpu.SemaphoreType.DMA((2,)),
            ],
        ),
        out_shape=jax.ShapeDtypeStruct((n_rows * ROW_WORDS, LANES), U32),
        compiler_params=pltpu.CompilerParams(dimension_semantics=("arbitrary",)),
        name="dispatch",
    )(zero_mask, dest, h2, g2)


def _expert_kernel(layer, be_ref, nw_ref, nx_ref, sl_ref, nu_ref, x_ref, w1_hbm, w3_hbm, w2_hbm, o_ref,
                   w1_f, w3_f, w2_f, w1_s, w3_s, w2_s, sem):
    b0 = STEP_BLOCKS * pl.program_id(0)
    n_used = nu_ref[0]

    def fetch(e, slot):
        return (pltpu.make_async_copy(w1_hbm.at[layer, e], w1_f.at[slot], sem.at[slot]),
                pltpu.make_async_copy(w3_hbm.at[layer, e], w3_f.at[slot], sem.at[slot]),
                pltpu.make_async_copy(w2_hbm.at[layer, e], w2_f.at[slot], sem.at[slot]))

    def switch(b):
        @pl.when((b < n_used) & (nw_ref[b] == 1))
        def _():
            slot = sl_ref[b]
            stage = slot % 2

            @pl.when(b == 0)
            def _():
                for c in fetch(be_ref[0], 0):
                    c.start()

            for c in fetch(be_ref[b], stage):
                c.wait()
            w1_s[slot] = w1_f[stage].astype(BF16)
            w3_s[slot] = w3_f[stage].astype(BF16)
            w2_s[slot] = w2_f[stage].astype(BF16)

            @pl.when(nx_ref[b] >= 0)
            def _():
                for c in fetch(nx_ref[b], 1 - stage):
                    c.start()

    @pl.when(b0 < n_used)
    def _():
        for k in range(STEP_BLOCKS):
            switch(b0 + k)
        nb = STEP_BLOCKS
        slots = [sl_ref[b0 + k] for k in range(nb)]
        xs = [_load_rows(x_ref, (), k * MOE_BLOCK, MOE_BLOCK).astype(BF16) for k in range(nb)]
        h1 = [jnp.dot(xs[k], w1_s[slots[k]], preferred_element_type=F32) for k in range(nb)]
        h3 = [jnp.dot(xs[k], w3_s[slots[k]], preferred_element_type=F32) for k in range(nb)]
        for k in range(nb):
            hid = (h1[k] * jax.nn.sigmoid(h1[k]) * h3[k]).astype(BF16)
            y = jnp.dot(hid, w2_s[slots[k]], preferred_element_type=F32)
            for j in range(ROW_WORDS):
                w = _pack_pair(y[:, j * LANES:(j + 1) * LANES], y[:, HALF + j * LANES:HALF + (j + 1) * LANES])
                o_ref[pl.ds(k * MOE_BLOCK * ROW_WORDS + j, MOE_BLOCK, stride=ROW_WORDS), :] = w

    @pl.when(b0 >= n_used)
    def _():
        o_ref[...] = jnp.zeros_like(o_ref)


def _experts(xs, plan, w1, w3, w2, layer):
    n_rows = xs.shape[0] // ROW_WORDS
    blk = (STEP_BLOCKS * MOE_BLOCK * ROW_WORDS, LANES)
    any_spec = pl.BlockSpec(memory_space=pl.ANY)
    return pl.pallas_call(
        functools.partial(_expert_kernel, layer),
        grid_spec=pltpu.PrefetchScalarGridSpec(
            num_scalar_prefetch=5,
            grid=(n_rows // (STEP_BLOCKS * MOE_BLOCK),),
            in_specs=[
                pl.BlockSpec(blk, lambda p, be, nw, nx, sl, nu: (jnp.minimum(p, (nu[0] - 1) // STEP_BLOCKS), 0)),
                any_spec, any_spec, any_spec,
            ],
            out_specs=pl.BlockSpec(blk, lambda p, be, nw, nx, sl, nu: (p, 0)),
            scratch_shapes=[
                pltpu.VMEM((2, D_MODEL, EXPERT_FF), F32),
                pltpu.VMEM((2, D_MODEL, EXPERT_FF), F32),
                pltpu.VMEM((2, EXPERT_FF, D_MODEL), F32),
                pltpu.VMEM((W_SLOTS, D_MODEL, EXPERT_FF), BF16),
                pltpu.VMEM((W_SLOTS, D_MODEL, EXPERT_FF), BF16),
                pltpu.VMEM((W_SLOTS, EXPERT_FF, D_MODEL), BF16),
                pltpu.SemaphoreType.DMA((2,)),
            ],
        ),
        out_shape=jax.ShapeDtypeStruct(xs.shape, U32),
        compiler_params=pltpu.CompilerParams(dimension_semantics=("arbitrary",), vmem_limit_bytes=VMEM_BIG),
        name="experts",
    )(*plan, xs, w1, w3, w2)


def _issue_row_gathers(tm, d_ref, ys_ref, ybuf, sem, slot, unrolled):
    def one(t):
        for k in range(2):
            d = pl.multiple_of(d_ref[k * tm + t] * ROW_WORDS, ROW_WORDS)
            r = (k * tm + t) * ROW_WORDS
            if not isinstance(r, int):
                r = pl.multiple_of(r, ROW_WORDS)
            pltpu.make_async_copy(
                ys_ref.at[pl.ds(d, ROW_WORDS)], ybuf.at[slot, pl.ds(r, ROW_WORDS)], sem.at[slot]).start(priority=k)

    if unrolled:
        for t in range(tm):
            one(t)
    else:
        def body(t, c):
            one(t)
            return c

        lax.fori_loop(0, tm, body, 0, unroll=8)


def _combined_tile(tm, r_ref, h_ref, ys_ref, ybuf, sem, slot):
    _rows_copy(ys_ref, ybuf.at[slot], sem.at[slot], 2 * tm * ROW_WORDS).wait()
    r = r_ref[...]
    w0 = r[:, R_W0:R_W0 + 1]
    w1 = r[:, R_W1:R_W1 + 1]
    return h_ref[...] + (w0 * _load_rows(ybuf, (slot,), 0, tm) + w1 * _load_rows(ybuf, (slot,), tm, tm))


def _combine_kernel(tm, n_tiles, dcur_ref, dnext_ref, r_ref, h_ref, fg_ref, ys_ref, o_ref, ybuf, sem):
    i = pl.program_id(0)

    @pl.when(i == 0)
    def _():
        _issue_row_gathers(tm, dcur_ref, ys_ref, ybuf, sem, 0, False)

    @pl.when(i + 1 < n_tiles)
    def _():
        _issue_row_gathers(tm, dnext_ref, ys_ref, ybuf, sem, (i + 1) % 2, False)

    o_ref[...] = _rms(_combined_tile(tm, r_ref, h_ref, ys_ref, ybuf, sem, i % 2), fg_ref[...])


def _combine_specs(tm, n_tiles):
    row = lambda i: (i, 0)
    return [
        pl.BlockSpec((2 * tm,), lambda i: (i,), memory_space=pltpu.SMEM),
        pl.BlockSpec((2 * tm,), lambda i: (jnp.minimum(i + 1, n_tiles - 1),), memory_space=pltpu.SMEM),
        pl.BlockSpec((tm, LANES), row),
        pl.BlockSpec((tm, D_MODEL), row),
    ]


def _combine_scratch(tm):
    return [pltpu.VMEM((2, 2 * tm * ROW_WORDS, LANES), U32), pltpu.SemaphoreType.DMA((2,))]


def _combine_final(ys, dest, route, h2, final_g, tm):
    t = h2.shape[0]
    n_tiles = t // tm
    return pl.pallas_call(
        functools.partial(_combine_kernel, tm, n_tiles),
        grid=(n_tiles,),
        in_specs=_combine_specs(tm, n_tiles)
        + [pl.BlockSpec((1, D_MODEL), lambda i: (0, 0)), pl.BlockSpec(memory_space=pl.ANY)],
        out_specs=pl.BlockSpec((tm, D_MODEL), lambda i: (i, 0)),
        out_shape=jax.ShapeDtypeStruct((t, D_MODEL), F32),
        scratch_shapes=_combine_scratch(tm),
        compiler_params=pltpu.CompilerParams(dimension_semantics=("arbitrary",)),
        name="combine",
    )(dest, dest, route, h2, final_g, ys)


def _combine_proj_kernel(tm, n_tiles, tiles_per_seq, dcur_ref, dnext_ref, r_ref, hm_ref, g_ref, w_ref, bg_ref,
                         rot_ref, cw_ref, ys_ref, o_ref, h_ref, ybuf, sem, carry_s):
    i = pl.program_id(0)
    slot = i % 2
    first = (i % tiles_per_seq) == 0

    @pl.when(i == 0)
    def _():
        carry_s[...] = jnp.zeros_like(carry_s)
        _issue_row_gathers(tm, dcur_ref, ys_ref, ybuf, sem, 0, False)

    h = _combined_tile(tm, r_ref, hm_ref, ys_ref, ybuf, sem, slot)
    h_ref[...] = h
    _issue_row_gathers(tm, dnext_ref, ys_ref, ybuf, sem, 1 - slot, True)
    _proj_body(_rms(h, g_ref[...]).astype(BF16), first, False, w_ref, bg_ref, rot_ref, cw_ref, o_ref, carry_s)

    @pl.when(i == n_tiles - 1)
    def _():
        _rows_copy(ys_ref, ybuf.at[1 - slot], sem.at[1 - slot], 2 * tm * ROW_WORDS).wait()


def _combine_proj(ys, dest, route, h2, g1, w_in, b_gate, rot, conv_w, seq, tm, layer):
    t = h2.shape[0]
    n_tiles = t // tm
    row = lambda i: (i, 0)
    return pl.pallas_call(
        functools.partial(_combine_proj_kernel, tm, n_tiles, seq // tm),
        grid=(n_tiles,),
        in_specs=_combine_specs(tm, n_tiles) + _proj_specs(w_in, conv_w, tm, layer)
        + [pl.BlockSpec(memory_space=pl.ANY)],
        out_specs=[pl.BlockSpec((tm, P_COLS), row), pl.BlockSpec((tm, D_MODEL), row)],
        out_shape=[jax.ShapeDtypeStruct((t, P_COLS), BF16), jax.ShapeDtypeStruct((t, D_MODEL), F32)],
        scratch_shapes=_combine_scratch(tm) + [pltpu.VMEM((8, D_MODEL), F32)],
        compiler_params=pltpu.CompilerParams(dimension_semantics=("arbitrary",), vmem_limit_bytes=VMEM_BIG),
        name="combine_proj",
    )(dest, dest, route, h2, g1, w_in, b_gate, rot, conv_w, ys)


def _rotary_table(positions):
    half = ROT_DIM // 2
    inv = ROPE_THETA ** (-jnp.arange(0, ROT_DIM, 2, dtype=F32) / ROT_DIM)
    ang = positions.reshape(-1).astype(F32)[:, None] * inv
    cs1 = jnp.concatenate([jnp.cos(ang), jnp.sin(ang), jnp.ones((ang.shape[0], 1), F32)], axis=1)
    lane = jnp.arange(LANES) % HEAD_DIM
    j = jnp.arange(2 * half + 1)[:, None]
    a = jnp.where(lane < half, j == lane, jnp.where(lane < ROT_DIM, j == lane - half, j == 2 * half))
    b = -((lane < half) & (j == half + lane)).astype(F32)
    c = ((lane >= half) & (lane < ROT_DIM) & (j == lane)).astype(F32)
    place = jnp.concatenate([a.astype(F32), b, c], axis=1)
    c1 = cs1.astype(BF16)
    r1 = cs1 - c1.astype(F32)
    c2 = r1.astype(BF16)
    c3 = (r1 - c2.astype(F32)).astype(BF16)
    pb = place.astype(BF16)
    return sum(jnp.dot(c, pb, preferred_element_type=F32) for c in (c1, c2, c3))


def _sink_table(sinks):
    s = (sinks.astype(F32) * LOG2E).reshape(N_KV_HEADS, 4, 2).transpose(0, 2, 1)
    s = jnp.broadcast_to(s[..., None], (N_KV_HEADS, 2, 4, CHUNK)).reshape(2 * N_KV_HEADS, 4 * CHUNK)
    return jnp.concatenate([s, jnp.zeros((8 - 2 * N_KV_HEADS, 4 * CHUNK), F32)], axis=0)


def _router_params(w_group, b_group, w_route, b_route):
    d = w_group.shape[0]
    pad = LANES - N_GROUPS - N_EXPERTS
    wt = jnp.concatenate([w_group.T, w_route.transpose(0, 2, 1).reshape(N_EXPERTS, d), jnp.zeros((pad, d), F32)], axis=0)
    hi = wt.astype(BF16)
    lo = (wt - hi.astype(F32)).astype(BF16)
    bias = jnp.concatenate([b_group, b_route.reshape(-1), jnp.zeros((pad,), F32)])
    return jnp.concatenate([hi, lo], axis=0), jnp.broadcast_to(bias[:, None], (LANES, LANES))


def _moe_plan(route_t, counts, n_tok, tm):
    n_rows = -(-(2 * n_tok) // MOE_BLOCK) * MOE_BLOCK + N_EXPERTS * MOE_BLOCK
    n_blocks = n_rows // MOE_BLOCK
    ids = jnp.arange(N_EXPERTS, dtype=jnp.int32)
    cnt = counts[R_ROW0:R_ROW0 + N_EXPERTS, 0].astype(jnp.int32)
    padded = (cnt + MOE_BLOCK - 1) // MOE_BLOCK * MOE_BLOCK
    pad_end = jnp.sum(jnp.where(ids[None, :] <= ids[:, None], padded[None, :], 0), axis=1)
    pad_start = pad_end - padded
    e = route_t[R_E0:R_E1 + 1].astype(jnp.int32)
    rank = route_t[R_RANK0:R_RANK1 + 1].astype(jnp.int32)
    dest = jnp.sum(jnp.where(e[..., None] == ids, pad_start, 0), axis=-1) + rank
    dest = dest.reshape(2, n_tok // tm, tm).transpose(1, 0, 2).reshape(-1)
    n_used = pad_end[-1] // MOE_BLOCK
    blk = jnp.arange(n_blocks, dtype=jnp.int32)
    be = jnp.minimum(jnp.sum((pad_end[None, :] <= blk[:, None] * MOE_BLOCK).astype(jnp.int32), axis=1), N_EXPERTS - 1)
    be_last = jnp.sum(jnp.where(blk == n_used - 1, be, 0))
    be = jnp.where(blk < n_used, be, be_last)
    new_expert = jnp.concatenate([jnp.ones((1,), jnp.int32), (be[1:] != be[:-1]).astype(jnp.int32)])
    later = blk[None, :] > blk[:, None]
    seq_slot = (jnp.sum(jnp.where(later.T | (blk[None, :] == blk[:, None]), new_expert[None, :], 0), axis=1) - 1) % W_SLOTS
    next_start = jnp.min(jnp.where(later & (new_expert[None, :] == 1), blk[None, :], n_blocks), axis=1)
    next_expert = jnp.sum(jnp.where(blk[None, :] == next_start[:, None], be[None, :], 0), axis=1)
    next_expert = jnp.where(next_start < n_blocks, next_expert, -1)
    last = pad_end // MOE_BLOCK - 1
    is_last = jnp.any((blk[:, None] == last[None, :]) & (padded[None, :] > 0), axis=1)
    zero_mask = (is_last | (blk >= n_used)).astype(jnp.int32)
    plan = (be, new_expert, next_expert.astype(jnp.int32), seq_slot.astype(jnp.int32), n_used.astype(jnp.int32).reshape(1))
    return n_rows, dest, plan, zero_mask


def kernel(x, positions, norm1_g, w_in, b_gate, conv_w, sinks, w_conv_out, w_attn_out, w_o, norm2_g, w_group, b_group, w_route, b_route, w1, w3, w2, final_g):
    b, s, d = x.shape
    t = b * s
    depth = w_in.shape[0]
    tm = ROW_TILE
    rot = _rotary_table(positions)
    h = x.reshape(t, d)
    ys = dest = route = None
    w_in_b, w_co_b, w_ao_b, w_o_b = (w.astype(BF16) for w in (w_in, w_conv_out, w_attn_out, w_o))
    for l in range(depth):
        if l == 0:
            p = _proj(h, norm1_g[l][None], w_in_b, b_gate[l][None], rot, conv_w[l], s, PROJ_TILE, l)
        else:
            p, h = _combine_proj(ys, dest, route, h, norm1_g[l][None], w_in_b, b_gate[l][None], rot, conv_w[l], s, tm, l)
        g2 = norm2_g[l][None]
        wt_router, bt_router = _router_params(w_group[l], b_group[l], w_route[l], b_route[l])
        h, route, route_t, counts = _mix(p, h, _sink_table(sinks[l]), w_co_b, w_ao_b, w_o_b, g2, wt_router, bt_router,
                                         s, tm, l)
        n_rows, dest, plan, zero_mask = _moe_plan(route_t, counts, t, tm)
        xs = _dispatch(h, g2, dest, zero_mask, n_rows, tm)
        ys = _experts(xs, plan, w1, w3, w2, l)
    return _combine_final(ys, dest, route, h, final_g[None], tm).reshape(b, s, d)
```

```python
import functools

import jax
import jax.numpy as jnp
from jax import lax
from jax.experimental import pallas as pl
from jax.experimental.pallas import tpu as pltpu

F32 = jnp.float32
BF16 = jnp.bfloat16
U32 = jnp.uint32

D_MODEL = 1024
HALF = D_MODEL // 2
RMS_EPS = 1e-5
CHUNK = 64
WINDOW_CHUNKS = 2
HEAD_DIM = 64
N_Q_HEADS = 16
N_KV_HEADS = 2
ROT_DIM = 16
ROPE_THETA = 500000.0
N_GROUPS = 8
EXPERTS_PER_GROUP = 8
N_EXPERTS = 64
EXPERT_FF = 512
MOE_BLOCK = 256
STEP_BLOCKS = 4
W_SLOTS = 4
LANES = 128

W_CB, W_CC, W_CX, W_Q, W_K, W_G = 0, 1024, 2048, 3072, 4096, 4352
P_CI, P_Q, P_KV, P_G = 0, 1024, 2048, 2304
P_COLS = 4352
HALO = WINDOW_CHUNKS * CHUNK
KEYS = HALO + CHUNK
LOG2E = 1.4426950408889634
Q_SCALE = HEAD_DIM ** -0.5 * LOG2E
SCORES_AHEAD = 4

ROW_TILE = 512
PROJ_TILE = 1024
VMEM_BIG = 56 * 1024 * 1024
NT = (((1,), (1,)), ((), ()))


def _rms(x, g):
    ms = jnp.mean(x * x, axis=-1, keepdims=True)
    return x * lax.rsqrt(ms + RMS_EPS) * g


def _pack_pair(lo, hi):
    def rnd(x):
        return lax.bitcast_convert_type(x, U32) + U32(0x8000)

    return (rnd(hi) & U32(0xFFFF0000)) | (rnd(lo) >> 16)


def _unpack_pair(w):
    lo = lax.bitcast_convert_type(w << 16, F32)
    hi = lax.bitcast_convert_type(w & U32(0xFFFF0000), F32)
    return lo, hi


ROW_WORDS = HALF // LANES


def _store_rows(ref, lead, n, x):
    for j in range(ROW_WORDS):
        w = _pack_pair(x[:, j * LANES:(j + 1) * LANES], x[:, HALF + j * LANES:HALF + (j + 1) * LANES])
        ref[lead + (pl.ds(j, n, stride=ROW_WORDS), slice(None))] = w


def _load_rows(ref, lead, row0, n):
    parts = [_unpack_pair(ref[lead + (pl.ds(row0 * ROW_WORDS + j, n, stride=ROW_WORDS), slice(None))])
             for j in range(ROW_WORDS)]
    return jnp.concatenate([p[0] for p in parts] + [p[1] for p in parts], axis=1)


def _proj_kernel(tiles_per_seq, x_ref, g_ref, w_ref, bg_ref, rot_ref, cw_ref, o_ref, carry_s):
    first = (pl.program_id(0) % tiles_per_seq) == 0

    @pl.when(pl.program_id(0) == 0)
    def _():
        carry_s[...] = jnp.zeros_like(carry_s)

    _proj_body(_rms(x_ref[...], g_ref[...]).astype(BF16), first, True, w_ref, bg_ref, rot_ref, cw_ref, o_ref, carry_s)


def _proj_body(xn, first, gates_first, w_ref, bg_ref, rot_ref, cw_ref, o_ref, carry_s):
    tm = xn.shape[0]

    def mm(c0, n):
        return jnp.dot(xn, w_ref[0, :, c0:c0 + n], preferred_element_type=F32)

    ra = rot_ref[:, 0:LANES]
    rb = rot_ref[:, LANES:2 * LANES]
    rc = rot_ref[:, 2 * LANES:3 * LANES]

    def rot(t):
        return t * ra + pltpu.roll(t, LANES - ROT_DIM // 2, 1) * rb + pltpu.roll(t, ROT_DIM // 2, 1) * rc

    nc = 512
    row = lax.broadcasted_iota(jnp.int32, (tm, 1), 0)
    cw = cw_ref[...]

    def gates():
        for j in range(4):
            gt = mm(W_G + j * nc, nc) + bg_ref[:, j * nc:(j + 1) * nc]
            o_ref[:, P_G + j * nc:P_G + (j + 1) * nc] = jax.nn.sigmoid(gt).astype(BF16)

    def queries():
        for j in range(2):
            q = mm(W_Q + j * nc, nc)
            for p in range(nc // LANES):
                t = rot(q[:, p * LANES:(p + 1) * LANES]) * Q_SCALE
                o_ref[:, P_Q + j * nc + p * LANES:P_Q + j * nc + (p + 1) * LANES] = t.astype(BF16)

    def conv():
        for j in range(2):
            cs = slice(j * nc, (j + 1) * nc)
            u = mm(W_CB + j * nc, nc) * mm(W_CX + j * nc, nc)
            prev = carry_s[:, cs]
            h1 = jnp.where(first, 0.0, prev[7:8])
            h2 = jnp.where(first, 0.0, prev[6:7])
            s1 = jnp.where(row == 0, h1, pltpu.roll(u, 1, 0))
            s2 = jnp.where(row == 0, h2, jnp.where(row == 1, h1, pltpu.roll(u, 2, 0)))
            y = cw[2:3, cs] * u + cw[1:2, cs] * s1 + cw[0:1, cs] * s2
            carry_s[:, cs] = u[tm - 8:tm]
            o_ref[:, P_CI + j * nc:P_CI + (j + 1) * nc] = (mm(W_CC + j * nc, nc) * y).astype(BF16)

    def keys_values():
        kv = mm(W_K, 2 * LANES)
        o_ref[:, P_KV:P_KV + LANES] = rot(kv[:, :LANES]).astype(BF16)
        o_ref[:, P_KV + LANES:P_KV + 2 * LANES] = kv[:, LANES:].astype(BF16)

    for group in (gates, queries, conv, keys_values) if gates_first else (conv, queries, keys_values, gates):
        group()


def _proj_specs(w_in, conv_w, tm, layer):
    fixed = lambda i: (0, 0)
    return [
        pl.BlockSpec((1, D_MODEL), fixed),
        pl.BlockSpec((1,) + w_in.shape[1:], lambda i: (layer, 0, 0), pipeline_mode=pl.Buffered(1)),
        pl.BlockSpec((1, 2 * D_MODEL), fixed),
        pl.BlockSpec((tm, 3 * LANES), lambda i: (i, 0)),
        pl.BlockSpec(conv_w.shape, fixed),
    ]


def _proj(x2, g1, w_in, b_gate, rot, conv_w, seq, tm, layer):
    t = x2.shape[0]
    row = lambda i: (i, 0)
    return pl.pallas_call(
        functools.partial(_proj_kernel, seq // tm),
        grid=(t // tm,),
        in_specs=[pl.BlockSpec((tm, D_MODEL), row)] + _proj_specs(w_in, conv_w, tm, layer),
        out_specs=pl.BlockSpec((tm, P_COLS), row),
        out_shape=jax.ShapeDtypeStruct((t, P_COLS), BF16),
        scratch_shapes=[pltpu.VMEM((8, D_MODEL), F32)],
        compiler_params=pltpu.CompilerParams(dimension_semantics=("arbitrary",), vmem_limit_bytes=VMEM_BIG),
        name="proj",
    )(x2, g1, w_in, b_gate, rot, conv_w)


R_E0, R_E1, R_RANK0, R_RANK1, R_W0, R_W1 = range(6)
R_ROW0 = N_GROUPS


def _route_tile(tm, h, g_ref, wt_ref, bt_ref, tri_ref, r_ref, rt_ref, cnt_ref, base_s):
    hn = _rms(h, g_ref[...])
    hi = hn.astype(BF16)
    lo = (hn - hi.astype(F32)).astype(BF16)
    both = lax.dot_general(wt_ref[...], hi, NT, preferred_element_type=F32)
    lt = (both[:LANES] + both[LANES:] + lax.dot_general(wt_ref[0:LANES, :], lo, NT, preferred_element_type=F32)
          + bt_ref[:, 0:1])
    row = lax.broadcasted_iota(jnp.int32, (LANES, tm), 0)
    neg = -jnp.inf

    def first_max(v):
        m = jnp.max(v, axis=0, keepdims=True)
        return m, jnp.min(jnp.where(v == m, row, LANES), axis=0, keepdims=True)

    gl = jnp.where(row < N_GROUPS, lt, neg)
    gmax, gtop = first_max(gl)
    pg = 1.0 / jnp.sum(jnp.exp(gl - gmax), axis=0, keepdims=True)
    in_group = (row >= R_ROW0) & (row < R_ROW0 + N_EXPERTS) & (((row - R_ROW0) >> 3) == gtop)
    el = jnp.where(in_group, lt, neg)
    m1, i1 = first_max(el)
    m2, i2 = first_max(jnp.where(row == i1, neg, el))
    e2 = jnp.exp(m2 - m1)
    den = 1.0 + e2
    w0 = pg * (1.0 / den)
    w1 = pg * (e2 / den)
    hit0 = row == i1
    hit1 = row == i2
    a = jnp.where(hit0 | hit1, 1.0, 0.0)
    before = jnp.dot(a.astype(BF16), tri_ref[...], preferred_element_type=F32) + base_s[:, 0:1]
    rank0 = jnp.sum(jnp.where(hit0, before, 0.0), axis=0, keepdims=True)
    rank1 = jnp.sum(jnp.where(hit1, before, 0.0), axis=0, keepdims=True)
    base_s[...] = base_s[...] + jnp.sum(a, axis=1, keepdims=True)
    cnt_ref[...] = base_s[...]
    vals = ((i1 - R_ROW0).astype(F32), (i2 - R_ROW0).astype(F32), rank0, rank1, w0, w1)
    slab = jnp.zeros((LANES, tm), F32)
    for k, v in enumerate(vals):
        slab = jnp.where(row == k, v, slab)
    r_ref[...] = slab.T
    rt_ref[...] = slab[0:8]


def _mix_kernel(tm, tiles_per_seq, p_ref, hkv_ref, x_ref, sink_ref, wco_ref, wao_ref, wo_ref,
                g2_ref, wt_ref, bt_ref, tri_ref, o_ref, r_ref, rt_ref, cnt_ref, ke_s, ko_s, vta_s, vtb_s, attn_s, base_s):
    i = pl.program_id(0)
    first = (i % tiles_per_seq) == 0
    lo = lax.broadcasted_iota(jnp.int32, (1, LANES), 1) < HEAD_DIM

    @pl.when(i == 0)
    def _():
        base_s[...] = jnp.zeros_like(base_s)

    kv_all = jnp.concatenate([hkv_ref[...], p_ref[:, P_KV:P_KV + 2 * LANES]], axis=0)
    k = kv_all[:, :LANES].astype(F32)
    kr = pltpu.roll(k, HEAD_DIM, 1)
    ke_s[0] = jnp.where(lo, k, 0.0).astype(BF16)
    ko_s[0] = jnp.where(lo, 0.0, kr).astype(BF16)
    ke_s[1] = jnp.where(lo, kr, 0.0).astype(BF16)
    ko_s[1] = jnp.where(lo, 0.0, k).astype(BF16)
    v = kv_all[:, LANES:].astype(F32)
    n_keys = HALO + tm
    for j in range(n_keys // LANES):
        vta_s[:, j * LANES:(j + 1) * LANES] = v[j * LANES:(j + 1) * LANES].T.astype(BF16)
    for j in range(n_keys // LANES - 1):
        vtb_s[:, j * LANES:(j + 1) * LANES] = v[CHUNK + j * LANES:CHUNK + (j + 1) * LANES].T.astype(BF16)
    vtb_s[:, n_keys - LANES:n_keys - CHUNK] = v[n_keys - CHUNK:n_keys].T.astype(BF16)

    krow = lax.broadcasted_iota(jnp.int32, (2 * KEYS, 1), 0)
    krow = jnp.where(krow >= KEYS, krow - KEYS, krow)

    def scores(c, g):
        r0 = c * CHUNK
        qbase = P_Q + g * 4 * LANES
        q = jnp.concatenate(
            [p_ref[r0:r0 + CHUNK, qbase + p * LANES:qbase + (p + 1) * LANES] for p in range(4)], axis=0)
        kk = jnp.concatenate([ke_s[g, r0:r0 + KEYS, :], ko_s[g, r0:r0 + KEYS, :]], axis=0)
        st = lax.dot_general(kk, q, NT, preferred_element_type=F32)
        if r0 < HALO:
            nbad = jnp.where(first, HALO - r0, 0)
            st = jnp.where(krow < nbad, -jnp.inf, st)
        return st

    def finish(c, g, st):
        r0 = c * CHUNK
        vt_s, v0 = (vta_s, r0) if c % 2 == 0 else (vtb_s, r0 - CHUNK)
        vt = vt_s[g * HEAD_DIM:(g + 1) * HEAD_DIM, v0:v0 + KEYS]
        outs = []
        for par in range(2):
            s = st[par * KEYS:(par + 1) * KEYS]
            sc = sink_ref[2 * g + par:2 * g + par + 1, :]
            m = jnp.maximum(jnp.max(s, axis=0, keepdims=True), sc)
            p = jnp.exp2(s - m)
            l = jnp.sum(p, axis=0, keepdims=True) + jnp.exp2(sc - m)
            outs.append(jnp.dot(vt, p.astype(BF16), preferred_element_type=F32) / l)
        ot = jnp.concatenate(outs, axis=0)
        for p2 in range(2):
            blk = ot[:, p2 * LANES:(p2 + 1) * LANES].T
            for h in range(2):
                c0 = (g * 4 + 2 * p2 + h) * LANES
                attn_s[r0:r0 + CHUNK, c0:c0 + LANES] = blk[h * CHUNK:(h + 1) * CHUNK].astype(BF16)

    units = [(c, g) for c in range(tm // CHUNK) for g in range(N_KV_HEADS)]
    pending = [scores(*u) for u in units[:SCORES_AHEAD]]
    conv_blocks = []
    for n, (c, g) in enumerate(units):
        st = pending.pop(0)
        if n + SCORES_AHEAD < len(units):
            pending.append(scores(*units[n + SCORES_AHEAD]))
        finish(c, g, st)
        if n % 4 == 3:
            cb0 = (n // 4) * (D_MODEL // 4)
            conv_blocks.append(jnp.dot(p_ref[:, P_CI:P_CI + D_MODEL], wco_ref[0, :, cb0:cb0 + D_MODEL // 4],
                                       preferred_element_type=F32))

    conv_out = jnp.concatenate(conv_blocks, axis=1)
    attn_out = jnp.dot(attn_s[...], wao_ref[0], preferred_element_type=F32)
    gc = p_ref[:, P_G:P_G + D_MODEL].astype(F32)
    ga = p_ref[:, P_G + D_MODEL:P_G + 2 * D_MODEL].astype(F32)
    merged = (gc * conv_out + ga * attn_out).astype(BF16)
    h_new = x_ref[...] + jnp.dot(merged, wo_ref[0], preferred_element_type=F32)
    o_ref[...] = h_new
    _route_tile(tm, h_new, g2_ref, wt_ref, bt_ref, tri_ref, r_ref, rt_ref, cnt_ref, base_s)


def _mix(p, x2, sink_tab, w_co, w_ao, w_o, g2, wt_router, bt_router, seq, tm, layer):
    t = x2.shape[0]
    row = lambda i: (i, 0)
    fixed = lambda i: (0, 0)
    sq = (1, D_MODEL, D_MODEL)
    wsel = lambda i: (layer, 0, 0)
    n_keys = HALO + tm
    return pl.pallas_call(
        functools.partial(_mix_kernel, tm, seq // tm),
        grid=(t // tm,),
        in_specs=[
            pl.BlockSpec((tm, P_COLS), row),
            pl.BlockSpec((HALO, 2 * LANES), lambda i: (jnp.maximum(i * (tm // HALO) - 1, 0), P_KV // (2 * LANES))),
            pl.BlockSpec((tm, D_MODEL), row),
            pl.BlockSpec(sink_tab.shape, fixed),
            pl.BlockSpec(sq, wsel),
            pl.BlockSpec(sq, wsel),
            pl.BlockSpec(sq, wsel),
            pl.BlockSpec((1, D_MODEL), fixed),
            pl.BlockSpec((2 * LANES, D_MODEL), fixed),
            pl.BlockSpec((LANES, LANES), fixed),
            pl.BlockSpec((tm, tm), fixed),
        ],
        out_specs=[
            pl.BlockSpec((tm, D_MODEL), row),
            pl.BlockSpec((tm, LANES), row),
            pl.BlockSpec((8, tm), lambda i: (0, i)),
            pl.BlockSpec((LANES, LANES), fixed),
        ],
        out_shape=[
            jax.ShapeDtypeStruct((t, D_MODEL), F32),
            jax.ShapeDtypeStruct((t, LANES), F32),
            jax.ShapeDtypeStruct((8, t), F32),
            jax.ShapeDtypeStruct((LANES, LANES), F32),
        ],
        scratch_shapes=[
            pltpu.VMEM((N_KV_HEADS, n_keys, LANES), BF16),
            pltpu.VMEM((N_KV_HEADS, n_keys, LANES), BF16),
            pltpu.VMEM((2 * HEAD_DIM, n_keys), BF16),
            pltpu.VMEM((2 * HEAD_DIM, n_keys), BF16),
            pltpu.VMEM((tm, D_MODEL), BF16),
            pltpu.VMEM((LANES, LANES), F32),
        ],
        compiler_params=pltpu.CompilerParams(dimension_semantics=("arbitrary",), vmem_limit_bytes=VMEM_BIG),
        name="mix",
    )(p, p, x2, sink_tab, w_co, w_ao, w_o, g2, wt_router, bt_router, jnp.triu(jnp.ones((tm, tm), BF16), 1))


def _rows_copy(src_ref, dst_ref, sem, n):
    return pltpu.make_async_copy(src_ref.at[pl.ds(0, n)], dst_ref.at[pl.ds(0, n)], sem)


def _dispatch_kernel(tm, n_tiles, n_blocks, zm_ref, d_ref, h_ref, g_ref, xs_ref, hn_s, zbuf, zsem, sem):
    i = pl.program_id(0)
    blk_rows = MOE_BLOCK * ROW_WORDS

    @pl.when(i == 0)
    def _():
        zbuf[...] = jnp.zeros_like(zbuf)

        def zero_copy(b):
            r = pl.multiple_of(b * blk_rows, blk_rows)
            return pltpu.make_async_copy(zbuf, xs_ref.at[pl.ds(r, blk_rows)], zsem)

        def start(b, c):
            @pl.when(zm_ref[b] == 1)
            def _():
                zero_copy(b).start()
            return c

        def wait(b, c):
            @pl.when(zm_ref[b] == 1)
            def _():
                zero_copy(b).wait()
            return c

        lax.fori_loop(0, n_blocks, start, 0)
        lax.fori_loop(0, n_blocks, wait, 0)

    slot = i % 2
    _store_rows(hn_s, (slot,), tm, _rms(h_ref[...], g_ref[...]))

    def body(t, c):
        src = hn_s.at[slot, pl.ds(pl.multiple_of(t * ROW_WORDS, ROW_WORDS), ROW_WORDS)]
        for k in range(2):
            d = pl.multiple_of(d_ref[k * tm + t] * ROW_WORDS, ROW_WORDS)
            pltpu.make_async_copy(src, xs_ref.at[pl.ds(d, ROW_WORDS)], sem.at[slot]).start(priority=k)
        return c

    lax.fori_loop(0, tm, body, 0, unroll=8)

    def drain(s):
        for _ in range(2):
            _rows_copy(hn_s.at[s], xs_ref, sem.at[s], tm * ROW_WORDS).wait()

    @pl.when(i > 0)
    def _():
        drain(1 - slot)

    @pl.when(i == n_tiles - 1)
    def _():
        drain(slot)


def _dispatch(h2, g2, dest, zero_mask, n_rows, tm):
    n_tiles = h2.shape[0] // tm
    n_blocks = n_rows // MOE_BLOCK
    return pl.pallas_call(
        functools.partial(_dispatch_kernel, tm, n_tiles, n_blocks),
        grid_spec=pltpu.PrefetchScalarGridSpec(
            num_scalar_prefetch=1,
            grid=(n_tiles,),
            in_specs=[
                pl.BlockSpec((2 * tm,), lambda i, zm: (i,), memory_space=pltpu.SMEM),
                pl.BlockSpec((tm, D_MODEL), lambda i, zm: (i, 0)),
                pl.BlockSpec((1, D_MODEL), lambda i, zm: (0, 0)),
            ],
            out_specs=pl.BlockSpec(memory_space=pl.ANY),
            scratch_shapes=[
                pltpu.VMEM((2, tm * ROW_WORDS, LANES), U32),
                pltpu.VMEM((MOE_BLOCK * ROW_WORDS, LANES), U32),
                pltpu.SemaphoreType.DMA(()),
                pltpu.SemaphoreType.DMA((2,)),
            ],
        ),
        out_shape=jax.ShapeDtypeStruct((n_rows * ROW_WORDS, LANES), U32),
        compiler_params=pltpu.CompilerParams(dimension_semantics=("arbitrary",)),
        name="dispatch",
    )(zero_mask, dest, h2, g2)


def _expert_kernel(layer, be_ref, nw_ref, nx_ref, sl_ref, nu_ref, x_ref, w1_hbm, w3_hbm, w2_hbm, o_ref,
                   w1_f, w3_f, w2_f, w1_s, w3_s, w2_s, sem):
    b0 = STEP_BLOCKS * pl.program_id(0)
    n_used = nu_ref[0]

    def fetch(e, slot):
        return (pltpu.make_async_copy(w1_hbm.at[layer, e], w1_f.at[slot], sem.at[slot]),
                pltpu.make_async_copy(w3_hbm.at[layer, e], w3_f.at[slot], sem.at[slot]),
                pltpu.make_async_copy(w2_hbm.at[layer, e], w2_f.at[slot], sem.at[slot]))

    def switch(b):
        @pl.when((b < n_used) & (nw_ref[b] == 1))
        def _():
            slot = sl_ref[b]
            stage = slot % 2

            @pl.when(b == 0)
            def _():
                for c in fetch(be_ref[0], 0):
                    c.start()

            @pl.when(nx_ref[b] >= 0)
            def _():
                for c in fetch(nx_ref[b], 1 - stage):
                    c.start()

            for c in fetch(be_ref[b], stage):
                c.wait()
            w1_s[slot] = w1_f[stage].astype(BF16)
            w3_s[slot] = w3_f[stage].astype(BF16)
            w2_s[slot] = w2_f[stage].astype(BF16)

    @pl.when(b0 < n_used)
    def _():
        for k in range(STEP_BLOCKS):
            switch(b0 + k)
        nb = STEP_BLOCKS
        slots = [sl_ref[b0 + k] for k in range(nb)]
        xs = [_load_rows(x_ref, (), k * MOE_BLOCK, MOE_BLOCK).astype(BF16) for k in range(nb)]
        h1 = [jnp.dot(xs[k], w1_s[slots[k]], preferred_element_type=F32) for k in range(nb)]
        h3 = [jnp.dot(xs[k], w3_s[slots[k]], preferred_element_type=F32) for k in range(nb)]
        for k in range(nb):
            hid = (h1[k] * jax.nn.sigmoid(h1[k]) * h3[k]).astype(BF16)
            y = jnp.dot(hid, w2_s[slots[k]], preferred_element_type=F32)
            for j in range(ROW_WORDS):
                w = _pack_pair(y[:, j * LANES:(j + 1) * LANES], y[:, HALF + j * LANES:HALF + (j + 1) * LANES])
                o_ref[pl.ds(k * MOE_BLOCK * ROW_WORDS + j, MOE_BLOCK, stride=ROW_WORDS), :] = w

    @pl.when(b0 >= n_used)
    def _():
        o_ref[...] = jnp.zeros_like(o_ref)


def _experts(xs, plan, w1, w3, w2, layer):
    n_rows = xs.shape[0] // ROW_WORDS
    blk = (STEP_BLOCKS * MOE_BLOCK * ROW_WORDS, LANES)
    any_spec = pl.BlockSpec(memory_space=pl.ANY)
    return pl.pallas_call(
        functools.partial(_expert_kernel, layer),
        grid_spec=pltpu.PrefetchScalarGridSpec(
            num_scalar_prefetch=5,
            grid=(n_rows // (STEP_BLOCKS * MOE_BLOCK),),
            in_specs=[
                pl.BlockSpec(blk, lambda p, be, nw, nx, sl, nu: (jnp.minimum(p, (nu[0] - 1) // STEP_BLOCKS), 0)),
                any_spec, any_spec, any_spec,
            ],
            out_specs=pl.BlockSpec(blk, lambda p, be, nw, nx, sl, nu: (p, 0)),
            scratch_shapes=[
                pltpu.VMEM((2, D_MODEL, EXPERT_FF), F32),
                pltpu.VMEM((2, D_MODEL, EXPERT_FF), F32),
                pltpu.VMEM((2, EXPERT_FF, D_MODEL), F32),
                pltpu.VMEM((W_SLOTS, D_MODEL, EXPERT_FF), BF16),
                pltpu.VMEM((W_SLOTS, D_MODEL, EXPERT_FF), BF16),
                pltpu.VMEM((W_SLOTS, EXPERT_FF, D_MODEL), BF16),
                pltpu.SemaphoreType.DMA((2,)),
            ],
        ),
        out_shape=jax.ShapeDtypeStruct(xs.shape, U32),
        compiler_params=pltpu.CompilerParams(dimension_semantics=("arbitrary",), vmem_limit_bytes=VMEM_BIG),
        name="experts",
    )(*plan, xs, w1, w3, w2)


def _issue_row_gathers(tm, d_ref, ys_ref, ybuf, sem, slot, unrolled):
    def one(t):
        for k in range(2):
            d = pl.multiple_of(d_ref[k * tm + t] * ROW_WORDS, ROW_WORDS)
            r = (k * tm + t) * ROW_WORDS
            if not isinstance(r, int):
                r = pl.multiple_of(r, ROW_WORDS)
            pltpu.make_async_copy(
                ys_ref.at[pl.ds(d, ROW_WORDS)], ybuf.at[slot, pl.ds(r, ROW_WORDS)], sem.at[slot]).start(priority=k)

    if unrolled:
        for t in range(tm):
            one(t)
    else:
        def body(t, c):
            one(t)
            return c

        lax.fori_loop(0, tm, body, 0, unroll=8)


def _combined_tile(tm, r_ref, h_ref, ys_ref, ybuf, sem, slot):
    _rows_copy(ys_ref, ybuf.at[slot], sem.at[slot], 2 * tm * ROW_WORDS).wait()
    r = r_ref[...]
    w0 = r[:, R_W0:R_W0 + 1]
    w1 = r[:, R_W1:R_W1 + 1]
    return h_ref[...] + (w0 * _load_rows(ybuf, (slot,), 0, tm) + w1 * _load_rows(ybuf, (slot,), tm, tm))


def _combine_kernel(tm, n_tiles, dcur_ref, dnext_ref, r_ref, h_ref, fg_ref, ys_ref, o_ref, ybuf, sem):
    i = pl.program_id(0)

    @pl.when(i == 0)
    def _():
        _issue_row_gathers(tm, dcur_ref, ys_ref, ybuf, sem, 0, False)

    @pl.when(i + 1 < n_tiles)
    def _():
        _issue_row_gathers(tm, dnext_ref, ys_ref, ybuf, sem, (i + 1) % 2, False)

    o_ref[...] = _rms(_combined_tile(tm, r_ref, h_ref, ys_ref, ybuf, sem, i % 2), fg_ref[...])


def _combine_specs(tm, n_tiles):
    row = lambda i: (i, 0)
    return [
        pl.BlockSpec((2 * tm,), lambda i: (i,), memory_space=pltpu.SMEM),
        pl.BlockSpec((2 * tm,), lambda i: (jnp.minimum(i + 1, n_tiles - 1),), memory_space=pltpu.SMEM),
        pl.BlockSpec((tm, LANES), row),
        pl.BlockSpec((tm, D_MODEL), row),
    ]


def _combine_scratch(tm):
    return [pltpu.VMEM((2, 2 * tm * ROW_WORDS, LANES), U32), pltpu.SemaphoreType.DMA((2,))]


def _combine_final(ys, dest, route, h2, final_g, tm):
    t = h2.shape[0]
    n_tiles = t // tm
    return pl.pallas_call(
        functools.partial(_combine_kernel, tm, n_tiles),
        grid=(n_tiles,),
        in_specs=_combine_specs(tm, n_tiles)
        + [pl.BlockSpec((1, D_MODEL), lambda i: (0, 0)), pl.BlockSpec(memory_space=pl.ANY)],
        out_specs=pl.BlockSpec((tm, D_MODEL), lambda i: (i, 0)),
        out_shape=jax.ShapeDtypeStruct((t, D_MODEL), F32),
        scratch_shapes=_combine_scratch(tm),
        compiler_params=pltpu.CompilerParams(dimension_semantics=("arbitrary",)),
        name="combine",
    )(dest, dest, route, h2, final_g, ys)


def _combine_proj_kernel(tm, n_tiles, tiles_per_seq, dcur_ref, dnext_ref, r_ref, hm_ref, g_ref, w_ref, bg_ref,
                         rot_ref, cw_ref, ys_ref, o_ref, h_ref, ybuf, sem, carry_s):
    i = pl.program_id(0)
    slot = i % 2
    first = (i % tiles_per_seq) == 0

    @pl.when(i == 0)
    def _():
        carry_s[...] = jnp.zeros_like(carry_s)
        _issue_row_gathers(tm, dcur_ref, ys_ref, ybuf, sem, 0, False)

    h = _combined_tile(tm, r_ref, hm_ref, ys_ref, ybuf, sem, slot)
    h_ref[...] = h
    _issue_row_gathers(tm, dnext_ref, ys_ref, ybuf, sem, 1 - slot, True)
    _proj_body(_rms(h, g_ref[...]).astype(BF16), first, False, w_ref, bg_ref, rot_ref, cw_ref, o_ref, carry_s)

    @pl.when(i == n_tiles - 1)
    def _():
        _rows_copy(ys_ref, ybuf.at[1 - slot], sem.at[1 - slot], 2 * tm * ROW_WORDS).wait()


def _combine_proj(ys, dest, route, h2, g1, w_in, b_gate, rot, conv_w, seq, tm, layer):
    t = h2.shape[0]
    n_tiles = t // tm
    row = lambda i: (i, 0)
    return pl.pallas_call(
        functools.partial(_combine_proj_kernel, tm, n_tiles, seq // tm),
        grid=(n_tiles,),
        in_specs=_combine_specs(tm, n_tiles) + _proj_specs(w_in, conv_w, tm, layer)
        + [pl.BlockSpec(memory_space=pl.ANY)],
        out_specs=[pl.BlockSpec((tm, P_COLS), row), pl.BlockSpec((tm, D_MODEL), row)],
        out_shape=[jax.ShapeDtypeStruct((t, P_COLS), BF16), jax.ShapeDtypeStruct((t, D_MODEL), F32)],
        scratch_shapes=_combine_scratch(tm) + [pltpu.VMEM((8, D_MODEL), F32)],
        compiler_params=pltpu.CompilerParams(dimension_semantics=("arbitrary",), vmem_limit_bytes=VMEM_BIG),
        name="combine_proj",
    )(dest, dest, route, h2, g1, w_in, b_gate, rot, conv_w, ys)


def _rotary_table(positions):
    half = ROT_DIM // 2
    inv = ROPE_THETA ** (-jnp.arange(0, ROT_DIM, 2, dtype=F32) / ROT_DIM)
    ang = positions.reshape(-1).astype(F32)[:, None] * inv
    cs1 = jnp.concatenate([jnp.cos(ang), jnp.sin(ang), jnp.ones((ang.shape[0], 1), F32)], axis=1)
    lane = jnp.arange(LANES) % HEAD_DIM
    j = jnp.arange(2 * half + 1)[:, None]
    a = jnp.where(lane < half, j == lane, jnp.where(lane < ROT_DIM, j == lane - half, j == 2 * half))
    b = -((lane < half) & (j == half + lane)).astype(F32)
    c = ((lane >= half) & (lane < ROT_DIM) & (j == lane)).astype(F32)
    place = jnp.concatenate([a.astype(F32), b, c], axis=1)
    return jnp.dot(cs1, place, precision=lax.Precision.HIGHEST)


def _sink_table(sinks):
    s = (sinks.astype(F32) * LOG2E).reshape(N_KV_HEADS, 4, 2).transpose(0, 2, 1)
    s = jnp.broadcast_to(s[..., None], (N_KV_HEADS, 2, 4, CHUNK)).reshape(2 * N_KV_HEADS, 4 * CHUNK)
    return jnp.concatenate([s, jnp.zeros((8 - 2 * N_KV_HEADS, 4 * CHUNK), F32)], axis=0)


def _router_params(w_group, b_group, w_route, b_route):
    d = w_group.shape[0]
    pad = LANES - N_GROUPS - N_EXPERTS
    wt = jnp.concatenate([w_group.T, w_route.transpose(0, 2, 1).reshape(N_EXPERTS, d), jnp.zeros((pad, d), F32)], axis=0)
    hi = wt.astype(BF16)
    lo = (wt - hi.astype(F32)).astype(BF16)
    bias = jnp.concatenate([b_group, b_route.reshape(-1), jnp.zeros((pad,), F32)])
    return jnp.concatenate([hi, lo], axis=0), jnp.broadcast_to(bias[:, None], (LANES, LANES))


def _moe_plan(route_t, counts, n_tok, tm):
    n_rows = -(-(2 * n_tok) // MOE_BLOCK) * MOE_BLOCK + N_EXPERTS * MOE_BLOCK
    n_blocks = n_rows // MOE_BLOCK
    ids = jnp.arange(N_EXPERTS, dtype=jnp.int32)
    cnt = counts[R_ROW0:R_ROW0 + N_EXPERTS, 0].astype(jnp.int32)
    padded = (cnt + MOE_BLOCK - 1) // MOE_BLOCK * MOE_BLOCK
    pad_end = jnp.sum(jnp.where(ids[None, :] <= ids[:, None], padded[None, :], 0), axis=1)
    pad_start = pad_end - padded
    e = route_t[R_E0:R_E1 + 1].astype(jnp.int32)
    rank = route_t[R_RANK0:R_RANK1 + 1].astype(jnp.int32)
    dest = jnp.sum(jnp.where(e[..., None] == ids, pad_start, 0), axis=-1) + rank
    dest = dest.reshape(2, n_tok // tm, tm).transpose(1, 0, 2).reshape(-1)
    n_used = pad_end[-1] // MOE_BLOCK
    blk = jnp.arange(n_blocks, dtype=jnp.int32)
    be = jnp.minimum(jnp.sum((pad_end[None, :] <= blk[:, None] * MOE_BLOCK).astype(jnp.int32), axis=1), N_EXPERTS - 1)
    be_last = jnp.sum(jnp.where(blk == n_used - 1, be, 0))
    be = jnp.where(blk < n_used, be, be_last)
    new_expert = jnp.concatenate([jnp.ones((1,), jnp.int32), (be[1:] != be[:-1]).astype(jnp.int32)])
    later = blk[None, :] > blk[:, None]
    seq_slot = (jnp.sum(jnp.where(later.T | (blk[None, :] == blk[:, None]), new_expert[None, :], 0), axis=1) - 1) % W_SLOTS
    next_start = jnp.min(jnp.where(later & (new_expert[None, :] == 1), blk[None, :], n_blocks), axis=1)
    next_expert = jnp.sum(jnp.where(blk[None, :] == next_start[:, None], be[None, :], 0), axis=1)
    next_expert = jnp.where(next_start < n_blocks, next_expert, -1)
    last = pad_end // MOE_BLOCK - 1
    is_last = jnp.any((blk[:, None] == last[None, :]) & (padded[None, :] > 0), axis=1)
    zero_mask = (is_last | (blk >= n_used)).astype(jnp.int32)
    plan = (be, new_expert, next_expert.astype(jnp.int32), seq_slot.astype(jnp.int32), n_used.astype(jnp.int32).reshape(1))
    return n_rows, dest, plan, zero_mask


def kernel(x, positions, norm1_g, w_in, b_gate, conv_w, sinks, w_conv_out, w_attn_out, w_o, norm2_g, w_group, b_group, w_route, b_route, w1, w3, w2, final_g):
    b, s, d = x.shape
    t = b * s
    depth = w_in.shape[0]
    tm = ROW_TILE
    rot = _rotary_table(positions)
    h = x.reshape(t, d)
    ys = dest = route = None
    w_in_b, w_co_b, w_ao_b, w_o_b = (w.astype(BF16) for w in (w_in, w_conv_out, w_attn_out, w_o))
    for l in range(depth):
        if l == 0:
            p = _proj(h, norm1_g[l][None], w_in_b, b_gate[l][None], rot, conv_w[l], s, PROJ_TILE, l)
        else:
            p, h = _combine_proj(ys, dest, route, h, norm1_g[l][None], w_in_b, b_gate[l][None], rot, conv_w[l], s, tm, l)
        g2 = norm2_g[l][None]
        wt_router, bt_router = _router_params(w_group[l], b_group[l], w_route[l], b_route[l])
        h, route, route_t, counts = _mix(p, h, _sink_table(sinks[l]), w_co_b, w_ao_b, w_o_b, g2, wt_router, bt_router,
                                         s, tm, l)
        n_rows, dest, plan, zero_mask = _moe_plan(route_t, counts, t, tm)
        xs = _dispatch(h, g2, dest, zero_mask, n_rows, tm)
        ys = _experts(xs, plan, w1, w3, w2, l)
    return _combine_final(ys, dest, route, h, final_g[None], tm).reshape(b, s, d)
```
